```python
import math
import jax, jax.numpy as jnp
from jax import lax
import numpy as np

D_MODEL = 1024
BATCH = 4
SEQ = 8192
DEPTH = 1
DEC_BATCH = 128
DEC_SEQ = 1
PAST_LEN = 8192
PAGE_SIZE = 128

SGU_GROUPS = 4
SGU_GROUP_DIM = 128
SGU_DIM = SGU_GROUPS * SGU_GROUP_DIM
CHUNK = 128
NSA_HEADS = 8
NSA_KV = 2
NSA_REP = NSA_HEADS // NSA_KV
NSA_DH = 64
CMP_BLOCK = 32
CMP_STRIDE = 16
SEL_BLOCK = 64
SEL_TOPK = 16
N_LOCAL = 2
WINDOW = 512
Q_BLOCK = 128
MEM_TOKENS = 256
MEM_HEADS = 4
MEM_DH = 128
N_BRANCH = 3
N_EXPERTS = 32
TOP_K = 4
D_EXPERT = 1024
SWIGLU_LIMIT = 7.0
SWIGLU_ALPHA = 1.702
EPS = 1e-5
NSA_Q = NSA_HEADS * NSA_DH
NSA_KVW = NSA_KV * NSA_DH
MEM_Q = MEM_HEADS * MEM_DH
BRANCH_DIM = 512
D_IN = 2 * SGU_DIM + NSA_Q + 6 * NSA_KVW + 3 * NSA_HEADS + MEM_Q + N_BRANCH * D_MODEL

kernel_name = 'gated_sgu_nsa_memxattn_moe_step'


def rmsnorm(x, g):
    xf = x.astype(jnp.float32)
    y = xf * lax.rsqrt(jnp.mean(xf * xf, axis=-1, keepdims=True) + EPS)
    return (y * g.astype(jnp.float32)).astype(x.dtype)


def layernorm(x, g, b):
    xf = x.astype(jnp.float32)
    mu = jnp.mean(xf, axis=-1, keepdims=True)
    xc = xf - mu
    y = xc * lax.rsqrt(jnp.mean(xc * xc, axis=-1, keepdims=True) + EPS)
    return (y * g.astype(jnp.float32) + b.astype(jnp.float32)).astype(x.dtype)


def masked_probs(s, mask):
    s = jnp.where(mask, s.astype(jnp.float32), -jnp.inf)
    m = jnp.max(s, axis=-1, keepdims=True)
    m = jnp.where(jnp.isfinite(m), m, 0.0)
    p = jnp.exp(s - m)
    return p / jnp.maximum(jnp.sum(p, axis=-1, keepdims=True), 1e-30)


def _split_points():
    sizes = (SGU_DIM, SGU_DIM, NSA_Q, 6 * NSA_KVW, 3 * NSA_HEADS, MEM_Q, N_BRANCH * D_MODEL)
    return [int(s) for s in np.cumsum(sizes)[:-1]]


def branch_inputs(x, norm_g, w_in):
    a = rmsnorm(x, norm_g)
    u, v, q, kv, ng, mq, mg = jnp.split(a @ w_in, _split_points(), axis=-1)
    bn, t = x.shape[:2]
    return (u, v, q.reshape(bn, t, NSA_HEADS, NSA_DH),
            kv.reshape(bn, t, 6, NSA_KV, NSA_DH),
            jax.nn.sigmoid(ng).reshape(bn, t, NSA_HEADS, 3),
            mq.reshape(bn, t, MEM_HEADS, MEM_DH), mg)


def sgu_branch(u, v, ln_g, ln_b, w_s, b_s, n_chunks, rows):
    z1 = jax.nn.gelu(u)
    z2 = layernorm(jax.nn.gelu(v), ln_g, ln_b)
    vv = z2.reshape(u.shape[0], n_chunks, rows, SGU_GROUPS, SGU_GROUP_DIM)
    w = jnp.tril(w_s[:, :rows, :rows])
    mixed = jnp.einsum('gts,bcsgd->bctgd', w, vv) + b_s[:, :rows].T[None, None, :, :, None]
    return z1 * mixed.reshape(z1.shape), z2


def compress_mlp(k, w1, b1, w2):
    bn, length, g, dh = k.shape
    kk = k.transpose(0, 2, 1, 3).reshape(bn * g, length, dh)
    c = lax.conv_general_dilated(kk, w1, (CMP_STRIDE,), 'VALID', dimension_numbers=('NWC', 'WIO', 'NWC'))
    c = c.reshape(bn, g, -1, dh).transpose(0, 2, 1, 3)
    return jax.nn.gelu(c + b1) @ w2


def to_blocks(k):
    bn, length, g, dh = k.shape
    ns = -(-length // SEL_BLOCK)
    k = jnp.pad(k, ((0, 0), (0, ns * SEL_BLOCK - length), (0, 0), (0, 0)))
    return k.reshape(bn, ns, SEL_BLOCK, g, dh).transpose(0, 3, 1, 2, 4)


def cmp_to_sel(nc, ns):
    cs = jnp.arange(nc) * CMP_STRIDE
    ss = jnp.arange(ns) * SEL_BLOCK
    m = (cs[:, None] <= ss[None, :] + SEL_BLOCK - 1) & (cs[:, None] + CMP_BLOCK - 1 >= ss[None, :])
    return m.astype(jnp.float32)


def nsa_attend(q, gates, qpos, kc, vc, ks_blk, vs_blk, kw, vw, wpos):
    bn, tq = q.shape[:2]
    nc, ns = kc.shape[1], ks_blk.shape[2]
    scale = NSA_DH ** -0.5
    qg = q.reshape(bn, tq, NSA_KV, NSA_REP, NSA_DH)
    cend = jnp.arange(nc) * CMP_STRIDE + CMP_BLOCK - 1
    p_c = masked_probs(jnp.einsum('btgrd,bngd->bgrtn', qg, kc) * scale, cend[None, :] <= qpos[:, None])
    o_c = jnp.einsum('bgrtn,bngd->btgrd', p_c.astype(vc.dtype), vc)
    imp = jnp.einsum('bgrtn,nj->bgtj', p_c, cmp_to_sel(nc, ns))
    blk = jnp.arange(ns)[None, :]
    cur = (qpos // SEL_BLOCK)[:, None]
    valid = blk <= cur
    forced = valid & ((blk == 0) | (blk > cur - N_LOCAL))
    imp = jnp.where(forced, jnp.inf, jnp.where(valid, imp, -jnp.inf))
    k_sel = min(SEL_TOPK, ns)
    _, idx = lax.top_k(imp, k_sel)
    flat = idx.reshape(bn, NSA_KV, tq * k_sel)[..., None, None]
    ks = jnp.take_along_axis(ks_blk, flat, axis=2).reshape(bn, NSA_KV, tq, k_sel * SEL_BLOCK, NSA_DH)
    vs = jnp.take_along_axis(vs_blk, flat, axis=2).reshape(bn, NSA_KV, tq, k_sel * SEL_BLOCK, NSA_DH)
    spos = (idx[..., None] * SEL_BLOCK + jnp.arange(SEL_BLOCK)).reshape(bn, NSA_KV, tq, k_sel * SEL_BLOCK)
    smask = (spos <= qpos[None, None, :, None])[:, :, None]
    p_s = masked_probs(jnp.einsum('btgrd,bgtkd->bgrtk', qg, ks) * scale, smask)
    o_s = jnp.einsum('bgrtk,bgtkd->btgrd', p_s.astype(vs.dtype), vs)
    wmask = (wpos[None, :] <= qpos[:, None]) & (wpos[None, :] > qpos[:, None] - WINDOW) & (wpos[None, :] >= 0)
    p_w = masked_probs(jnp.einsum('btgrd,bwgd->bgrtw', qg, kw) * scale, wmask)
    o_w = jnp.einsum('bgrtw,bwgd->btgrd', p_w.astype(vw.dtype), vw)
    g = gates.reshape(bn, tq, NSA_KV, NSA_REP, 3)
    o = g[..., 0:1] * o_c + g[..., 1:2] * o_s + g[..., 2:3] * o_w
    return o.reshape(bn, tq, NSA_Q)


def nsa_prompt(q, gates, kv, w1, b1, w2):
    bn, length = q.shape[:2]
    kc = compress_mlp(kv[:, :, 0], w1[0], b1[0], w2[0])
    vc = compress_mlp(kv[:, :, 1], w1[1], b1[1], w2[1])
    ks_blk, vs_blk = to_blocks(kv[:, :, 2]), to_blocks(kv[:, :, 3])
    pad = ((0, 0), (WINDOW, 0), (0, 0), (0, 0))
    kwp, vwp = jnp.pad(kv[:, :, 4], pad), jnp.pad(kv[:, :, 5], pad)
    nb = length // Q_BLOCK
    qb = q.reshape(bn, nb, Q_BLOCK, NSA_HEADS, NSA_DH).transpose(1, 0, 2, 3, 4)
    gb = gates.reshape(bn, nb, Q_BLOCK, NSA_HEADS, 3).transpose(1, 0, 2, 3, 4)

    def one_block(args):
        i, qi, gi = args
        s0 = i * Q_BLOCK
        kw = lax.dynamic_slice_in_dim(kwp, s0, WINDOW + Q_BLOCK, axis=1)
        vw = lax.dynamic_slice_in_dim(vwp, s0, WINDOW + Q_BLOCK, axis=1)
        wpos = s0 - WINDOW + jnp.arange(WINDOW + Q_BLOCK)
        return nsa_attend(qi, gi, s0 + jnp.arange(Q_BLOCK), kc, vc, ks_blk, vs_blk, kw, vw, wpos)

    out = lax.map(one_block, (jnp.arange(nb), qb, gb))
    return out.transpose(1, 0, 2, 3).reshape(bn, length, NSA_Q)


def nsa_sample(q, gates, kv, past_rows, win_buf, w1, b1, w2):
    tq = q.shape[1]
    past = past_rows.shape[1]
    rows = jnp.concatenate([past_rows, kv[:, :, :4]], axis=1)
    kc = compress_mlp(rows[:, :, 0], w1[0], b1[0], w2[0])
    vc = compress_mlp(rows[:, :, 1], w1[1], b1[1], w2[1])
    ks_blk, vs_blk = to_blocks(rows[:, :, 2]), to_blocks(rows[:, :, 3])
    win = jnp.concatenate([win_buf, kv[:, :, 4:]], axis=1)
    wbuf = win_buf.shape[1]
    wpos = past - wbuf + jnp.arange(wbuf + tq)
    o = nsa_attend(q, gates, past + jnp.arange(tq), kc, vc, ks_blk, vs_blk, win[:, :, 0], win[:, :, 1], wpos)
    return o, win[:, tq:]


def mem_attend(q, mkv):
    s = jnp.einsum('bthd,bmhd->bhtm', q, mkv[:, :, 0]) * (MEM_DH ** -0.5)
    p = jax.nn.softmax(s.astype(jnp.float32), axis=-1).astype(q.dtype)
    return jnp.einsum('bhtm,bmhd->bthd', p, mkv[:, :, 1]).reshape(q.shape[0], q.shape[1], MEM_Q)


def merge(o_a, o_b, o_m, mg, w_branch, w_out):
    o = jnp.stack([o_a, o_b, o_m], axis=2)
    br = jnp.einsum('btkc,kcd->btkd', o, w_branch)
    g = jax.nn.sigmoid(mg).reshape(br.shape)
    return jnp.sum(g * br, axis=2) @ w_out


def moe(h, rw, rb, wgu, bgu, wd, bd):
    shp = h.shape
    xt = h.reshape(-1, shp[-1])
    logits = (xt @ rw).astype(jnp.float32) + rb.astype(jnp.float32)
    top_v, top_i = lax.top_k(logits, TOP_K)
    gate = jnp.sum(jax.nn.one_hot(top_i, N_EXPERTS, dtype=jnp.float32)
                   * jax.nn.softmax(top_v, axis=-1)[..., None], axis=1)
    acc = jnp.zeros(xt.shape, jnp.float32)
    for e in range(N_EXPERTS):
        gt, up = jnp.split(xt @ wgu[e] + bgu[e], 2, axis=-1)
        gt = jnp.minimum(gt, SWIGLU_LIMIT)
        up = jnp.clip(up, -SWIGLU_LIMIT, SWIGLU_LIMIT)
        y = ((up + 1) * gt * jax.nn.sigmoid(SWIGLU_ALPHA * gt)) @ wd[e] + bd[e]
        acc = acc + gate[:, e:e + 1] * y
    return acc.astype(h.dtype).reshape(shp)


def setup_inputs(seed: int = 0) -> dict:
    key = jax.random.key(seed)
    ks = jax.random.split(key, 32)
    f32 = jnp.float32

    def nrm(k, shape, scale=1.0):
        return jax.random.normal(k, shape, f32) * scale

    n_pages = PAST_LEN // PAGE_SIZE
    n_used = DEC_BATCH * n_pages
    n_pool = n_used + max(1, n_used // 4)
    wbuf = min(WINDOW, PAST_LEN)
    page_table = jax.random.permutation(ks[3], n_pool)[:n_used].reshape(DEC_BATCH, n_pages).astype(jnp.int32)
    return {
        'x_prompt': nrm(ks[0], (BATCH, SEQ, D_MODEL)),
        'x_sample': nrm(ks[1], (DEC_BATCH, DEC_SEQ, D_MODEL)),
        'mem_prompt': nrm(ks[2], (BATCH, MEM_TOKENS, D_MODEL)),
        'cache_nsa_kv': nrm(ks[4], (DEPTH, n_pool, PAGE_SIZE, 4, NSA_KV, NSA_DH)),
        'cache_nsa_win': nrm(ks[5], (DEPTH, DEC_BATCH, wbuf, 2, NSA_KV, NSA_DH)),
        'cache_mem_kv': nrm(ks[6], (DEPTH, DEC_BATCH, MEM_TOKENS, 2, MEM_HEADS, MEM_DH)),
        'page_table': page_table,
        'norm1_g': 1.0 + nrm(ks[7], (DEPTH, D_MODEL), 0.02),
        'w_in': nrm(ks[8], (DEPTH, D_MODEL, D_IN), D_MODEL ** -0.5),
        'sgu_ln_g': 1.0 + nrm(ks[9], (DEPTH, SGU_DIM), 0.02),
        'sgu_ln_b': nrm(ks[10], (DEPTH, SGU_DIM), 0.02),
        'sgu_w': nrm(ks[11], (DEPTH, SGU_GROUPS, CHUNK, CHUNK), CHUNK ** -0.5),
        'sgu_b': 1.0 + nrm(ks[12], (DEPTH, SGU_GROUPS, CHUNK), 0.02),
        'cmp_w1': nrm(ks[13], (DEPTH, 2, CMP_BLOCK, NSA_DH, NSA_DH), (CMP_BLOCK * NSA_DH) ** -0.5),
        'cmp_b1': nrm(ks[14], (DEPTH, 2, NSA_DH), 0.02),
        'cmp_w2': nrm(ks[15], (DEPTH, 2, NSA_DH, NSA_DH), NSA_DH ** -0.5),
        'w_mem_kv': nrm(ks[16], (DEPTH, D_MODEL, 2 * MEM_Q), D_MODEL ** -0.5),
        'w_branch': nrm(ks[17], (DEPTH, N_BRANCH, BRANCH_DIM, D_MODEL), BRANCH_DIM ** -0.5),
        'w_out': nrm(ks[18], (DEPTH, D_MODEL, D_MODEL), D_MODEL ** -0.5),
        'norm2_g': 1.0 + nrm(ks[19], (DEPTH, D_MODEL), 0.02),
        'router_w': nrm(ks[20], (DEPTH, D_MODEL, N_EXPERTS), D_MODEL ** -0.5),
        'router_b': nrm(ks[21], (DEPTH, N_EXPERTS), 0.01),
        'exp_w_gate_up': nrm(ks[22], (DEPTH, N_EXPERTS, D_MODEL, 2 * D_EXPERT), D_MODEL ** -0.5),
        'exp_b_gate_up': nrm(ks[23], (DEPTH, N_EXPERTS, 2 * D_EXPERT), 0.02),
        'exp_w_down': nrm(ks[24], (DEPTH, N_EXPERTS, D_EXPERT, D_MODEL), D_EXPERT ** -0.5),
        'exp_b_down': nrm(ks[25], (DEPTH, N_EXPERTS, D_MODEL), 0.02),
        'final_norm_g': 1.0 + nrm(ks[26], (D_MODEL,), 0.02),
    }


def reference(x_prompt, x_sample, mem_prompt, cache_nsa_kv, cache_nsa_win, cache_mem_kv, page_table,
              norm1_g, w_in, sgu_ln_g, sgu_ln_b, sgu_w, sgu_b, cmp_w1, cmp_b1, cmp_w2, w_mem_kv,
              w_branch, w_out, norm2_g, router_w, router_b, exp_w_gate_up, exp_b_gate_up,
              exp_w_down, exp_b_down, final_norm_g):
    bp, length = x_prompt.shape[:2]
    bs, tq = x_sample.shape[:2]
    past = page_table.shape[1] * cache_nsa_kv.shape[2]
    hp, hs = x_prompt, x_sample
    kvp_l, winp_l, memp_l, kvs_l, wins_l, vs_l = [], [], [], [], [], []
    for l in range(DEPTH):
        u, v, q, kv, ng, mq, mg = branch_inputs(hp, norm1_g[l], w_in[l])
        o_a, _ = sgu_branch(u, v, sgu_ln_g[l], sgu_ln_b[l], sgu_w[l], sgu_b[l], length // CHUNK, CHUNK)
        o_b = nsa_prompt(q, ng, kv, cmp_w1[l], cmp_b1[l], cmp_w2[l])
        mkv = (mem_prompt @ w_mem_kv[l]).reshape(bp, MEM_TOKENS, 2, MEM_HEADS, MEM_DH)
        o_m = mem_attend(mq, mkv)
        hp = hp + merge(o_a, o_b, o_m, mg, w_branch[l], w_out[l])
        hp = hp + moe(rmsnorm(hp, norm2_g[l]), router_w[l], router_b[l], exp_w_gate_up[l],
                      exp_b_gate_up[l], exp_w_down[l], exp_b_down[l])
        kvp_l.append(kv[:, :, :4])
        winp_l.append(kv[:, length - min(WINDOW, length):, 4:])
        memp_l.append(mkv)
        u, v, q, kv, ng, mq, mg = branch_inputs(hs, norm1_g[l], w_in[l])
        o_a, z2 = sgu_branch(u, v, sgu_ln_g[l], sgu_ln_b[l], sgu_w[l], sgu_b[l], 1, tq)
        past_rows = cache_nsa_kv[l][page_table].reshape(bs, past, 4, NSA_KV, NSA_DH)
        o_b, win_new = nsa_sample(q, ng, kv, past_rows, cache_nsa_win[l], cmp_w1[l], cmp_b1[l], cmp_w2[l])
        o_m = mem_attend(mq, cache_mem_kv[l])
        hs = hs + merge(o_a, o_b, o_m, mg, w_branch[l], w_out[l])
        hs = hs + moe(rmsnorm(hs, norm2_g[l]), router_w[l], router_b[l], exp_w_gate_up[l],
                      exp_b_gate_up[l], exp_w_down[l], exp_b_down[l])
        kvs_l.append(kv[:, :, :4])
        wins_l.append(win_new)
        vs_l.append(z2)
    return (rmsnorm(hp, final_norm_g), rmsnorm(hs, final_norm_g), jnp.stack(kvp_l), jnp.stack(winp_l),
            jnp.stack(memp_l), jnp.stack(kvs_l), jnp.stack(wins_l), jnp.stack(vs_l))
```

```python
import functools

import numpy as np
import jax
import jax.numpy as jnp
from jax import lax
from jax.experimental import pallas as pl
from jax.experimental.pallas import tpu as pltpu

F32, BF16, I32 = jnp.float32, jnp.bfloat16, jnp.int32

D_MODEL = 1024
SGU_GROUPS, SGU_GROUP_DIM, SGU_DIM, CHUNK = 4, 128, 512, 128
NSA_HEADS, NSA_KV, NSA_REP, NSA_DH = 8, 2, 4, 64
CMP_BLOCK, CMP_STRIDE, SEL_BLOCK, SEL_TOPK, N_LOCAL, WINDOW, Q_BLOCK = 32, 16, 64, 16, 2, 512, 128
MEM_TOKENS, MEM_HEADS, MEM_DH = 256, 4, 128
N_EXPERTS, TOP_K, D_EXPERT = 32, 4, 1024
SWIGLU_LIMIT, SWIGLU_ALPHA, EPS = 7.0, 1.702, 1e-5
NSA_Q, NSA_KVW, MEM_Q = NSA_HEADS * NSA_DH, NSA_KV * NSA_DH, MEM_HEADS * MEM_DH
C_U, C_V, C_Q, C_KV, C_NG, C_MQ, C_MG = 0, 512, 1024, 1536, 2304, 2328, 2840

LANES = 128
VMEM_LIMIT = 56 * 1024 * 1024
MASKED = -1e9
KV_TILE = 512
PAGES_PER_STEP = 16
MOE_TILE, MOE_CAP = 1024, 160


def _dot(a, b):
    return jnp.dot(a, b, preferred_element_type=F32)


def _dot_nt(a, b):
    return lax.dot_general(a, b, (((1,), (1,)), ((), ())), preferred_element_type=F32)


def _dot_tn(a, b):
    return lax.dot_general(a, b, (((0,), (0,)), ((), ())), preferred_element_type=F32)


def _rms(x, g):
    return x * lax.rsqrt(jnp.mean(x * x, axis=-1, keepdims=True) + EPS) * g


def _layernorm(x, g, b):
    mu = jnp.mean(x, axis=-1, keepdims=True)
    xc = x - mu
    return xc * lax.rsqrt(jnp.mean(xc * xc, axis=-1, keepdims=True) + EPS) * g + b


def _masked_probs(s, mask):
    s = jnp.where(mask, s, -jnp.inf)
    m = jnp.max(s, axis=-1, keepdims=True)
    m = jnp.where(jnp.isfinite(m), m, 0.0)
    p = jnp.exp(s - m)
    return p / jnp.maximum(jnp.sum(p, axis=-1, keepdims=True), 1e-30)


def _split_hi_lo(x):
    hi = x.astype(BF16)
    return hi, (x - hi.astype(F32)).astype(BF16)


def _params(n_axes):
    return pltpu.CompilerParams(dimension_semantics=("arbitrary",) * n_axes, vmem_limit_bytes=VMEM_LIMIT)


def _full(shape):
    return pl.BlockSpec(shape, lambda *_: (0,) * len(shape))


def _mkv_body(x_ref, w_ref, o_ref, ob_ref):
    r = _dot(x_ref[...].astype(BF16), w_ref[...])
    o_ref[...] = r
    ob_ref[...] = r.astype(BF16)


def _mem_kv(mem2d, w_bf):
    rows, tm = mem2d.shape[0], MEM_TOKENS
    return pl.pallas_call(
        _mkv_body, grid=(rows // tm,),
        in_specs=[pl.BlockSpec((tm, D_MODEL), lambda i: (i, 0)), _full((D_MODEL, 2 * MEM_Q))],
        out_specs=[pl.BlockSpec((tm, 2 * MEM_Q), lambda i: (i, 0))] * 2,
        out_shape=[jax.ShapeDtypeStruct((rows, 2 * MEM_Q), F32), jax.ShapeDtypeStruct((rows, 2 * MEM_Q), BF16)],
        compiler_params=_params(1), name="mem_kv")(mem2d, w_bf)


def _inproj_prompt_body(x_ref, g_ref, w_ref, lng_ref, lnb_ref, ws_ref, bst_ref, mkv_ref,
                        oa_ref, qh_ref, kv4_ref, kvw_ref, kvh_ref, gate_ref, om_ref, *, tm):
    ab = _rms(x_ref[0], g_ref[...]).astype(BF16)

    def proj(lo, hi):
        return _dot(ab, w_ref[:, lo:hi])

    z1 = jax.nn.gelu(proj(0, 512))
    z2 = _layernorm(jax.nn.gelu(proj(512, 1024)), lng_ref[...], lnb_ref[...])
    tril = lax.broadcasted_iota(I32, (CHUNK, CHUNK), 0) >= lax.broadcasted_iota(I32, (CHUNK, CHUNK), 1)
    for g in range(SGU_GROUPS):
        cols = slice(g * SGU_GROUP_DIM, (g + 1) * SGU_GROUP_DIM)
        w = jnp.where(tril, ws_ref[g], 0.0).astype(BF16)
        for c in range(tm // CHUNK):
            rows = slice(c * CHUNK, (c + 1) * CHUNK)
            mixed = _dot(w, z2[rows, cols].astype(BF16)) + bst_ref[:, g:g + 1]
            oa_ref[0, rows, cols] = (z1[rows, cols] * mixed).astype(BF16)

    q = proj(1024, 1536)
    for h in range(NSA_HEADS):
        qh_ref[0, h] = q[:, h * NSA_DH:(h + 1) * NSA_DH].astype(BF16)

    kv = proj(1536, 2304)
    kv4_ref[0] = kv[:, :512]
    kvw_ref[0] = kv[:, 512:]
    for c in range(12):
        kvh_ref[0, c] = kv[:, c * NSA_DH:(c + 1) * NSA_DH].astype(BF16)

    gate_ref[0, 0] = jax.nn.sigmoid(proj(2304, 2432))
    gate_ref[0, 1] = jax.nn.sigmoid(proj(2432, 2560))

    mq = proj(2560, 3072).astype(BF16)
    for h in range(MEM_HEADS):
        cols = slice(h * MEM_DH, (h + 1) * MEM_DH)
        s = _dot_nt(mq[:, cols], mkv_ref[0, :, cols]) * (MEM_DH ** -0.5)
        p = jnp.exp(s - jnp.max(s, axis=-1, keepdims=True))
        p = p / jnp.sum(p, axis=-1, keepdims=True)
        om_ref[0, :, cols] = _dot(p.astype(BF16), mkv_ref[0, :, MEM_Q + h * MEM_DH:MEM_Q + (h + 1) * MEM_DH]).astype(BF16)


def _inproj_prompt(x, g1, w, lng, lnb, ws, bst, mkv_bf):
    b, t, _ = x.shape
    tm = min(512, t)
    ncol = w.shape[1]

    def row(width):
        return pl.BlockSpec((1, tm, width), lambda bi, i: (bi, i, 0))

    def heads(n):
        return pl.BlockSpec((1, n, tm, NSA_DH), lambda bi, i: (bi, 0, i, 0))

    return pl.pallas_call(
        functools.partial(_inproj_prompt_body, tm=tm), grid=(b, t // tm),
        in_specs=[row(D_MODEL), _full((1, D_MODEL)), _full((D_MODEL, ncol)), _full((1, SGU_DIM)), _full((1, SGU_DIM)),
                  _full((SGU_GROUPS, CHUNK, CHUNK)), _full((CHUNK, SGU_GROUPS)),
                  pl.BlockSpec((1, MEM_TOKENS, 2 * MEM_Q), lambda bi, i: (bi, 0, 0))],
        out_specs=[row(SGU_DIM), heads(NSA_HEADS), row(512), row(256), heads(12),
                   pl.BlockSpec((1, 2, tm, LANES), lambda bi, i: (bi, 0, i, 0)), row(MEM_Q)],
        out_shape=[jax.ShapeDtypeStruct((b, t, SGU_DIM), BF16), jax.ShapeDtypeStruct((b, NSA_HEADS, t, NSA_DH), BF16),
                   jax.ShapeDtypeStruct((b, t, 512), F32), jax.ShapeDtypeStruct((b, t, 256), F32),
                   jax.ShapeDtypeStruct((b, 12, t, NSA_DH), BF16), jax.ShapeDtypeStruct((b, 2, t, LANES), F32),
                   jax.ShapeDtypeStruct((b, t, MEM_Q), BF16)],
        compiler_params=_params(2), name="inproj_prompt")(x, g1, w, lng, lnb, ws, bst, mkv_bf)


def _inproj_sample_body(x_ref, g_ref, w_ref, lng_ref, lnb_ref, wsc_ref, bsc_ref,
                        oa_ref, z2_ref, q_ref, kv_ref, gate_ref, mq_ref):
    ab = _rms(x_ref[...], g_ref[...]).astype(BF16)

    def proj(lo, hi):
        return _dot(ab, w_ref[:, lo:hi])

    z1 = jax.nn.gelu(proj(0, 512))
    z2 = _layernorm(jax.nn.gelu(proj(512, 1024)), lng_ref[...], lnb_ref[...])
    z2_ref[...] = z2
    oa_ref[...] = (z1 * (wsc_ref[...] * z2 + bsc_ref[...])).astype(BF16)
    q_ref[...] = proj(1024, 2048).astype(BF16)
    kv_ref[...] = proj(2048, 2816)
    gate_ref[...] = jax.nn.sigmoid(proj(2816, 2944))
    mq_ref[...] = proj(2944, 3456)


def _inproj_sample(x, g1, w, lng, lnb, wsc, bsc):
    n = x.shape[0]
    shapes = [((n, SGU_DIM), BF16), ((n, SGU_DIM), F32), ((n, 2 * NSA_Q), BF16), ((n, 768), F32),
              ((n, LANES), F32), ((n, MEM_Q), F32)]
    return pl.pallas_call(
        _inproj_sample_body, grid=(1,),
        in_specs=[_full(x.shape), _full((1, D_MODEL)), _full(w.shape), _full((1, SGU_DIM)), _full((1, SGU_DIM)),
                  _full((1, SGU_DIM)), _full((1, SGU_DIM))],
        out_specs=[_full(s) for s, _ in shapes],
        out_shape=[jax.ShapeDtypeStruct(s, d) for s, d in shapes],
        compiler_params=_params(1), name="inproj_sample")(x, g1, w, lng, lnb, wsc, bsc)


def _compress_finish(ab, b1, w2):
    n = ab.shape[0]
    c = ab[:, :LANES] + pltpu.roll(ab[:, LANES:], n - 1, 0) + b1
    return _dot(jax.nn.gelu(c).astype(BF16), w2)


def _compress_prompt_body(r_ref, wk_ref, b1_ref, w2_ref, o_ref):
    out = _compress_finish(_dot(r_ref[0, 0], wk_ref[0]), b1_ref[0], w2_ref[0])
    for g in range(NSA_KV):
        o_ref[0, 0, g] = out[:, g * NSA_DH:(g + 1) * NSA_DH].astype(BF16)


def _compress_prompt(r, wk, b1, w2):
    _, b, n, width = r.shape
    return pl.pallas_call(
        _compress_prompt_body, grid=(2, b),
        in_specs=[pl.BlockSpec((1, 1, n, width), lambda k, bi: (k, bi, 0, 0)),
                  pl.BlockSpec((1, width, 2 * LANES), lambda k, bi: (k, 0, 0)),
                  pl.BlockSpec((1, 1, LANES), lambda k, bi: (k, 0, 0)),
                  pl.BlockSpec((1, LANES, LANES), lambda k, bi: (k, 0, 0))],
        out_specs=pl.BlockSpec((1, 1, NSA_KV, n, NSA_DH), lambda k, bi: (k, bi, 0, 0, 0)),
        out_shape=jax.ShapeDtypeStruct((2, b, NSA_KV, n, NSA_DH), BF16),
        compiler_params=_params(2), name="compress_prompt")(r, wk, b1, w2)


def _topk_rows(w, k, on_pick):
    n = w.shape[0]
    jidx = lax.broadcasted_iota(I32, w.shape, 0)
    for r in range(k):
        m = jnp.max(w, axis=0, keepdims=True)
        first = jnp.min(jnp.where(w == m, jidx, n), axis=0, keepdims=True)
        pick = jidx == first
        on_pick(r, first, pick)
        w = jnp.where(pick, -jnp.inf, w)


def _nsa_prompt_body(q_ref, g_ref, kc_ref, vc_ref, ks_ref, vs_ref, kw_ref, vw_ref, mt_ref, es_ref, o_ref):
    ns = mt_ref.shape[0]
    rows = NSA_REP * Q_BLOCK
    s0 = pl.program_id(2) * Q_BLOCK
    q = q_ref[0].reshape(rows, NSA_DH)
    tcol = s0 + lax.broadcasted_iota(I32, (Q_BLOCK, 1), 0)
    qpos = jnp.concatenate([tcol] * NSA_REP, axis=0)

    ncp = kc_ref.shape[-2]
    s = _dot_nt(q, kc_ref[0, 0, 0])
    cend = lax.broadcasted_iota(I32, (1, ncp), 1) * CMP_STRIDE + (CMP_BLOCK - 1)
    probs = _masked_probs(s, cend <= qpos)
    o_c = _dot(probs.astype(BF16), vc_ref[0, 0, 0])

    psum = probs[0:Q_BLOCK]
    for r in range(1, NSA_REP):
        psum = psum + probs[r * Q_BLOCK:(r + 1) * Q_BLOCK]
    hi, lo = _split_hi_lo(psum)
    imp = _dot_nt(mt_ref[...], hi) + _dot_nt(mt_ref[...], lo)
    jidx = lax.broadcasted_iota(I32, (ns, Q_BLOCK), 0)
    cur = lax.shift_right_logical(s0 + lax.broadcasted_iota(I32, (ns, Q_BLOCK), 1), 6)
    valid = jidx <= cur
    forced = valid & ((jidx == 0) | (jidx > cur - N_LOCAL))
    w = jnp.where(forced, jnp.inf, jnp.where(valid, imp, -jnp.inf))
    picked = [jnp.zeros((ns, Q_BLOCK), F32)]

    def on_pick(r, first, pick):
        picked[0] = jnp.where(pick, 1.0, picked[0])

    _topk_rows(w, min(SEL_TOPK, ns), on_pick)
    bias = jnp.where(valid, jnp.where(picked[0] > 0.0, 0.0, MASKED), MASKED)
    bias_t = bias.T.astype(BF16)

    def step(kt, carry):
        m, l, acc = carry
        k0 = pl.multiple_of(kt * KV_TILE, KV_TILE)
        sc = _dot_nt(q, ks_ref[0, 0, pl.ds(k0, KV_TILE), :])
        sb = _dot(bias_t, es_ref[kt])
        sc = sc + jnp.concatenate([sb] * NSA_REP, axis=0)
        kpos = k0 + lax.broadcasted_iota(I32, (1, KV_TILE), 1)
        sc = jnp.where(kpos <= qpos, sc, MASKED)
        m_new = jnp.maximum(m, jnp.max(sc, axis=-1, keepdims=True))
        alpha = jnp.exp(m - m_new)
        p = jnp.exp(sc - m_new)
        l = alpha * l + jnp.sum(p, axis=-1, keepdims=True)
        acc = alpha * acc + _dot(p.astype(BF16), vs_ref[0, 0, pl.ds(k0, KV_TILE), :])
        return m_new, l, acc

    n_tiles = (s0 + Q_BLOCK + KV_TILE - 1) // KV_TILE
    init = (jnp.full((rows, 1), -jnp.inf, F32), jnp.zeros((rows, 1), F32), jnp.zeros((rows, NSA_DH), F32))
    _, l, acc = lax.fori_loop(0, n_tiles, step, init)
    o_s = acc / jnp.maximum(l, 1e-30)

    span = WINDOW + Q_BLOCK
    start = pl.multiple_of(jnp.maximum(s0 - WINDOW, 0), Q_BLOCK)
    sw = _dot_nt(q, kw_ref[0, 0, pl.ds(start, span), :])
    wpos = start + lax.broadcasted_iota(I32, (1, span), 1)
    pw = _masked_probs(sw, (wpos <= qpos) & (wpos > qpos - WINDOW))
    o_w = _dot(pw.astype(BF16), vw_ref[0, 0, pl.ds(start, span), :])

    gates = g_ref[0, 0]
    outs = []
    for r in range(NSA_REP):
        sl = slice(r * Q_BLOCK, (r + 1) * Q_BLOCK)
        outs.append(gates[:, r:r + 1] * o_c[sl] + gates[:, 4 + r:5 + r] * o_s[sl] + gates[:, 8 + r:9 + r] * o_w[sl])
    o_ref[0] = jnp.concatenate(outs, axis=1).astype(BF16)


def _nsa_prompt(qh, gates, kcv, kvh, mt, es):
    b, _, t, _ = qh.shape
    ncp = kcv.shape[-2]

    def kv(c):
        return pl.BlockSpec((1, 1, t, NSA_DH), lambda bi, g, i: (bi, c + g, 0, 0))

    def cmp(k):
        return pl.BlockSpec((1, 1, 1, ncp, NSA_DH), lambda bi, g, i: (k, bi, g, 0, 0))

    return pl.pallas_call(
        _nsa_prompt_body, grid=(b, NSA_KV, t // Q_BLOCK),
        in_specs=[pl.BlockSpec((1, NSA_REP, Q_BLOCK, NSA_DH), lambda bi, g, i: (bi, g, i, 0)),
                  pl.BlockSpec((1, 1, Q_BLOCK, LANES), lambda bi, g, i: (bi, g, i, 0)),
                  cmp(0), cmp(1), kv(4), kv(6), kv(8), kv(10), _full(mt.shape), _full(es.shape)],
        out_specs=pl.BlockSpec((1, Q_BLOCK, NSA_REP * NSA_DH), lambda bi, g, i: (bi, i, g)),
        out_shape=jax.ShapeDtypeStruct((b, t, NSA_Q), BF16),
        compiler_params=_params(3), name="nsa_prompt")(qh, gates, kcv, kcv, kvh, kvh, kvh, kvh, mt, es)


def _sample_cmp_body(pt_ref, *refs, n_chunks):
    pc = PAGES_PER_STEP
    pages = refs[:2 * pc]
    q_ref, wk_ref, b1_ref, w2_ref, m_ref, oc_ref, imp_ref, rk_ref, ab_ref = refs[2 * pc:]
    c = pl.program_id(1)
    per_page = 128 // CMP_STRIDE
    for kind in range(2):
        for k in range(pc):
            for j in range(CMP_STRIDE):
                rk_ref[kind, k * per_page:(k + 1) * per_page, j * LANES:(j + 1) * LANES] = (
                    pages[kind * pc + k][0, pl.ds(j, per_page, stride=CMP_STRIDE), :])
    row0 = pl.multiple_of(c * (pc * per_page), pc * per_page)
    for kind in range(2):
        ab_ref[kind, pl.ds(row0, pc * per_page), :] = _dot(rk_ref[kind].astype(BF16), wk_ref[kind])

    @pl.when(c == n_chunks - 1)
    def _():
        kc = _compress_finish(ab_ref[0], b1_ref[0], w2_ref[0]).astype(BF16)
        vc = _compress_finish(ab_ref[1], b1_ref[1], w2_ref[1]).astype(BF16)
        ncp = kc.shape[0]
        row = lax.broadcasted_iota(I32, (NSA_HEADS, LANES), 0)
        lane = lax.broadcasted_iota(I32, (NSA_HEADS, LANES), 1)
        qbd = jnp.where((lane >= NSA_DH) == (row >= NSA_REP), q_ref[0], jnp.zeros((), BF16))
        s = _dot_nt(qbd, kc)
        n = lax.broadcasted_iota(I32, (1, ncp), 1)
        probs = _masked_probs(s, n * CMP_STRIDE + (CMP_BLOCK - 1) <= ncp * CMP_STRIDE)
        oc_ref[0] = _dot(probs.astype(BF16), vc)
        hi, lo = _split_hi_lo(probs)
        imp8 = _dot(hi, m_ref[...]) + _dot(lo, m_ref[...])
        r8 = lax.broadcasted_iota(I32, imp8.shape, 0)
        imp_ref[0, 0:1, :] = jnp.sum(jnp.where(r8 < NSA_REP, imp8, 0.0), axis=0, keepdims=True)
        imp_ref[0, 1:2, :] = jnp.sum(jnp.where(r8 >= NSA_REP, imp8, 0.0), axis=0, keepdims=True)


def _sample_cmp(pt_flat, cache, qrep, wk, b1, w2, m, n_pages):
    nb = qrep.shape[0]
    pc = PAGES_PER_STEP
    n_chunks = n_pages // pc
    ncp = n_pages * 128 // CMP_STRIDE
    ns = m.shape[1]

    def page(kind, k):
        return pl.BlockSpec((1, 128, LANES), lambda b, c, pt: (pt[b * n_pages + c * pc + k], 0, kind))

    grid_spec = pltpu.PrefetchScalarGridSpec(
        num_scalar_prefetch=1, grid=(nb, n_chunks),
        in_specs=[page(kind, k) for kind in range(2) for k in range(pc)] + [
            pl.BlockSpec((1, NSA_HEADS, LANES), lambda b, c, pt: (b, 0, 0)),
            pl.BlockSpec(wk.shape, lambda b, c, pt: (0, 0, 0)), pl.BlockSpec(b1.shape, lambda b, c, pt: (0, 0, 0)),
            pl.BlockSpec(w2.shape, lambda b, c, pt: (0, 0, 0)), pl.BlockSpec(m.shape, lambda b, c, pt: (0, 0))],
        out_specs=[pl.BlockSpec((1, NSA_HEADS, LANES), lambda b, c, pt: (b, 0, 0)),
                   pl.BlockSpec((1, NSA_KV, ns), lambda b, c, pt: (b, 0, 0))],
        scratch_shapes=[pltpu.VMEM((2, pc * 8, CMP_STRIDE * LANES), F32), pltpu.VMEM((2, ncp, 2 * LANES), F32)])
    return pl.pallas_call(
        functools.partial(_sample_cmp_body, n_chunks=n_chunks), grid_spec=grid_spec,
        out_shape=[jax.ShapeDtypeStruct((nb, NSA_HEADS, LANES), F32), jax.ShapeDtypeStruct((nb, NSA_KV, ns), F32)],
        compiler_params=_params(2), name="sample_compress")(pt_flat, *([cache] * (2 * pc)), qrep, wk, b1, w2, m)


def _sample_topk_body(imp_ref, idx_ref, *, k):
    ns = imp_ref.shape[1]
    idx_ref[...] = jnp.zeros(idx_ref.shape, I32)
    for g in range(NSA_KV):
        jidx = lax.broadcasted_iota(I32, imp_ref.shape[1:], 0)
        w = jnp.where((jidx == 0) | (jidx > ns - N_LOCAL), jnp.inf, imp_ref[g])

        def on_pick(r, first, pick, g=g):
            idx_ref[g, r:r + 1, :] = first

        _topk_rows(w, k, on_pick)


def _sample_topk(imp_t, k):
    _, ns, nb = imp_t.shape
    return pl.pallas_call(
        functools.partial(_sample_topk_body, k=k), grid=(1,),
        in_specs=[_full(imp_t.shape)], out_specs=_full((NSA_KV, SEL_TOPK, nb)),
        out_shape=jax.ShapeDtypeStruct((NSA_KV, SEL_TOPK, nb), I32),
        compiler_params=_params(1), name="sample_topk")(imp_t)


def _softmax_with_self(qbd, keys, vals, k_self, v_self, mask):
    s = _dot_nt(qbd, keys)
    if mask is not None:
        s = jnp.where(mask, s, -jnp.inf)
    s_self = jnp.sum(qbd.astype(F32) * k_self.astype(BF16).astype(F32), axis=-1, keepdims=True)
    m = jnp.maximum(jnp.max(s, axis=-1, keepdims=True), s_self)
    p = jnp.exp(s - m)
    p_self = jnp.exp(s_self - m)
    den = jnp.sum(p, axis=-1, keepdims=True) + p_self
    num = _dot(p.astype(BF16), vals) + p_self.astype(BF16).astype(F32) * v_self.astype(BF16).astype(F32)
    return num / den


def _sample_attn_body(idx_ref, pt_ref, *refs, n_sel):
    blocks = refs[:NSA_KV * n_sel]
    (win_ref, q_ref, kvn_ref, gate_ref, oc_ref, mq_ref, mem_ref,
     ob_ref, om_ref, wn_ref, ks_ref, vs_ref) = refs[NSA_KV * n_sel:]
    row = lax.broadcasted_iota(I32, (NSA_HEADS, LANES), 0)
    lane = lax.broadcasted_iota(I32, (NSA_HEADS, LANES), 1)
    own_half = (lane >= NSA_DH) == (row >= NSA_REP)
    qbd = jnp.where(own_half, q_ref[0], jnp.zeros((), BF16))
    kvn = kvn_ref[0]

    o_sel = []
    for g in range(NSA_KV):
        for k in range(n_sel):
            blk = blocks[g * n_sel + k][0]
            ks_ref[g, k * SEL_BLOCK:(k + 1) * SEL_BLOCK, :] = blk[:, :LANES].astype(BF16)
            vs_ref[g, k * SEL_BLOCK:(k + 1) * SEL_BLOCK, :] = blk[:, LANES:].astype(BF16)
        o_sel.append(_softmax_with_self(qbd, ks_ref[g], vs_ref[g], kvn[:, 256:384], kvn[:, 384:512], None))
    o_s = jnp.where(row < NSA_REP, o_sel[0], o_sel[1])

    win = win_ref[0]
    wbuf = win.shape[0]
    wrow = lax.broadcasted_iota(I32, (1, wbuf), 1)
    o_w = _softmax_with_self(qbd, win[:, :LANES].astype(BF16), win[:, LANES:].astype(BF16),
                             kvn[:, 512:640], kvn[:, 640:768], wrow >= 1)
    last = lax.broadcasted_iota(I32, win.shape, 0) == wbuf - 1
    wn_ref[0] = jnp.where(last, kvn[:, 512:768], pltpu.roll(win, wbuf - 1, 0))

    gate = gate_ref[0]
    o = jnp.zeros((NSA_HEADS, LANES), F32)
    for j, branch in enumerate((oc_ref[0], o_s, o_w)):
        gj = jnp.sum(jnp.where(lane == j * NSA_HEADS + row, gate, 0.0), axis=-1, keepdims=True)
        o = o + gj * branch
    o = jnp.where(own_half, o, 0.0)
    ob_ref[0] = o[:, :NSA_DH] + o[:, NSA_DH:]

    r8 = lax.broadcasted_iota(I32, (NSA_HEADS, MEM_Q), 0)
    head = lax.shift_right_logical(lax.broadcasted_iota(I32, (NSA_HEADS, MEM_Q), 1), 7)
    qm = jnp.where(head == r8, mq_ref[0], 0.0).astype(BF16)
    mem = mem_ref[0]
    s = _dot_nt(qm, mem[:, :MEM_Q].astype(BF16)) * (MEM_DH ** -0.5)
    p = jnp.exp(s - jnp.max(s, axis=-1, keepdims=True))
    p = p / jnp.sum(p, axis=-1, keepdims=True)
    om = _dot(p.astype(BF16), mem[:, MEM_Q:].astype(BF16))
    om_ref[0] = jnp.sum(jnp.where(head == r8, om, 0.0), axis=0, keepdims=True)


def _sample_attn(idx_flat, pt_flat, cache_half, win, qrep, kvn, gate, oc, mq, memkv, n_pages, n_sel):
    nb, wbuf, _ = win.shape

    def sel_block(g, k):
        def index(b, idx, pt):
            j = idx[b * (NSA_KV * SEL_TOPK) + g * SEL_TOPK + k]
            return (pt[b * n_pages + lax.shift_right_logical(j, 1)] * 2 + (j & 1), 0, 1)
        return pl.BlockSpec((1, SEL_BLOCK, 2 * LANES), index)

    def per_seq(shape):
        return pl.BlockSpec((1,) + shape, lambda b, idx, pt: (b,) + (0,) * len(shape))

    grid_spec = pltpu.PrefetchScalarGridSpec(
        num_scalar_prefetch=2, grid=(nb,),
        in_specs=[sel_block(g, k) for g in range(NSA_KV) for k in range(n_sel)] + [
            per_seq((wbuf, 2 * LANES)), per_seq((NSA_HEADS, LANES)), per_seq((1, 768)), per_seq((1, LANES)),
            per_seq((NSA_HEADS, LANES)), per_seq((1, MEM_Q)), per_seq((MEM_TOKENS, 2 * MEM_Q))],
        out_specs=[per_seq((NSA_HEADS, NSA_DH)), per_seq((1, MEM_Q)), per_seq((wbuf, 2 * LANES))],
        scratch_shapes=[pltpu.VMEM((NSA_KV, n_sel * SEL_BLOCK, LANES), BF16)] * 2)
    return pl.pallas_call(
        functools.partial(_sample_attn_body, n_sel=n_sel), grid_spec=grid_spec,
        out_shape=[jax.ShapeDtypeStruct((nb, NSA_HEADS, NSA_DH), F32), jax.ShapeDtypeStruct((nb, 1, MEM_Q), F32),
                   jax.ShapeDtypeStruct((nb, wbuf, 2 * LANES), F32)],
        compiler_params=_params(1), name="sample_attn")(
            idx_flat, pt_flat, *([cache_half] * (NSA_KV * n_sel)), win, qrep, kvn, gate, oc, mq, memkv)


def _merge_body(x_ref, oa_ref, ob_ref, om_ref, g1_ref, wmg_ref, wb_ref, wo_ref, g2_ref, rwh_ref, rwl_ref, rb_ref,
                h1_ref, hn_ref, gt_ref):
    x = x_ref[...]
    ab = _rms(x, g1_ref[...]).astype(BF16)
    s = None
    for k, o_ref in enumerate((oa_ref, ob_ref, om_ref)):
        gk = jax.nn.sigmoid(_dot(ab, wmg_ref[:, k * D_MODEL:(k + 1) * D_MODEL]))
        term = gk * _dot(o_ref[...], wb_ref[k])
        s = term if s is None else s + term
    h1 = x + _dot(s.astype(BF16), wo_ref[...])
    h1_ref[...] = h1
    hn = _rms(h1, g2_ref[...])
    hi, lo = _split_hi_lo(hn)
    hn_ref[...] = hi

    logit = _dot_nt(rwh_ref[...], hi) + _dot_nt(rwh_ref[...], lo) + _dot_nt(rwl_ref[...], hi) + rb_ref[...]
    picked = [jnp.zeros(logit.shape, F32)]

    def on_pick(r, first, pick):
        picked[0] = jnp.where(pick, 1.0, picked[0])

    _topk_rows(logit, TOP_K, on_pick)
    top = jnp.max(logit, axis=0, keepdims=True)
    ex = jnp.where(picked[0] > 0.0, jnp.exp(logit - top), 0.0)
    gt_ref[...] = ex / jnp.sum(ex, axis=0, keepdims=True)


def _merge(x, oa, ob, om, g1, wmg, wb, wo, g2, rwh, rwl, rb, tm):
    n = x.shape[0]

    def row(width):
        return pl.BlockSpec((tm, width), lambda i: (i, 0))

    return pl.pallas_call(
        _merge_body, grid=(n // tm,),
        in_specs=[row(D_MODEL), row(512), row(512), row(512), _full((1, D_MODEL)), _full(wmg.shape), _full(wb.shape),
                  _full(wo.shape), _full((1, D_MODEL)), _full(rwh.shape), _full(rwl.shape), _full(rb.shape)],
        out_specs=[row(D_MODEL), row(D_MODEL), pl.BlockSpec((N_EXPERTS, tm), lambda i: (0, i))],
        out_shape=[jax.ShapeDtypeStruct((n, D_MODEL), F32), jax.ShapeDtypeStruct((n, D_MODEL), BF16),
                   jax.ShapeDtypeStruct((N_EXPERTS, n), F32)],
        compiler_params=_params(1), name="merge")(x, oa, ob, om, g1, wmg, wb, wo, g2, rwh, rwl, rb)


def _moe_body(cnt_ref, hn_ref, h1_ref, gt_ref, u_ref, wgu_ref, bgu_ref, wd_ref, bd_ref, fg_ref, y_ref,
              acc_ref, rank_ref, *, cap):
    i, e = pl.program_id(0), pl.program_id(1)
    t = hn_ref.shape[0]

    @pl.when(e == 0)
    def _():
        sel = jnp.where(gt_ref[...] > 0.0, 1.0, 0.0).astype(BF16)
        rank_ref[...] = _dot(sel, u_ref[...])
        acc_ref[...] = jnp.zeros(acc_ref.shape, F32)

    g_row = gt_ref[pl.ds(e, 1), :]
    r_row = rank_ref[pl.ds(e, 1), :]
    slot = lax.broadcasted_iota(I32, (cap, t), 0).astype(F32)

    def chunk(ch, carry):
        want = slot + (ch * cap).astype(F32)
        hit = (g_row > 0.0) & (r_row == want)
        onehot = jnp.where(hit, 1.0, 0.0).astype(BF16)
        xe = _dot(onehot, hn_ref[...]).astype(BF16)
        gu = _dot(xe, wgu_ref[0]) + bgu_ref[0]
        gate = jnp.minimum(gu[:, :D_EXPERT], SWIGLU_LIMIT)
        up = jnp.clip(gu[:, D_EXPERT:], -SWIGLU_LIMIT, SWIGLU_LIMIT)
        act = (up + 1.0) * gate * jax.nn.sigmoid(SWIGLU_ALPHA * gate)
        ye = _dot(act.astype(BF16), wd_ref[0]) + bd_ref[0]
        ye = ye * jnp.sum(jnp.where(hit, g_row, 0.0), axis=-1, keepdims=True)
        acc_ref[...] += _dot_tn(onehot, ye.astype(BF16))
        return carry

    n_chunks = (cnt_ref[i * N_EXPERTS + e] + cap - 1) // cap
    lax.fori_loop(0, n_chunks, chunk, 0)

    @pl.when(e == N_EXPERTS - 1)
    def _():
        y_ref[...] = _rms(h1_ref[...] + acc_ref[...], fg_ref[...])


def _moe(counts, hn, h1, gate_t, upper, wgu, bgu, wd, bd, fg, tile, cap):
    n = hn.shape[0]
    grid_spec = pltpu.PrefetchScalarGridSpec(
        num_scalar_prefetch=1, grid=(n // tile, N_EXPERTS),
        in_specs=[pl.BlockSpec((tile, D_MODEL), lambda i, e, c: (i, 0)),
                  pl.BlockSpec((tile, D_MODEL), lambda i, e, c: (i, 0)),
                  pl.BlockSpec((N_EXPERTS, tile), lambda i, e, c: (0, i)),
                  pl.BlockSpec((tile, tile), lambda i, e, c: (0, 0)),
                  pl.BlockSpec((1, D_MODEL, 2 * D_EXPERT), lambda i, e, c: (e, 0, 0)),
                  pl.BlockSpec((1, 1, 2 * D_EXPERT), lambda i, e, c: (e, 0, 0)),
                  pl.BlockSpec((1, D_EXPERT, D_MODEL), lambda i, e, c: (e, 0, 0)),
                  pl.BlockSpec((1, 1, D_MODEL), lambda i, e, c: (e, 0, 0)),
                  pl.BlockSpec((1, D_MODEL), lambda i, e, c: (0, 0))],
        out_specs=pl.BlockSpec((tile, D_MODEL), lambda i, e, c: (i, 0)),
        scratch_shapes=[pltpu.VMEM((tile, D_MODEL), F32), pltpu.VMEM((N_EXPERTS, tile), F32)])
    return pl.pallas_call(
        functools.partial(_moe_body, cap=cap), grid_spec=grid_spec,
        out_shape=jax.ShapeDtypeStruct((n, D_MODEL), F32),
        compiler_params=_params(2), name="moe")(counts, hn, h1, gate_t, upper, wgu, bgu, wd, bd, fg)


def _gate_columns(order):
    idx = np.zeros((LANES,), np.int32)
    ok = np.zeros((LANES,), bool)
    for c, (h, j) in enumerate(order):
        idx[c], ok[c] = C_NG + h * 3 + j, True
    return idx, ok


def _prep_in_weights(w_in):
    def sect(lo, hi):
        return w_in[:, lo:hi]

    def gates(order):
        idx, ok = _gate_columns(order)
        return jnp.where(jnp.asarray(ok)[None, :], w_in[:, idx], 0.0)

    w_q = sect(C_Q, C_KV) * (NSA_DH ** -0.5)
    g_prompt = [gates([(NSA_REP * g + r, j) for j in range(3) for r in range(NSA_REP)]) for g in range(NSA_KV)]
    w_prompt = jnp.concatenate([sect(C_U, C_Q), w_q, sect(C_KV, C_NG)] + g_prompt + [sect(C_MQ, C_MG)], axis=1)
    q_rep = jnp.concatenate([w_q.reshape(D_MODEL, NSA_HEADS, 1, NSA_DH)] * 2, axis=2).reshape(D_MODEL, 2 * NSA_Q)
    g_sample = gates([(h, j) for j in range(3) for h in range(NSA_HEADS)])
    w_sample = jnp.concatenate([sect(C_U, C_Q), q_rep, sect(C_KV, C_NG), g_sample, sect(C_MQ, C_MG)], axis=1)
    return w_prompt.astype(BF16), w_sample.astype(BF16), sect(C_MG, w_in.shape[1]).astype(BF16)


def _prep_compress(w1, b1, w2):
    eye = jnp.eye(NSA_KV, dtype=F32)
    w = w1.reshape(2, 2, CMP_STRIDE, NSA_DH, NSA_DH)
    wk = jnp.einsum('khjde,xg->kjxdhge', w, eye).reshape(2, CMP_STRIDE * LANES, 2 * LANES)
    b = jnp.concatenate([b1, b1], axis=-1).reshape(2, 1, LANES)
    w2d = jnp.einsum('kde,xg->kxdge', w2, eye).reshape(2, LANES, LANES)
    return wk.astype(BF16), b, w2d.astype(BF16)


def _cmp_to_sel_t(ncp, ns):
    cs = np.arange(ncp) * CMP_STRIDE
    ss = np.arange(ns) * SEL_BLOCK
    m = (cs[None, :] <= ss[:, None] + SEL_BLOCK - 1) & (cs[None, :] + CMP_BLOCK - 1 >= ss[:, None])
    m[:, ncp - 1] = False
    return m.astype(np.float32)


def _block_onehot(t):
    e = (np.arange(t // SEL_BLOCK)[:, None] == (np.arange(t) // SEL_BLOCK)[None, :]).astype(np.float32)
    return e.reshape(t // SEL_BLOCK, t // KV_TILE, KV_TILE).transpose(1, 0, 2)


def _moe_group(h1_and_gates, weights, tile, cap):
    h1, hn, gate_t = h1_and_gates
    n = h1.shape[0]
    counts = jnp.sum((gate_t > 0.0).reshape(N_EXPERTS, n // tile, tile), axis=-1, dtype=I32).T.reshape(-1)
    upper = jnp.asarray(np.triu(np.ones((tile, tile), np.float32), 1), BF16)
    return _moe(counts, hn, h1, gate_t, upper, *weights, tile, cap)


def kernel(x_prompt, x_sample, mem_prompt, cache_nsa_kv, cache_nsa_win, cache_mem_kv, page_table,
           norm1_g, w_in, sgu_ln_g, sgu_ln_b, sgu_w, sgu_b, cmp_w1, cmp_b1, cmp_w2, w_mem_kv,
           w_branch, w_out, norm2_g, router_w, router_b, exp_w_gate_up, exp_b_gate_up,
           exp_w_down, exp_b_down, final_norm_g):
    assert norm1_g.shape[0] == 1, "one layer"
    bp, t, _ = x_prompt.shape
    nb = x_sample.shape[0]
    n_pool, page = cache_nsa_kv.shape[1:3]
    n_pages = page_table.shape[1]
    past = n_pages * page
    wbuf = cache_nsa_win.shape[2]
    assert page == 128 and wbuf == WINDOW and n_pages % PAGES_PER_STEP == 0 and t % KV_TILE == 0 and t >= WINDOW + Q_BLOCK

    g1, g2, fg = norm1_g[0][None], norm2_g[0][None], final_norm_g[None]
    lng, lnb = sgu_ln_g[0][None], sgu_ln_b[0][None]
    w_prompt, w_sample, w_mg = _prep_in_weights(w_in[0])
    wk, b1c, w2c = _prep_compress(cmp_w1[0], cmp_b1[0], cmp_w2[0])
    wb, wo = w_branch[0].astype(BF16), w_out[0].astype(BF16)
    rwh, rwl = _split_hi_lo(router_w[0].T)
    rb = router_b[0][:, None]
    moe_w = (exp_w_gate_up[0].astype(BF16), exp_b_gate_up[0][:, None, :], exp_w_down[0].astype(BF16),
             exp_b_down[0][:, None, :], fg)
    merge_w = (g1, w_mg, wb, wo, g2, rwh, rwl, rb)

    mkv, mkv_bf = _mem_kv(mem_prompt.reshape(bp * MEM_TOKENS, D_MODEL), w_mem_kv[0].astype(BF16))
    oa, qh, kv4, kvw, kvh, gates, om = _inproj_prompt(
        x_prompt, g1, w_prompt, lng, lnb, sgu_w[0], sgu_b[0].T, mkv_bf.reshape(bp, MEM_TOKENS, 2 * MEM_Q))
    ncp, ns = t // CMP_STRIDE, t // SEL_BLOCK
    r = jnp.stack([kv4[:, :, k * LANES:(k + 1) * LANES].reshape(bp, ncp, CMP_STRIDE * LANES) for k in range(2)]).astype(BF16)
    kcv = _compress_prompt(r, wk, b1c, w2c)
    mt = jnp.asarray(_cmp_to_sel_t(ncp, ns), BF16)
    ob = _nsa_prompt(qh, gates, kcv, kvh, mt, jnp.asarray(_block_onehot(t), BF16))
    n = bp * t
    x2 = x_prompt.reshape(n, D_MODEL)
    y_prompt = _moe_group(_merge(x2, oa.reshape(n, 512), ob.reshape(n, 512), om.reshape(n, 512), *merge_w, tm=min(512, n)),
                          moe_w, min(MOE_TILE, n), MOE_CAP)

    xs = x_sample.reshape(nb, D_MODEL)
    wsc = jnp.repeat(sgu_w[0][:, 0, 0], SGU_GROUP_DIM)[None]
    bsc = jnp.repeat(sgu_b[0][:, 0], SGU_GROUP_DIM)[None]
    oa_s, z2_s, q_s, kv_s, gate_s, mq_s = _inproj_sample(xs, g1, w_sample, lng, lnb, wsc, bsc)
    qrep = q_s.reshape(nb, NSA_HEADS, LANES)
    pt_flat = page_table.reshape(-1)
    ncp_s, ns_s = past // CMP_STRIDE, past // SEL_BLOCK
    m_s = jnp.asarray(_cmp_to_sel_t(ncp_s, ns_s).T, BF16)
    oc_s, imp_s = _sample_cmp(pt_flat, cache_nsa_kv[0].reshape(n_pool, page, 512), qrep, wk, b1c, w2c, m_s, n_pages)
    n_sel = min(SEL_TOPK, ns_s + 1) - 1
    idx = _sample_topk(imp_s.transpose(1, 2, 0), n_sel)
    idx_flat = idx.transpose(2, 0, 1).reshape(-1)
    ob_s, om_s, win_new = _sample_attn(
        idx_flat, pt_flat, cache_nsa_kv[0].reshape(n_pool * 2, SEL_BLOCK, 512), cache_nsa_win[0].reshape(nb, wbuf, 256),
        qrep, kv_s[:, None, :], gate_s[:, None, :], oc_s, mq_s[:, None, :],
        cache_mem_kv[0].reshape(nb, MEM_TOKENS, 2 * MEM_Q), n_pages, n_sel)
    y_sample = _moe_group(_merge(xs, oa_s, ob_s.reshape(nb, NSA_Q).astype(BF16), om_s.reshape(nb, MEM_Q).astype(BF16),
                                 *merge_w, tm=nb), moe_w, nb, 32)

    return (y_prompt.reshape(bp, t, D_MODEL), y_sample.reshape(nb, 1, D_MODEL),
            kv4.reshape(1, bp, t, 4, NSA_KV, NSA_DH), kvw[:, t - WINDOW:].reshape(1, bp, WINDOW, 2, NSA_KV, NSA_DH),
            mkv.reshape(1, bp, MEM_TOKENS, 2, MEM_HEADS, MEM_DH), kv_s[:, :512].reshape(1, nb, 1, 4, NSA_KV, NSA_DH),
            win_new.reshape(1, nb, wbuf, 2, NSA_KV, NSA_DH), z2_s.reshape(1, nb, 1, SGU_DIM))
```

```python
import functools

import numpy as np
import jax
import jax.numpy as jnp
from jax import lax
from jax.experimental import pallas as pl
from jax.experimental.pallas import tpu as pltpu

F32, BF16, I32 = jnp.float32, jnp.bfloat16, jnp.int32

D_MODEL = 1024
SGU_GROUPS, SGU_GROUP_DIM, SGU_DIM, CHUNK = 4, 128, 512, 128
NSA_HEADS, NSA_KV, NSA_REP, NSA_DH = 8, 2, 4, 64
CMP_BLOCK, CMP_STRIDE, SEL_BLOCK, SEL_TOPK, N_LOCAL, WINDOW, Q_BLOCK = 32, 16, 64, 16, 2, 512, 128
MEM_TOKENS, MEM_HEADS, MEM_DH = 256, 4, 128
N_EXPERTS, TOP_K, D_EXPERT = 32, 4, 1024
SWIGLU_LIMIT, SWIGLU_ALPHA, EPS = 7.0, 1.702, 1e-5
NSA_Q, NSA_KVW, MEM_Q = NSA_HEADS * NSA_DH, NSA_KV * NSA_DH, MEM_HEADS * MEM_DH
C_U, C_V, C_Q, C_KV, C_NG, C_MQ, C_MG = 0, 512, 1024, 1536, 2304, 2328, 2840

LANES = 128
VMEM_LIMIT = 56 * 1024 * 1024
MASKED = -1e9
KV_TILE = 512
HEAD_ROWS = 4
PAGES_PER_STEP = 16
MOE_TILE, MOE_CAP = 1024, 160


def _dot(a, b):
    return jnp.dot(a, b, preferred_element_type=F32)


def _dot_nt(a, b):
    return lax.dot_general(a, b, (((1,), (1,)), ((), ())), preferred_element_type=F32)


def _dot_tn(a, b):
    return lax.dot_general(a, b, (((0,), (0,)), ((), ())), preferred_element_type=F32)


def _rms(x, g):
    return x * lax.rsqrt(jnp.mean(x * x, axis=-1, keepdims=True) + EPS) * g


def _layernorm(x, g, b):
    mu = jnp.mean(x, axis=-1, keepdims=True)
    xc = x - mu
    return xc * lax.rsqrt(jnp.mean(xc * xc, axis=-1, keepdims=True) + EPS) * g + b


def _masked_probs(s, mask):
    s = jnp.where(mask, s, -jnp.inf)
    m = jnp.max(s, axis=-1, keepdims=True)
    m = jnp.where(jnp.isfinite(m), m, 0.0)
    p = jnp.exp(s - m)
    return p / jnp.maximum(jnp.sum(p, axis=-1, keepdims=True), 1e-30)


def _softmax(s):
    p = jnp.exp(s - jnp.max(s, axis=-1, keepdims=True))
    return p / jnp.sum(p, axis=-1, keepdims=True)


def _split_hi_lo(x):
    hi = x.astype(BF16)
    return hi, (x - hi.astype(F32)).astype(BF16)


def _params(n_axes):
    return pltpu.CompilerParams(dimension_semantics=("arbitrary",) * n_axes, vmem_limit_bytes=VMEM_LIMIT)


def _full(shape):
    return pl.BlockSpec(shape, lambda *_: (0,) * len(shape))


def _mkv_body(x_ref, w_ref, o_ref, ob_ref):
    r = _dot(x_ref[...].astype(BF16), w_ref[...])
    o_ref[...] = r
    ob_ref[...] = r.astype(BF16)


def _mem_kv(mem2d, w_bf):
    rows, tm = mem2d.shape[0], MEM_TOKENS
    return pl.pallas_call(
        _mkv_body, grid=(rows // tm,),
        in_specs=[pl.BlockSpec((tm, D_MODEL), lambda i: (i, 0)), _full((D_MODEL, 2 * MEM_Q))],
        out_specs=[pl.BlockSpec((tm, 2 * MEM_Q), lambda i: (i, 0))] * 2,
        out_shape=[jax.ShapeDtypeStruct((rows, 2 * MEM_Q), F32), jax.ShapeDtypeStruct((rows, 2 * MEM_Q), BF16)],
        compiler_params=_params(1), name="mem_kv")(mem2d, w_bf)


def _inproj_prompt_body(x_ref, g_ref, w_ref, wkvt_ref, lng_ref, lnb_ref, ws_ref, bst_ref, mkv_ref,
                        oa_ref, qh_ref, kv4t_ref, kvwt_ref, kt_ref, v_ref, kcv_ref, gate_ref, om_ref, *, tm):
    ab = _rms(x_ref[0], g_ref[...]).astype(BF16)

    def proj(lo, hi):
        return _dot(ab, w_ref[:, lo:hi])

    z1 = jax.nn.gelu(proj(0, 512))
    z2 = _layernorm(jax.nn.gelu(proj(512, 1024)), lng_ref[...], lnb_ref[...])
    tril = lax.broadcasted_iota(I32, (CHUNK, CHUNK), 0) >= lax.broadcasted_iota(I32, (CHUNK, CHUNK), 1)
    for g in range(SGU_GROUPS):
        cols = slice(g * SGU_GROUP_DIM, (g + 1) * SGU_GROUP_DIM)
        w = jnp.where(tril, ws_ref[g], 0.0).astype(BF16)
        for c in range(tm // CHUNK):
            rows = slice(c * CHUNK, (c + 1) * CHUNK)
            mixed = _dot(w, z2[rows, cols].astype(BF16)) + bst_ref[:, g:g + 1]
            oa_ref[0, rows, cols] = (z1[rows, cols] * mixed).astype(BF16)

    q = proj(1024, 1536)
    for h in range(NSA_HEADS):
        qh_ref[0, h] = q[:, h * NSA_DH:(h + 1) * NSA_DH].astype(BF16)

    kvt = _dot_nt(wkvt_ref[...], ab)
    kv4t_ref[0] = kvt[:512]
    kvwt_ref[0] = kvt[512:]
    for c, r0 in enumerate((256, 320, 512, 576)):
        for u in range(tm // LANES):
            kt_ref[0, c, u] = kvt[r0:r0 + NSA_DH, u * LANES:(u + 1) * LANES].astype(BF16)
    kv = proj(1536, 2304)
    kcv_ref[0] = kv[:, :256].astype(BF16)
    for c, c0 in enumerate((384, 448, 640, 704)):
        v_ref[0, c] = kv[:, c0:c0 + NSA_DH].astype(BF16)

    gate_ref[0, 0] = jax.nn.sigmoid(proj(2304, 2432))
    gate_ref[0, 1] = jax.nn.sigmoid(proj(2432, 2560))

    mq = proj(2560, 3072).astype(BF16)
    for h in range(MEM_HEADS):
        cols = slice(h * MEM_DH, (h + 1) * MEM_DH)
        p = _softmax(_dot_nt(mq[:, cols], mkv_ref[0, :, cols]) * (MEM_DH ** -0.5))
        om_ref[0, :, cols] = _dot(p.astype(BF16), mkv_ref[0, :, MEM_Q + h * MEM_DH:MEM_Q + (h + 1) * MEM_DH]).astype(BF16)


def _inproj_prompt(x, g1, w, wkvt, lng, lnb, ws, bst, mkv_bf):
    b, t, _ = x.shape
    tm = min(512, t)

    def row(width):
        return pl.BlockSpec((1, tm, width), lambda bi, i: (bi, i, 0))

    def col(height):
        return pl.BlockSpec((1, height, tm), lambda bi, i: (bi, 0, i))

    def heads(n):
        return pl.BlockSpec((1, n, tm, NSA_DH), lambda bi, i: (bi, 0, i, 0))

    return pl.pallas_call(
        functools.partial(_inproj_prompt_body, tm=tm), grid=(b, t // tm),
        in_specs=[row(D_MODEL), _full((1, D_MODEL)), _full(w.shape), _full(wkvt.shape),
                  _full((1, SGU_DIM)), _full((1, SGU_DIM)),
                  _full((SGU_GROUPS, CHUNK, CHUNK)), _full((CHUNK, SGU_GROUPS)),
                  pl.BlockSpec((1, MEM_TOKENS, 2 * MEM_Q), lambda bi, i: (bi, 0, 0))],
        out_specs=[row(SGU_DIM), heads(NSA_HEADS), col(512), col(256),
                   pl.BlockSpec((1, 4, tm // LANES, NSA_DH, LANES), lambda bi, i: (bi, 0, i, 0, 0)),
                   heads(4), row(256),
                   pl.BlockSpec((1, 2, tm, LANES), lambda bi, i: (bi, 0, i, 0)), row(MEM_Q)],
        out_shape=[jax.ShapeDtypeStruct((b, t, SGU_DIM), BF16), jax.ShapeDtypeStruct((b, NSA_HEADS, t, NSA_DH), BF16),
                   jax.ShapeDtypeStruct((b, 512, t), F32), jax.ShapeDtypeStruct((b, 256, t), F32),
                   jax.ShapeDtypeStruct((b, 4, t // LANES, NSA_DH, LANES), BF16),
                   jax.ShapeDtypeStruct((b, 4, t, NSA_DH), BF16), jax.ShapeDtypeStruct((b, t, 256), BF16),
                   jax.ShapeDtypeStruct((b, 2, t, LANES), F32), jax.ShapeDtypeStruct((b, t, MEM_Q), BF16)],
        compiler_params=_params(2), name="inproj_prompt")(x, g1, w, wkvt, lng, lnb, ws, bst, mkv_bf)


def _inproj_sample_body(x_ref, g_ref, w_ref, lng_ref, lnb_ref, wsc_ref, bsc_ref,
                        oa_ref, z2_ref, q_ref, kv_ref, gate_ref, mq_ref):
    ab = _rms(x_ref[...], g_ref[...]).astype(BF16)

    def proj(lo, hi):
        return _dot(ab, w_ref[:, lo:hi])

    z1 = jax.nn.gelu(proj(0, 512))
    z2 = _layernorm(jax.nn.gelu(proj(512, 1024)), lng_ref[...], lnb_ref[...])
    z2_ref[...] = z2
    oa_ref[...] = (z1 * (wsc_ref[...] * z2 + bsc_ref[...])).astype(BF16)
    q_ref[...] = proj(1024, 2048).astype(BF16)
    kv_ref[...] = proj(2048, 2816)
    gate_ref[...] = jax.nn.sigmoid(proj(2816, 2944))
    mq_ref[...] = proj(2944, 3456)


def _inproj_sample(x, g1, w, lng, lnb, wsc, bsc):
    n = x.shape[0]
    shapes = [((n, SGU_DIM), BF16), ((n, SGU_DIM), F32), ((n, 2 * NSA_Q), BF16), ((n, 768), F32),
              ((n, LANES), F32), ((n, MEM_Q), F32)]
    return pl.pallas_call(
        _inproj_sample_body, grid=(1,),
        in_specs=[_full(x.shape), _full((1, D_MODEL)), _full(w.shape), _full((1, SGU_DIM)), _full((1, SGU_DIM)),
                  _full((1, SGU_DIM)), _full((1, SGU_DIM))],
        out_specs=[_full(s) for s, _ in shapes],
        out_shape=[jax.ShapeDtypeStruct(s, d) for s, d in shapes],
        compiler_params=_params(1), name="inproj_sample")(x, g1, w, lng, lnb, wsc, bsc)


def _compress_finish(ab, b1, w2):
    n = ab.shape[0]
    c = ab[:, :LANES] + pltpu.roll(ab[:, LANES:], n - 1, 0) + b1
    return _dot(jax.nn.gelu(c).astype(BF16), w2)


def _compress_prompt_body(r_ref, wk_ref, b1_ref, w2_ref, o_ref):
    out = _compress_finish(_dot(r_ref[0, 0], wk_ref[0]), b1_ref[0], w2_ref[0])
    for g in range(NSA_KV):
        o_ref[0, 0, g] = out[:, g * NSA_DH:(g + 1) * NSA_DH].astype(BF16)


def _compress_prompt(r, wk, b1, w2):
    _, b, n, width = r.shape
    return pl.pallas_call(
        _compress_prompt_body, grid=(2, b),
        in_specs=[pl.BlockSpec((1, 1, n, width), lambda k, bi: (k, bi, 0, 0)),
                  pl.BlockSpec((1, width, 2 * LANES), lambda k, bi: (k, 0, 0)),
                  pl.BlockSpec((1, 1, LANES), lambda k, bi: (k, 0, 0)),
                  pl.BlockSpec((1, LANES, LANES), lambda k, bi: (k, 0, 0))],
        out_specs=pl.BlockSpec((1, 1, NSA_KV, n, NSA_DH), lambda k, bi: (k, bi, 0, 0, 0)),
        out_shape=jax.ShapeDtypeStruct((2, b, NSA_KV, n, NSA_DH), BF16),
        compiler_params=_params(2), name="compress_prompt")(r, wk, b1, w2)


def _topk_rows(w, k, on_pick):
    n = w.shape[0]
    jidx = lax.broadcasted_iota(I32, w.shape, 0)
    for r in range(k):
        m = jnp.max(w, axis=0, keepdims=True)
        first = jnp.min(jnp.where(w == m, jidx, n), axis=0, keepdims=True)
        pick = jidx == first
        on_pick(r, first, pick)
        w = jnp.where(pick, -jnp.inf, w)


def _nsa_prompt_body(q_ref, g_ref, kc_ref, vc_ref, kst_ref, vs_ref, kwt_ref, vw_ref, mt_ref, es_ref, o_ref):
    ns = mt_ref.shape[0]
    hr = HEAD_ROWS
    n_units, rows = NSA_REP // hr, hr * Q_BLOCK
    blk = pl.program_id(2)
    s0 = blk * Q_BLOCK
    tq = s0 + lax.broadcasted_iota(I32, (Q_BLOCK, 1), 0)
    tcol = jnp.concatenate([tq] * hr, axis=0)
    q = [q_ref[0, u * hr:(u + 1) * hr].reshape(rows, NSA_DH) for u in range(n_units)]

    ncp = kc_ref.shape[-2]
    cend = lax.broadcasted_iota(I32, (1, ncp), 1) * CMP_STRIDE + (CMP_BLOCK - 1)
    cmask = cend <= tcol
    o_c, psum = [], None
    for u in range(n_units):
        probs = _masked_probs(_dot_nt(q[u], kc_ref[0, 0, 0]), cmask)
        o_c.append(_dot(probs.astype(BF16), vc_ref[0, 0, 0]))
        for r in range(hr):
            part = probs[r * Q_BLOCK:(r + 1) * Q_BLOCK]
            psum = part if psum is None else psum + part

    hi, lo = _split_hi_lo(psum)
    imp = _dot_nt(mt_ref[...], hi) + _dot_nt(mt_ref[...], lo)
    jidx = lax.broadcasted_iota(I32, (ns, Q_BLOCK), 0)
    cur = lax.shift_right_logical(s0 + lax.broadcasted_iota(I32, (ns, Q_BLOCK), 1), 6)
    valid = jidx <= cur
    forced = valid & ((jidx == 0) | (jidx > cur - N_LOCAL))
    w = jnp.where(forced, jnp.inf, jnp.where(valid, imp, -jnp.inf))
    picked = [jnp.zeros((ns, Q_BLOCK), F32)]

    def on_pick(r, first, pick):
        picked[0] = jnp.where(pick, 1.0, picked[0])

    _topk_rows(w, min(SEL_TOPK, ns), on_pick)
    bias = jnp.where(valid, jnp.where(picked[0] > 0.0, 0.0, MASKED), MASKED)
    bias_t = bias.T.astype(BF16)

    bias_rows = jnp.concatenate([bias_t] * hr, axis=0)
    q_aug = [jnp.concatenate([bias_rows, q[u]], axis=1) for u in range(n_units)]
    tiles_per = KV_TILE // LANES

    def keys_aug(kt):
        kt4 = jnp.concatenate([kst_ref[0, 0, kt * tiles_per + j] for j in range(tiles_per)], axis=1)
        return jnp.concatenate([es_ref[kt], kt4], axis=0)

    def update(carry, qa, k_aug, v, mask):
        m, l, acc = carry
        sc = _dot(qa, k_aug)
        if mask is not None:
            sc = jnp.where(mask, sc, MASKED)
        m_new = jnp.maximum(m, jnp.max(sc, axis=-1, keepdims=True))
        alpha = jnp.exp(m - m_new)
        p = jnp.exp(sc - m_new)
        return m_new, alpha * l + jnp.sum(p, axis=-1, keepdims=True), alpha * acc + _dot(p.astype(BF16), v)

    def step(kt, carry):
        k_aug = keys_aug(kt)
        v = vs_ref[0, 0, pl.ds(pl.multiple_of(kt * KV_TILE, KV_TILE), KV_TILE), :]
        return tuple(update(carry[u], q_aug[u], k_aug, v, None) for u in range(n_units))

    one = (jnp.full((rows, 1), -jnp.inf, F32), jnp.zeros((rows, 1), F32), jnp.zeros((rows, NSA_DH), F32))
    last = (s0 + Q_BLOCK - 1) // KV_TILE
    carry = lax.fori_loop(0, last // 2, lambda p, c: step(2 * p + 1, step(2 * p, c)), (one,) * n_units)
    carry = lax.cond(last % 2 == 1, lambda c: step(last - 1, c), lambda c: c, carry)
    k0 = pl.multiple_of(last * KV_TILE, KV_TILE)
    causal = k0 + lax.broadcasted_iota(I32, (1, KV_TILE), 1) <= tcol
    k_aug, v = keys_aug(last), vs_ref[0, 0, pl.ds(k0, KV_TILE), :]
    o_s = []
    for u in range(n_units):
        _, l, acc = update(carry[u], q_aug[u], k_aug, v, causal)
        o_s.append(acc / jnp.maximum(l, 1e-30))

    span = WINDOW + Q_BLOCK
    w0 = jnp.maximum(blk - WINDOW // Q_BLOCK, 0)
    start = pl.multiple_of(w0 * Q_BLOCK, Q_BLOCK)
    kw = jnp.concatenate([kwt_ref[0, 0, w0 + j] for j in range(span // LANES)], axis=1)
    vw = vw_ref[0, 0, pl.ds(start, span), :]
    wpos = start + lax.broadcasted_iota(I32, (1, span), 1)
    wmask = (wpos <= tcol) & (wpos > tcol - WINDOW)

    gates = g_ref[0, 0]
    outs = []
    for u in range(n_units):
        pw = _masked_probs(_dot(q[u], kw), wmask)
        o_w = _dot(pw.astype(BF16), vw)
        for r in range(hr):
            h, sl = u * hr + r, slice(r * Q_BLOCK, (r + 1) * Q_BLOCK)
            outs.append(gates[:, h:h + 1] * o_c[u][sl] + gates[:, 4 + h:5 + h] * o_s[u][sl] + gates[:, 8 + h:9 + h] * o_w[sl])
    o_ref[0] = jnp.concatenate(outs, axis=1).astype(BF16)


def _nsa_prompt(qh, gates, kcv, kt, v, mt, es):
    b, _, t, _ = qh.shape
    ncp = kcv.shape[-2]

    def keys(c):
        return pl.BlockSpec((1, 1, t // LANES, NSA_DH, LANES), lambda bi, g, i: (bi, c + g, 0, 0, 0))

    def vals(c):
        return pl.BlockSpec((1, 1, t, NSA_DH), lambda bi, g, i: (bi, c + g, 0, 0))

    def cmp(k):
        return pl.BlockSpec((1, 1, 1, ncp, NSA_DH), lambda bi, g, i: (k, bi, g, 0, 0))

    return pl.pallas_call(
        _nsa_prompt_body, grid=(b, NSA_KV, t // Q_BLOCK),
        in_specs=[pl.BlockSpec((1, NSA_REP, Q_BLOCK, NSA_DH), lambda bi, g, i: (bi, g, i, 0)),
                  pl.BlockSpec((1, 1, Q_BLOCK, LANES), lambda bi, g, i: (bi, g, i, 0)),
                  cmp(0), cmp(1), keys(0), vals(0), keys(2), vals(2), _full(mt.shape), _full(es.shape)],
        out_specs=pl.BlockSpec((1, Q_BLOCK, NSA_REP * NSA_DH), lambda bi, g, i: (bi, i, g)),
        out_shape=jax.ShapeDtypeStruct((b, t, NSA_Q), BF16),
        compiler_params=_params(3), name="nsa_prompt")(qh, gates, kcv, kcv, kt, v, kt, v, mt, es)


def _sample_cmp_body(pt_ref, *refs, n_chunks):
    pc = PAGES_PER_STEP
    pages = refs[:pc]
    q_ref, wk_ref, b1_ref, w2_ref, m_ref, oc_ref, imp_ref, tp_ref, rk_ref, ab_ref = refs[pc:]
    c = pl.program_id(1)
    per_page = 128 // CMP_STRIDE
    for kind in range(2):
        for k in range(pc):
            tp_ref[kind * pc + k] = pages[k][0, kind * LANES:(kind + 1) * LANES, :].T
        for k in range(0, pc, 2):
            for j in range(CMP_STRIDE):
                both = [tp_ref[kind * pc + k + d, pl.ds(j, per_page, stride=CMP_STRIDE), :] for d in range(2)]
                rk_ref[kind, k * per_page:(k + 2) * per_page, j * LANES:(j + 1) * LANES] = (
                    jnp.concatenate(both, axis=0).astype(BF16))
    row0 = pl.multiple_of(c * (pc * per_page), pc * per_page)
    for kind in range(2):
        ab_ref[kind, pl.ds(row0, pc * per_page), :] = _dot(rk_ref[kind], wk_ref[kind])

    @pl.when(c == n_chunks - 1)
    def _():
        kc = _compress_finish(ab_ref[0], b1_ref[0], w2_ref[0]).astype(BF16)
        vc = _compress_finish(ab_ref[1], b1_ref[1], w2_ref[1]).astype(BF16)
        ncp = kc.shape[0]
        row = lax.broadcasted_iota(I32, (NSA_HEADS, LANES), 0)
        lane = lax.broadcasted_iota(I32, (NSA_HEADS, LANES), 1)
        qbd = jnp.where((lane >= NSA_DH) == (row >= NSA_REP), q_ref[0], jnp.zeros((), BF16))
        s = _dot_nt(qbd, kc)
        n = lax.broadcasted_iota(I32, (1, ncp), 1)
        probs = _masked_probs(s, n * CMP_STRIDE + (CMP_BLOCK - 1) <= ncp * CMP_STRIDE)
        oc_ref[0] = _dot(probs.astype(BF16), vc)
        hi, lo = _split_hi_lo(probs)
        imp8 = _dot(hi, m_ref[...]) + _dot(lo, m_ref[...])
        r8 = lax.broadcasted_iota(I32, imp8.shape, 0)
        imp_ref[0, 0:1, :] = jnp.sum(jnp.where(r8 < NSA_REP, imp8, 0.0), axis=0, keepdims=True)
        imp_ref[0, 1:2, :] = jnp.sum(jnp.where(r8 >= NSA_REP, imp8, 0.0), axis=0, keepdims=True)


def _sample_cmp(pt_flat, cache_t, qrep, wk, b1, w2, m, n_pages):
    nb = qrep.shape[0]
    pc = PAGES_PER_STEP
    n_chunks = n_pages // pc
    ncp = n_pages * 128 // CMP_STRIDE
    ns = m.shape[1]

    def page(k):
        return pl.BlockSpec((1, 2 * LANES, 128), lambda b, c, pt: (pt[b * n_pages + c * pc + k], 0, 0))

    grid_spec = pltpu.PrefetchScalarGridSpec(
        num_scalar_prefetch=1, grid=(nb, n_chunks),
        in_specs=[page(k) for k in range(pc)] + [
            pl.BlockSpec((1, NSA_HEADS, LANES), lambda b, c, pt: (b, 0, 0)),
            pl.BlockSpec(wk.shape, lambda b, c, pt: (0, 0, 0)), pl.BlockSpec(b1.shape, lambda b, c, pt: (0, 0, 0)),
            pl.BlockSpec(w2.shape, lambda b, c, pt: (0, 0, 0)), pl.BlockSpec(m.shape, lambda b, c, pt: (0, 0))],
        out_specs=[pl.BlockSpec((1, NSA_HEADS, LANES), lambda b, c, pt: (b, 0, 0)),
                   pl.BlockSpec((1, NSA_KV, ns), lambda b, c, pt: (b, 0, 0))],
        scratch_shapes=[pltpu.VMEM((2 * pc, 128, LANES), F32), pltpu.VMEM((2, pc * 8, CMP_STRIDE * LANES), BF16),
                        pltpu.VMEM((2, ncp, 2 * LANES), F32)])
    return pl.pallas_call(
        functools.partial(_sample_cmp_body, n_chunks=n_chunks), grid_spec=grid_spec,
        out_shape=[jax.ShapeDtypeStruct((nb, NSA_HEADS, LANES), F32), jax.ShapeDtypeStruct((nb, NSA_KV, ns), F32)],
        compiler_params=_params(2), name="sample_compress")(pt_flat, *([cache_t] * pc), qrep, wk, b1, w2, m)


def _sample_topk_body(imp_ref, idx_ref, *, k):
    ns = imp_ref.shape[1]
    idx_ref[...] = jnp.zeros(idx_ref.shape, I32)
    for g in range(NSA_KV):
        jidx = lax.broadcasted_iota(I32, imp_ref.shape[1:], 0)
        w = jnp.where((jidx == 0) | (jidx > ns - N_LOCAL), jnp.inf, imp_ref[g])

        def on_pick(r, first, pick, g=g):
            idx_ref[g, r:r + 1, :] = first

        _topk_rows(w, k, on_pick)


def _sample_topk(imp_t, k):
    _, ns, nb = imp_t.shape
    return pl.pallas_call(
        functools.partial(_sample_topk_body, k=k), grid=(1,),
        in_specs=[_full(imp_t.shape)], out_specs=_full((NSA_KV, SEL_TOPK, nb)),
        out_shape=jax.ShapeDtypeStruct((NSA_KV, SEL_TOPK, nb), I32),
        compiler_params=_params(1), name="sample_topk")(imp_t)


def _attend_with_self(qg, keys_t, vals_t, k_self, v_self, mask):
    s = _dot(qg, keys_t)
    if mask is not None:
        s = jnp.where(mask, s, -jnp.inf)
    s_self = jnp.sum(qg.astype(F32) * k_self.astype(BF16).astype(F32), axis=-1, keepdims=True)
    m = jnp.maximum(jnp.max(s, axis=-1, keepdims=True), s_self)
    p = jnp.exp(s - m)
    p_self = jnp.exp(s_self - m)
    den = jnp.sum(p, axis=-1, keepdims=True) + p_self
    num = _dot_nt(p.astype(BF16), vals_t) + p_self.astype(BF16).astype(F32) * v_self.astype(BF16).astype(F32)
    return num / den


def _sample_attn_body(idx_ref, pt_ref, *refs, n_sel):
    n_blk = NSA_KV * n_sel
    kblocks, vblocks = refs[:n_blk], refs[n_blk:2 * n_blk]
    (win_ref, q_ref, kvn_ref, gate_ref, oc_ref, mq_ref, mem_ref,
     ob_ref, om_ref, wn_ref, ks_ref, vs_ref) = refs[2 * n_blk:]
    b = pl.program_id(0)
    row = lax.broadcasted_iota(I32, (NSA_HEADS, NSA_DH), 0)
    q8 = q_ref[0][:, :NSA_DH]
    qg = [jnp.where((row >= NSA_REP) == (g == 1), q8, jnp.zeros((), BF16)) for g in range(NSA_KV)]
    kvn = kvn_ref[0]

    def new_row(kind, g):
        lo = kind * NSA_KVW + g * NSA_DH
        return kvn[:, lo:lo + NSA_DH]

    lane = lax.broadcasted_iota(I32, (1, n_sel * LANES), 1)
    upper = (lane & SEL_BLOCK) != 0
    slot = lax.shift_right_logical(lane, 7)
    o_sel = []
    for g in range(NSA_KV):
        want_upper = jnp.zeros((1, n_sel * LANES), I32)
        for k in range(n_sel):
            ks_ref[g, :, k * LANES:(k + 1) * LANES] = kblocks[g * n_sel + k][0].astype(BF16)
            vs_ref[g, :, k * LANES:(k + 1) * LANES] = vblocks[g * n_sel + k][0].astype(BF16)
            j = idx_ref[b * (NSA_KV * SEL_TOPK) + g * SEL_TOPK + k]
            want_upper = jnp.where(slot == k, j & 1, want_upper)
        o_sel.append(_attend_with_self(qg[g], ks_ref[g], vs_ref[g], new_row(2, g), new_row(3, g),
                                       upper == (want_upper != 0)))
    o_s = jnp.where(row < NSA_REP, o_sel[0], o_sel[1])

    win = win_ref[0]
    wbuf = win.shape[1]
    wcol = lax.broadcasted_iota(I32, (1, wbuf), 1)
    o_win = []
    for g in range(NSA_KV):
        kt = win[g * NSA_DH:(g + 1) * NSA_DH].astype(BF16)
        vt = win[NSA_KVW + g * NSA_DH:NSA_KVW + (g + 1) * NSA_DH].astype(BF16)
        o_win.append(_attend_with_self(qg[g], kt, vt, new_row(4, g), new_row(5, g), wcol >= 1))
    o_w = jnp.where(row < NSA_REP, o_win[0], o_win[1])
    rows_w = 2 * NSA_KVW
    eye = lax.broadcasted_iota(I32, (rows_w, rows_w), 0) == lax.broadcasted_iota(I32, (rows_w, rows_w), 1)
    new_col = jnp.sum(jnp.where(eye, kvn[:, 4 * NSA_KVW:], 0.0), axis=-1, keepdims=True)
    wn_ref[0] = jnp.where(lax.broadcasted_iota(I32, win.shape, 1) == wbuf - 1, new_col, pltpu.roll(win, wbuf - 1, 1))

    gate = gate_ref[0]
    grow = lax.broadcasted_iota(I32, (NSA_HEADS, LANES), 0)
    glane = lax.broadcasted_iota(I32, (NSA_HEADS, LANES), 1)
    oc = jnp.where((glane >= NSA_DH) == (grow >= NSA_REP), oc_ref[0], 0.0)
    o = jnp.zeros((NSA_HEADS, NSA_DH), F32)
    for j, branch in enumerate((oc[:, :NSA_DH] + oc[:, NSA_DH:], o_s, o_w)):
        gj = jnp.sum(jnp.where(glane == j * NSA_HEADS + grow, gate, 0.0), axis=-1, keepdims=True)
        o = o + gj * branch
    ob_ref[0] = o

    mq = mq_ref[0]
    per_tok = 2 * MEM_HEADS
    for h in range(MEM_HEADS):
        qh = jnp.broadcast_to(mq[:, h * MEM_DH:(h + 1) * MEM_DH], (NSA_HEADS, MEM_DH)).astype(BF16)
        km = mem_ref[0, pl.ds(h, MEM_TOKENS, stride=per_tok), :].astype(BF16)
        vm = mem_ref[0, pl.ds(MEM_HEADS + h, MEM_TOKENS, stride=per_tok), :].astype(BF16)
        p = _softmax(_dot_nt(qh, km) * (MEM_DH ** -0.5))
        om_ref[0, :, h * MEM_DH:(h + 1) * MEM_DH] = _dot(p.astype(BF16), vm)[0:1]


def _sample_attn(idx_flat, pt_flat, cache_t, win_t, qrep, kvn, gate, oc, mq, memkv, n_pages, n_sel):
    nb, rows_w, wbuf = win_t.shape

    def sel_block(kind, g, k):
        def index(b, idx, pt):
            j = idx[b * (NSA_KV * SEL_TOPK) + g * SEL_TOPK + k]
            return (pt[b * n_pages + lax.shift_right_logical(j, 1)], kind * NSA_KV + g, 0)
        return pl.BlockSpec((1, NSA_DH, 128), index)

    def per_seq(shape):
        return pl.BlockSpec((1,) + shape, lambda b, idx, pt: (b,) + (0,) * len(shape))

    grid_spec = pltpu.PrefetchScalarGridSpec(
        num_scalar_prefetch=2, grid=(nb,),
        in_specs=[sel_block(kind, g, k) for kind in (2, 3) for g in range(NSA_KV) for k in range(n_sel)] + [
            per_seq((rows_w, wbuf)), per_seq((NSA_HEADS, LANES)), per_seq((1, 768)), per_seq((1, LANES)),
            per_seq((NSA_HEADS, LANES)), per_seq((1, MEM_Q)), per_seq(memkv.shape[1:])],
        out_specs=[per_seq((NSA_HEADS, NSA_DH)), per_seq((1, MEM_Q)), per_seq((rows_w, wbuf))],
        scratch_shapes=[pltpu.VMEM((NSA_KV, NSA_DH, n_sel * LANES), BF16)] * 2)
    return pl.pallas_call(
        functools.partial(_sample_attn_body, n_sel=n_sel), grid_spec=grid_spec,
        out_shape=[jax.ShapeDtypeStruct((nb, NSA_HEADS, NSA_DH), F32), jax.ShapeDtypeStruct((nb, 1, MEM_Q), F32),
                   jax.ShapeDtypeStruct((nb, rows_w, wbuf), F32)],
        compiler_params=_params(1), name="sample_attn")(
            idx_flat, pt_flat, *([cache_t] * (2 * NSA_KV * n_sel)), win_t, qrep, kvn, gate, oc, mq, memkv)


def _merge_body(x_ref, oa_ref, ob_ref, om_ref, g1_ref, wmg_ref, wb_ref, wo_ref, g2_ref, rwh_ref, rwl_ref, rb_ref,
                h1_ref, hn_ref, gt_ref):
    x = x_ref[...]
    ab = _rms(x, g1_ref[...]).astype(BF16)
    s = None
    for k, o_ref in enumerate((oa_ref, ob_ref, om_ref)):
        gk = jax.nn.sigmoid(_dot(ab, wmg_ref[:, k * D_MODEL:(k + 1) * D_MODEL]))
        term = gk * _dot(o_ref[...], wb_ref[k])
        s = term if s is None else s + term
    h1 = x + _dot(s.astype(BF16), wo_ref[...])
    h1_ref[...] = h1
    hn = _rms(h1, g2_ref[...])
    hi, lo = _split_hi_lo(hn)
    hn_ref[...] = hi

    logit = _dot_nt(rwh_ref[...], hi) + _dot_nt(rwh_ref[...], lo) + _dot_nt(rwl_ref[...], hi) + rb_ref[...]
    picked = [jnp.zeros(logit.shape, F32)]

    def on_pick(r, first, pick):
        picked[0] = jnp.where(pick, 1.0, picked[0])

    _topk_rows(logit, TOP_K, on_pick)
    top = jnp.max(logit, axis=0, keepdims=True)
    ex = jnp.where(picked[0] > 0.0, jnp.exp(logit - top), 0.0)
    gt_ref[...] = ex / jnp.sum(ex, axis=0, keepdims=True)


def _merge(x, oa, ob, om, g1, wmg, wb, wo, g2, rwh, rwl, rb, tm):
    n = x.shape[0]

    def row(width):
        return pl.BlockSpec((tm, width), lambda i: (i, 0))

    return pl.pallas_call(
        _merge_body, grid=(n // tm,),
        in_specs=[row(D_MODEL), row(512), row(512), row(512), _full((1, D_MODEL)), _full(wmg.shape), _full(wb.shape),
                  _full(wo.shape), _full((1, D_MODEL)), _full(rwh.shape), _full(rwl.shape), _full(rb.shape)],
        out_specs=[row(D_MODEL), row(D_MODEL), pl.BlockSpec((N_EXPERTS, tm), lambda i: (0, i))],
        out_shape=[jax.ShapeDtypeStruct((n, D_MODEL), F32), jax.ShapeDtypeStruct((n, D_MODEL), BF16),
                   jax.ShapeDtypeStruct((N_EXPERTS, n), F32)],
        compiler_params=_params(1), name="merge")(x, oa, ob, om, g1, wmg, wb, wo, g2, rwh, rwl, rb)


def _moe_body(cnt_ref, hn_ref, h1_ref, gt_ref, u_ref, wgu_ref, bgu_ref, wd_ref, bd_ref, fg_ref, y_ref,
              acc_ref, rank_ref, *, cap):
    i, e = pl.program_id(0), pl.program_id(1)
    t = hn_ref.shape[0]

    @pl.when(e == 0)
    def _():
        sel = jnp.where(gt_ref[...] > 0.0, 1.0, 0.0).astype(BF16)
        rank_ref[...] = _dot(sel, u_ref[...])
        acc_ref[...] = jnp.zeros(acc_ref.shape, F32)

    g_row = gt_ref[pl.ds(e, 1), :]
    r_row = rank_ref[pl.ds(e, 1), :]
    slot = lax.broadcasted_iota(I32, (cap, t), 0).astype(F32)

    def chunk(ch, carry):
        want = slot + (ch * cap).astype(F32)
        hit = (g_row > 0.0) & (r_row == want)
        onehot = jnp.where(hit, 1.0, 0.0).astype(BF16)
        xe = _dot(onehot, hn_ref[...]).astype(BF16)
        gu = _dot(xe, wgu_ref[0]) + bgu_ref[0]
        gate = jnp.minimum(gu[:, :D_EXPERT], SWIGLU_LIMIT)
        up = jnp.clip(gu[:, D_EXPERT:], -SWIGLU_LIMIT, SWIGLU_LIMIT)
        act = (up + 1.0) * gate * jax.nn.sigmoid(SWIGLU_ALPHA * gate)
        ye = _dot(act.astype(BF16), wd_ref[0]) + bd_ref[0]
        ye = ye * jnp.sum(jnp.where(hit, g_row, 0.0), axis=-1, keepdims=True)
        acc_ref[...] += _dot_tn(onehot, ye.astype(BF16))
        return carry

    n_chunks = (cnt_ref[i * N_EXPERTS + e] + cap - 1) // cap
    lax.fori_loop(0, n_chunks, chunk, 0)

    @pl.when(e == N_EXPERTS - 1)
    def _():
        y_ref[...] = _rms(h1_ref[...] + acc_ref[...], fg_ref[...])


def _moe(counts, hn, h1, gate_t, upper, wgu, bgu, wd, bd, fg, tile, cap):
    n = hn.shape[0]
    grid_spec = pltpu.PrefetchScalarGridSpec(
        num_scalar_prefetch=1, grid=(n // tile, N_EXPERTS),
        in_specs=[pl.BlockSpec((tile, D_MODEL), lambda i, e, c: (i, 0)),
                  pl.BlockSpec((tile, D_MODEL), lambda i, e, c: (i, 0)),
                  pl.BlockSpec((N_EXPERTS, tile), lambda i, e, c: (0, i)),
                  pl.BlockSpec((tile, tile), lambda i, e, c: (0, 0)),
                  pl.BlockSpec((1, D_MODEL, 2 * D_EXPERT), lambda i, e, c: (e, 0, 0)),
                  pl.BlockSpec((1, 1, 2 * D_EXPERT), lambda i, e, c: (e, 0, 0)),
                  pl.BlockSpec((1, D_EXPERT, D_MODEL), lambda i, e, c: (e, 0, 0)),
                  pl.BlockSpec((1, 1, D_MODEL), lambda i, e, c: (e, 0, 0)),
                  pl.BlockSpec((1, D_MODEL), lambda i, e, c: (0, 0))],
        out_specs=pl.BlockSpec((tile, D_MODEL), lambda i, e, c: (i, 0)),
        scratch_shapes=[pltpu.VMEM((tile, D_MODEL), F32), pltpu.VMEM((N_EXPERTS, tile), F32)])
    return pl.pallas_call(
        functools.partial(_moe_body, cap=cap), grid_spec=grid_spec,
        out_shape=jax.ShapeDtypeStruct((n, D_MODEL), F32),
        compiler_params=_params(2), name="moe")(counts, hn, h1, gate_t, upper, wgu, bgu, wd, bd, fg)


def _gate_columns(order):
    idx = np.zeros((LANES,), np.int32)
    ok = np.zeros((LANES,), bool)
    for c, (h, j) in enumerate(order):
        idx[c], ok[c] = C_NG + h * 3 + j, True
    return idx, ok


def _prep_in_weights(w_in):
    def sect(lo, hi):
        return w_in[:, lo:hi]

    def gates(order):
        idx, ok = _gate_columns(order)
        return jnp.where(jnp.asarray(ok)[None, :], w_in[:, idx], 0.0)

    w_q = sect(C_Q, C_KV) * (NSA_DH ** -0.5)
    g_prompt = [gates([(NSA_REP * g + r, j) for j in range(3) for r in range(NSA_REP)]) for g in range(NSA_KV)]
    w_prompt = jnp.concatenate([sect(C_U, C_Q), w_q, sect(C_KV, C_NG)] + g_prompt + [sect(C_MQ, C_MG)], axis=1)
    q_rep = jnp.concatenate([w_q.reshape(D_MODEL, NSA_HEADS, 1, NSA_DH)] * 2, axis=2).reshape(D_MODEL, 2 * NSA_Q)
    g_sample = gates([(h, j) for j in range(3) for h in range(NSA_HEADS)])
    w_sample = jnp.concatenate([sect(C_U, C_Q), q_rep, sect(C_KV, C_NG), g_sample, sect(C_MQ, C_MG)], axis=1)
    w_kv_t = w_in.T[C_KV:C_NG]
    return w_prompt.astype(BF16), w_sample.astype(BF16), sect(C_MG, w_in.shape[1]).astype(BF16), w_kv_t.astype(BF16)


def _prep_compress(w1, b1, w2):
    eye = jnp.eye(NSA_KV, dtype=F32)
    w = w1.reshape(2, 2, CMP_STRIDE, NSA_DH, NSA_DH)
    wk = jnp.einsum('khjde,xg->kjxdhge', w, eye).reshape(2, CMP_STRIDE * LANES, 2 * LANES)
    b = jnp.concatenate([b1, b1], axis=-1).reshape(2, 1, LANES)
    w2d = jnp.einsum('kde,xg->kxdge', w2, eye).reshape(2, LANES, LANES)
    return wk.astype(BF16), b, w2d.astype(BF16)


def _cmp_to_sel_t(ncp, ns):
    cs = np.arange(ncp) * CMP_STRIDE
    ss = np.arange(ns) * SEL_BLOCK
    m = (cs[None, :] <= ss[:, None] + SEL_BLOCK - 1) & (cs[None, :] + CMP_BLOCK - 1 >= ss[:, None])
    m[:, ncp - 1] = False
    return m.astype(np.float32)


def _block_onehot(t):
    e = (np.arange(t // SEL_BLOCK)[:, None] == (np.arange(t) // SEL_BLOCK)[None, :]).astype(np.float32)
    return e.reshape(t // SEL_BLOCK, t // KV_TILE, KV_TILE).transpose(1, 0, 2)


def _moe_group(h1_and_gates, weights, tile, cap):
    h1, hn, gate_t = h1_and_gates
    n = h1.shape[0]
    counts = jnp.sum((gate_t > 0.0).reshape(N_EXPERTS, n // tile, tile), axis=-1, dtype=I32).T.reshape(-1)
    upper = jnp.asarray(np.triu(np.ones((tile, tile), np.float32), 1), BF16)
    return _moe(counts, hn, h1, gate_t, upper, *weights, tile, cap)


def kernel(x_prompt, x_sample, mem_prompt, cache_nsa_kv, cache_nsa_win, cache_mem_kv, page_table,
           norm1_g, w_in, sgu_ln_g, sgu_ln_b, sgu_w, sgu_b, cmp_w1, cmp_b1, cmp_w2, w_mem_kv,
           w_branch, w_out, norm2_g, router_w, router_b, exp_w_gate_up, exp_b_gate_up,
           exp_w_down, exp_b_down, final_norm_g):
    assert norm1_g.shape[0] == 1, "one layer"
    bp, t, _ = x_prompt.shape
    nb = x_sample.shape[0]
    n_pool, page = cache_nsa_kv.shape[1:3]
    n_pages = page_table.shape[1]
    past = n_pages * page
    wbuf = cache_nsa_win.shape[2]
    assert page == 128 and wbuf == WINDOW and n_pages % PAGES_PER_STEP == 0 and t % KV_TILE == 0 and t >= WINDOW + Q_BLOCK

    g1, g2, fg = norm1_g[0][None], norm2_g[0][None], final_norm_g[None]
    lng, lnb = sgu_ln_g[0][None], sgu_ln_b[0][None]
    w_prompt, w_sample, w_mg, w_kv_t = _prep_in_weights(w_in[0])
    wk, b1c, w2c = _prep_compress(cmp_w1[0], cmp_b1[0], cmp_w2[0])
    wb, wo = w_branch[0].astype(BF16), w_out[0].astype(BF16)
    rwh, rwl = _split_hi_lo(router_w[0].T)
    rb = router_b[0][:, None]
    moe_w = (exp_w_gate_up[0].astype(BF16), exp_b_gate_up[0][:, None, :], exp_w_down[0].astype(BF16),
             exp_b_down[0][:, None, :], fg)
    merge_w = (g1, w_mg, wb, wo, g2, rwh, rwl, rb)

    mkv, mkv_bf = _mem_kv(mem_prompt.reshape(bp * MEM_TOKENS, D_MODEL), w_mem_kv[0].astype(BF16))
    oa, qh, kv4t, kvwt, kt, v, kcv_rows, gates, om = _inproj_prompt(
        x_prompt, g1, w_prompt, w_kv_t, lng, lnb, sgu_w[0], sgu_b[0].T, mkv_bf.reshape(bp, MEM_TOKENS, 2 * MEM_Q))
    ncp, ns = t // CMP_STRIDE, t // SEL_BLOCK
    r = jnp.stack([kcv_rows[:, :, k * LANES:(k + 1) * LANES].reshape(bp, ncp, CMP_STRIDE * LANES) for k in range(2)])
    kcv = _compress_prompt(r, wk, b1c, w2c)
    mt = jnp.asarray(_cmp_to_sel_t(ncp, ns), BF16)
    ob = _nsa_prompt(qh, gates, kcv, kt, v, mt, jnp.asarray(_block_onehot(t), BF16))
    n = bp * t
    x2 = x_prompt.reshape(n, D_MODEL)
    y_prompt = _moe_group(_merge(x2, oa.reshape(n, 512), ob.reshape(n, 512), om.reshape(n, 512), *merge_w, tm=min(512, n)),
                          moe_w, min(MOE_TILE, n), MOE_CAP)

    xs = x_sample.reshape(nb, D_MODEL)
    wsc = jnp.repeat(sgu_w[0][:, 0, 0], SGU_GROUP_DIM)[None]
    bsc = jnp.repeat(sgu_b[0][:, 0], SGU_GROUP_DIM)[None]
    oa_s, z2_s, q_s, kv_s, gate_s, mq_s = _inproj_sample(xs, g1, w_sample, lng, lnb, wsc, bsc)
    qrep = q_s.reshape(nb, NSA_HEADS, LANES)
    pt_flat = page_table.reshape(-1)
    ncp_s, ns_s = past // CMP_STRIDE, past // SEL_BLOCK
    m_s = jnp.asarray(_cmp_to_sel_t(ncp_s, ns_s).T, BF16)
    cache_t = jnp.transpose(cache_nsa_kv[0], (0, 2, 3, 4, 1)).reshape(n_pool, 4 * NSA_KVW, page)
    win_t = jnp.transpose(cache_nsa_win[0], (0, 2, 3, 4, 1)).reshape(nb, 2 * NSA_KVW, wbuf)
    oc_s, imp_s = _sample_cmp(pt_flat, cache_t, qrep, wk, b1c, w2c, m_s, n_pages)
    n_sel = min(SEL_TOPK, ns_s + 1) - 1
    idx = _sample_topk(imp_s.transpose(1, 2, 0), n_sel)
    idx_flat = idx.transpose(2, 0, 1).reshape(-1)
    ob_s, om_s, win_new = _sample_attn(
        idx_flat, pt_flat, cache_t, win_t, qrep, kv_s[:, None, :], gate_s[:, None, :], oc_s, mq_s[:, None, :],
        cache_mem_kv[0].reshape(nb, MEM_TOKENS * 2 * MEM_HEADS, MEM_DH), n_pages, n_sel)
    y_sample = _moe_group(_merge(xs, oa_s, ob_s.reshape(nb, NSA_Q).astype(BF16), om_s.reshape(nb, MEM_Q).astype(BF16),
                                 *merge_w, tm=nb), moe_w, nb, 32)

    def positions_last(x_t, n_kinds):
        lead, _, npos = x_t.shape
        return jnp.transpose(x_t.reshape(lead, n_kinds, NSA_KV, NSA_DH, npos), (0, 4, 1, 2, 3))[None]

    return (y_prompt.reshape(bp, t, D_MODEL), y_sample.reshape(nb, 1, D_MODEL),
            positions_last(kv4t, 4), positions_last(kvwt[:, :, t - WINDOW:], 2),
            mkv.reshape(1, bp, MEM_TOKENS, 2, MEM_HEADS, MEM_DH), kv_s[:, :512].reshape(1, nb, 1, 4, NSA_KV, NSA_DH),
            positions_last(win_new, 2), z2_s.reshape(1, nb, 1, SGU_DIM))
```

```python
import functools

import numpy as np
import jax
import jax.numpy as jnp
from jax import lax
from jax.experimental import pallas as pl
from jax.experimental.pallas import tpu as pltpu

F32, BF16, I32 = jnp.float32, jnp.bfloat16, jnp.int32

D_MODEL = 1024
SGU_GROUPS, SGU_GROUP_DIM, SGU_DIM, CHUNK = 4, 128, 512, 128
NSA_HEADS, NSA_KV, NSA_REP, NSA_DH = 8, 2, 4, 64
CMP_BLOCK, CMP_STRIDE, SEL_BLOCK, SEL_TOPK, N_LOCAL, WINDOW, Q_BLOCK = 32, 16, 64, 16, 2, 512, 128
MEM_TOKENS, MEM_HEADS, MEM_DH = 256, 4, 128
N_EXPERTS, TOP_K, D_EXPERT = 32, 4, 1024
SWIGLU_LIMIT, SWIGLU_ALPHA, EPS = 7.0, 1.702, 1e-5
NSA_Q, NSA_KVW, MEM_Q = NSA_HEADS * NSA_DH, NSA_KV * NSA_DH, MEM_HEADS * MEM_DH
C_U, C_V, C_Q, C_KV, C_NG, C_MQ, C_MG = 0, 512, 1024, 1536, 2304, 2328, 2840

LANES = 128
VMEM_LIMIT = 56 * 1024 * 1024
MASKED = -1e9
KV_TILE = 512
HEAD_ROWS = 4
PAGES_PER_STEP = 16
MOE_TILE, MOE_CAP = 1024, 160
MOE_BLOCK = 2048


def _dot(a, b):
    return jnp.dot(a, b, preferred_element_type=F32)


def _dot_nt(a, b):
    return lax.dot_general(a, b, (((1,), (1,)), ((), ())), preferred_element_type=F32)


def _dot_tn(a, b):
    return lax.dot_general(a, b, (((0,), (0,)), ((), ())), preferred_element_type=F32)


def _rms(x, g):
    return x * lax.rsqrt(jnp.mean(x * x, axis=-1, keepdims=True) + EPS) * g


def _layernorm(x, g, b):
    mu = jnp.mean(x, axis=-1, keepdims=True)
    xc = x - mu
    return xc * lax.rsqrt(jnp.mean(xc * xc, axis=-1, keepdims=True) + EPS) * g + b


def _masked_exp(s, mask):
    s = jnp.where(mask, s, -jnp.inf)
    m = jnp.max(s, axis=-1, keepdims=True)
    m = jnp.where(jnp.isfinite(m), m, 0.0)
    return jnp.exp(s - m)


def _masked_probs(s, mask):
    p = _masked_exp(s, mask)
    return p * (1.0 / jnp.maximum(jnp.sum(p, axis=-1, keepdims=True), 1e-30))


def _softmax(s):
    p = jnp.exp(s - jnp.max(s, axis=-1, keepdims=True))
    return p / jnp.sum(p, axis=-1, keepdims=True)


def _split_hi_lo(x):
    hi = x.astype(BF16)
    return hi, (x - hi.astype(F32)).astype(BF16)


def _params(n_axes):
    return pltpu.CompilerParams(dimension_semantics=("arbitrary",) * n_axes, vmem_limit_bytes=VMEM_LIMIT)


def _full(shape):
    return pl.BlockSpec(shape, lambda *_: (0,) * len(shape))


def _mkv_body(x_ref, w_ref, o_ref, ob_ref):
    r = _dot(x_ref[...].astype(BF16), w_ref[...])
    o_ref[...] = r
    ob_ref[...] = r.astype(BF16)


def _mem_kv(mem2d, w_bf):
    rows, tm = mem2d.shape[0], MEM_TOKENS
    return pl.pallas_call(
        _mkv_body, grid=(rows // tm,),
        in_specs=[pl.BlockSpec((tm, D_MODEL), lambda i: (i, 0)), _full((D_MODEL, 2 * MEM_Q))],
        out_specs=[pl.BlockSpec((tm, 2 * MEM_Q), lambda i: (i, 0))] * 2,
        out_shape=[jax.ShapeDtypeStruct((rows, 2 * MEM_Q), F32), jax.ShapeDtypeStruct((rows, 2 * MEM_Q), BF16)],
        compiler_params=_params(1), name="mem_kv")(mem2d, w_bf)


def _inproj_prompt_body(x_ref, g_ref, w_ref, wkvt_ref, lng_ref, lnb_ref, ws_ref, bst_ref, mkv_ref,
                        oa_ref, qh_ref, kv4t_ref, kvwt_ref, kt_ref, v_ref, kcv_ref, gate_ref, om_ref, *, tm):
    ab = _rms(x_ref[0], g_ref[...]).astype(BF16)

    def proj(lo, hi):
        return _dot(ab, w_ref[:, lo:hi])

    z1 = jax.nn.gelu(proj(0, 512))
    z2 = _layernorm(jax.nn.gelu(proj(512, 1024)), lng_ref[...], lnb_ref[...])
    tril = lax.broadcasted_iota(I32, (CHUNK, CHUNK), 0) >= lax.broadcasted_iota(I32, (CHUNK, CHUNK), 1)
    for g in range(SGU_GROUPS):
        cols = slice(g * SGU_GROUP_DIM, (g + 1) * SGU_GROUP_DIM)
        w = jnp.where(tril, ws_ref[g], 0.0).astype(BF16)
        for c in range(tm // CHUNK):
            rows = slice(c * CHUNK, (c + 1) * CHUNK)
            mixed = _dot(w, z2[rows, cols].astype(BF16)) + bst_ref[:, g:g + 1]
            oa_ref[0, rows, cols] = (z1[rows, cols] * mixed).astype(BF16)

    q = proj(1024, 1536)
    for h in range(NSA_HEADS):
        qh_ref[0, h] = q[:, h * NSA_DH:(h + 1) * NSA_DH].astype(BF16)

    kvt = _dot_nt(wkvt_ref[...], ab)
    kv4t_ref[0] = kvt[:512]
    kvwt_ref[0] = kvt[512:]
    for c, r0 in enumerate((256, 320, 512, 576)):
        for u in range(tm // LANES):
            kt_ref[0, c, u] = kvt[r0:r0 + NSA_DH, u * LANES:(u + 1) * LANES].astype(BF16)
    kv = proj(1536, 2304)
    kcv_ref[0] = kv[:, :256].astype(BF16)
    v_ref[0, 0] = kv[:, 384:512].astype(BF16)
    v_ref[0, 1] = kv[:, 640:768].astype(BF16)

    gate_ref[0, 0] = jax.nn.sigmoid(proj(2304, 2432))
    gate_ref[0, 1] = jax.nn.sigmoid(proj(2432, 2560))

    mq = proj(2560, 3072).astype(BF16)
    for h in range(MEM_HEADS):
        cols = slice(h * MEM_DH, (h + 1) * MEM_DH)
        p = _softmax(_dot_nt(mq[:, cols], mkv_ref[0, :, cols]) * (MEM_DH ** -0.5))
        om_ref[0, :, cols] = _dot(p.astype(BF16), mkv_ref[0, :, MEM_Q + h * MEM_DH:MEM_Q + (h + 1) * MEM_DH]).astype(BF16)


def _inproj_prompt(x, g1, w, wkvt, lng, lnb, ws, bst, mkv_bf):
    b, t, _ = x.shape
    tm = min(512, t)

    def row(width):
        return pl.BlockSpec((1, tm, width), lambda bi, i: (bi, i, 0))

    def col(height):
        return pl.BlockSpec((1, height, tm), lambda bi, i: (bi, 0, i))

    def heads(n):
        return pl.BlockSpec((1, n, tm, NSA_DH), lambda bi, i: (bi, 0, i, 0))

    return pl.pallas_call(
        functools.partial(_inproj_prompt_body, tm=tm), grid=(b, t // tm),
        in_specs=[row(D_MODEL), _full((1, D_MODEL)), _full(w.shape), _full(wkvt.shape),
                  _full((1, SGU_DIM)), _full((1, SGU_DIM)),
                  _full((SGU_GROUPS, CHUNK, CHUNK)), _full((CHUNK, SGU_GROUPS)),
                  pl.BlockSpec((1, MEM_TOKENS, 2 * MEM_Q), lambda bi, i: (bi, 0, 0))],
        out_specs=[row(SGU_DIM), heads(NSA_HEADS), col(512), col(256),
                   pl.BlockSpec((1, 4, tm // LANES, NSA_DH, LANES), lambda bi, i: (bi, 0, i, 0, 0)),
                   pl.BlockSpec((1, 2, tm, NSA_KVW), lambda bi, i: (bi, 0, i, 0)), row(256),
                   pl.BlockSpec((1, 2, tm, LANES), lambda bi, i: (bi, 0, i, 0)), row(MEM_Q)],
        out_shape=[jax.ShapeDtypeStruct((b, t, SGU_DIM), BF16), jax.ShapeDtypeStruct((b, NSA_HEADS, t, NSA_DH), BF16),
                   jax.ShapeDtypeStruct((b, 512, t), F32), jax.ShapeDtypeStruct((b, 256, t), F32),
                   jax.ShapeDtypeStruct((b, 4, t // LANES, NSA_DH, LANES), BF16),
                   jax.ShapeDtypeStruct((b, 2, t, NSA_KVW), BF16), jax.ShapeDtypeStruct((b, t, 256), BF16),
                   jax.ShapeDtypeStruct((b, 2, t, LANES), F32), jax.ShapeDtypeStruct((b, t, MEM_Q), BF16)],
        compiler_params=_params(2), name="inproj_prompt")(x, g1, w, wkvt, lng, lnb, ws, bst, mkv_bf)


def _inproj_sample_body(x_ref, g_ref, w_ref, lng_ref, lnb_ref, wsc_ref, bsc_ref,
                        oa_ref, z2_ref, q_ref, kv_ref, gate_ref, mq_ref):
    ab = _rms(x_ref[...], g_ref[...]).astype(BF16)

    def proj(lo, hi):
        return _dot(ab, w_ref[:, lo:hi])

    z1 = jax.nn.gelu(proj(0, 512))
    z2 = _layernorm(jax.nn.gelu(proj(512, 1024)), lng_ref[...], lnb_ref[...])
    z2_ref[...] = z2
    oa_ref[...] = (z1 * (wsc_ref[...] * z2 + bsc_ref[...])).astype(BF16)
    q_ref[...] = proj(1024, 2048).astype(BF16)
    kv_ref[...] = proj(2048, 2816)
    gate_ref[...] = jax.nn.sigmoid(proj(2816, 2944))
    mq_ref[...] = proj(2944, 3456)


def _inproj_sample(x, g1, w, lng, lnb, wsc, bsc):
    n = x.shape[0]
    shapes = [((n, SGU_DIM), BF16), ((n, SGU_DIM), F32), ((n, 2 * NSA_Q), BF16), ((n, 768), F32),
              ((n, LANES), F32), ((n, MEM_Q), F32)]
    return pl.pallas_call(
        _inproj_sample_body, grid=(1,),
        in_specs=[_full(x.shape), _full((1, D_MODEL)), _full(w.shape), _full((1, SGU_DIM)), _full((1, SGU_DIM)),
                  _full((1, SGU_DIM)), _full((1, SGU_DIM))],
        out_specs=[_full(s) for s, _ in shapes],
        out_shape=[jax.ShapeDtypeStruct(s, d) for s, d in shapes],
        compiler_params=_params(1), name="inproj_sample")(x, g1, w, lng, lnb, wsc, bsc)


def _compress_finish(ab, b1, w2):
    n = ab.shape[0]
    c = ab[:, :LANES] + pltpu.roll(ab[:, LANES:], n - 1, 0) + b1
    return _dot(jax.nn.gelu(c).astype(BF16), w2)


def _compress_prompt_body(r_ref, wk_ref, b1_ref, w2_ref, o_ref):
    out = _compress_finish(_dot(r_ref[0, 0], wk_ref[0]), b1_ref[0], w2_ref[0])
    for g in range(NSA_KV):
        o_ref[0, 0, g] = out[:, g * NSA_DH:(g + 1) * NSA_DH].astype(BF16)


def _compress_prompt(r, wk, b1, w2):
    _, b, n, width = r.shape
    return pl.pallas_call(
        _compress_prompt_body, grid=(2, b),
        in_specs=[pl.BlockSpec((1, 1, n, width), lambda k, bi: (k, bi, 0, 0)),
                  pl.BlockSpec((1, width, 2 * LANES), lambda k, bi: (k, 0, 0)),
                  pl.BlockSpec((1, 1, LANES), lambda k, bi: (k, 0, 0)),
                  pl.BlockSpec((1, LANES, LANES), lambda k, bi: (k, 0, 0))],
        out_specs=pl.BlockSpec((1, 1, NSA_KV, n, NSA_DH), lambda k, bi: (k, bi, 0, 0, 0)),
        out_shape=jax.ShapeDtypeStruct((2, b, NSA_KV, n, NSA_DH), BF16),
        compiler_params=_params(2), name="compress_prompt")(r, wk, b1, w2)


def _topk_rows(w, k, on_pick):
    n = w.shape[0]
    jidx = lax.broadcasted_iota(I32, w.shape, 0)
    for r in range(k):
        m = jnp.max(w, axis=0, keepdims=True)
        first = jnp.min(jnp.where(w == m, jidx, n), axis=0, keepdims=True)
        pick = jidx == first
        on_pick(r, first, pick)
        w = jnp.where(pick, -jnp.inf, w)


def _nsa_prompt_body(q_ref, g_ref, kc_ref, vc_ref, kst_ref, vs_ref, kwt_ref, vw_ref, mt_ref, es_ref, o_ref):
    ns = mt_ref.shape[0]
    hr = HEAD_ROWS
    n_units, rows = NSA_REP // hr, hr * Q_BLOCK
    blk = pl.program_id(2)
    s0 = blk * Q_BLOCK
    tq = s0 + lax.broadcasted_iota(I32, (Q_BLOCK, 1), 0)
    tcol = jnp.concatenate([tq] * hr, axis=0)
    q = [q_ref[0, u * hr:(u + 1) * hr].reshape(rows, NSA_DH) for u in range(n_units)]

    ncp = kc_ref.shape[-2]
    cend = lax.broadcasted_iota(I32, (1, ncp), 1) * CMP_STRIDE + (CMP_BLOCK - 1)
    cmask = cend <= tcol
    o_c, psum = [], None
    for u in range(n_units):
        probs = _masked_probs(_dot_nt(q[u], kc_ref[0, 0, 0]), cmask)
        o_c.append(_dot(probs.astype(BF16), vc_ref[0, 0, 0]))
        for r in range(hr):
            part = probs[r * Q_BLOCK:(r + 1) * Q_BLOCK]
            psum = part if psum is None else psum + part

    hi, lo = _split_hi_lo(psum)
    imp = _dot_nt(mt_ref[...], hi) + _dot_nt(mt_ref[...], lo)
    jidx = lax.broadcasted_iota(I32, (ns, Q_BLOCK), 0)
    cur = lax.shift_right_logical(s0 + lax.broadcasted_iota(I32, (ns, Q_BLOCK), 1), 6)
    valid = jidx <= cur
    forced = valid & ((jidx == 0) | (jidx > cur - N_LOCAL))
    w = jnp.where(forced, jnp.inf, jnp.where(valid, imp, -jnp.inf))
    picked = [jnp.zeros((ns, Q_BLOCK), F32)]

    def on_pick(r, first, pick):
        picked[0] = jnp.where(pick, 1.0, picked[0])

    _topk_rows(w, min(SEL_TOPK, ns), on_pick)
    bias = jnp.where(valid, jnp.where(picked[0] > 0.0, 0.0, MASKED), MASKED)
    bias_t = bias.T.astype(BF16)

    bias_rows = jnp.concatenate([bias_t] * hr, axis=0)
    q_aug = [jnp.concatenate([bias_rows, q[u]], axis=1) for u in range(n_units)]
    tiles_per = KV_TILE // LANES

    def keys_aug(kt):
        kt4 = jnp.concatenate([kst_ref[0, 0, kt * tiles_per + j] for j in range(tiles_per)], axis=1)
        return jnp.concatenate([es_ref[kt], kt4], axis=0)

    second = pl.program_id(1) == 1

    def vals_aug(v_both):
        own = (lax.broadcasted_iota(I32, v_both.shape, 1) >= NSA_DH) == second
        return jnp.where(own, v_both, jnp.ones((), BF16))

    def normalised(acc):
        acc = jnp.where(second, pltpu.roll(acc, NSA_DH, 1), acc)
        return acc[:, :NSA_DH] * (1.0 / jnp.maximum(acc[:, NSA_DH:NSA_DH + 1], 1e-30))

    def update(carry, qa, k_aug, v, mask):
        m, acc = carry
        sc = _dot(qa, k_aug)
        if mask is not None:
            sc = jnp.where(mask, sc, MASKED)
        m_new = jnp.maximum(m, jnp.max(sc, axis=-1, keepdims=True))
        p = jnp.exp(sc - m_new)
        return m_new, jnp.exp(m - m_new) * acc + _dot(p.astype(BF16), v)

    def step(kt, carry):
        k_aug = keys_aug(kt)
        v = vals_aug(vs_ref[0, 0, pl.ds(pl.multiple_of(kt * KV_TILE, KV_TILE), KV_TILE), :])
        return tuple(update(carry[u], q_aug[u], k_aug, v, None) for u in range(n_units))

    one = (jnp.full((rows, 1), -jnp.inf, F32), jnp.zeros((rows, LANES), F32))
    last = (s0 + Q_BLOCK - 1) // KV_TILE
    carry = lax.fori_loop(0, last // 2, lambda p, c: step(2 * p + 1, step(2 * p, c)), (one,) * n_units)
    carry = lax.cond(last % 2 == 1, lambda c: step(last - 1, c), lambda c: c, carry)
    k0 = pl.multiple_of(last * KV_TILE, KV_TILE)
    causal = k0 + lax.broadcasted_iota(I32, (1, KV_TILE), 1) <= tcol
    k_aug, v = keys_aug(last), vals_aug(vs_ref[0, 0, pl.ds(k0, KV_TILE), :])
    o_s = [normalised(update(carry[u], q_aug[u], k_aug, v, causal)[1]) for u in range(n_units)]

    span = WINDOW + Q_BLOCK
    w0 = jnp.maximum(blk - WINDOW // Q_BLOCK, 0)
    start = pl.multiple_of(w0 * Q_BLOCK, Q_BLOCK)
    kw = jnp.concatenate([kwt_ref[0, 0, w0 + j] for j in range(span // LANES)], axis=1)
    vw = vals_aug(vw_ref[0, 0, pl.ds(start, span), :])
    wpos = start + lax.broadcasted_iota(I32, (1, span), 1)
    wmask = (wpos <= tcol) & (wpos > tcol - WINDOW)

    gates = g_ref[0, 0]
    outs = []
    for u in range(n_units):
        o_w = normalised(_dot(_masked_exp(_dot(q[u], kw), wmask).astype(BF16), vw))
        for r in range(hr):
            h, sl = u * hr + r, slice(r * Q_BLOCK, (r + 1) * Q_BLOCK)
            outs.append(gates[:, h:h + 1] * o_c[u][sl] + gates[:, 4 + h:5 + h] * o_s[u][sl] + gates[:, 8 + h:9 + h] * o_w[sl])
    o_ref[0] = jnp.concatenate(outs, axis=1).astype(BF16)


def _nsa_prompt(qh, gates, kcv, kt, v, mt, es):
    b, _, t, _ = qh.shape
    ncp = kcv.shape[-2]

    def keys(c):
        return pl.BlockSpec((1, 1, t // LANES, NSA_DH, LANES), lambda bi, g, i: (bi, c + g, 0, 0, 0))

    def vals(kind):
        return pl.BlockSpec((1, 1, t, NSA_KVW), lambda bi, g, i: (bi, kind, 0, 0))

    def cmp(k):
        return pl.BlockSpec((1, 1, 1, ncp, NSA_DH), lambda bi, g, i: (k, bi, g, 0, 0))

    return pl.pallas_call(
        _nsa_prompt_body, grid=(b, NSA_KV, t // Q_BLOCK),
        in_specs=[pl.BlockSpec((1, NSA_REP, Q_BLOCK, NSA_DH), lambda bi, g, i: (bi, g, i, 0)),
                  pl.BlockSpec((1, 1, Q_BLOCK, LANES), lambda bi, g, i: (bi, g, i, 0)),
                  cmp(0), cmp(1), keys(0), vals(0), keys(2), vals(1), _full(mt.shape), _full(es.shape)],
        out_specs=pl.BlockSpec((1, Q_BLOCK, NSA_REP * NSA_DH), lambda bi, g, i: (bi, i, g)),
        out_shape=jax.ShapeDtypeStruct((b, t, NSA_Q), BF16),
        compiler_params=_params(3), name="nsa_prompt")(qh, gates, kcv, kcv, kt, v, kt, v, mt, es)


def _sample_cmp_body(pt_ref, *refs, n_chunks):
    pc = PAGES_PER_STEP
    pages = refs[:pc]
    q_ref, wk_ref, b1_ref, w2_ref, m_ref, perm_ref, oc_ref, imp_ref, rk_ref, ab_ref = refs[pc:]
    c = pl.program_id(1)
    per_page = 128 // CMP_STRIDE

    def by_offset(k, kind):
        x = pages[k][0, kind * LANES:(kind + 1) * LANES, :].astype(BF16)
        return _dot(x, perm_ref[...]).T

    for kind in range(2):
        for k in range(0, pc, 2):
            pair = [by_offset(k + d, kind) for d in range(2)]
            for j in range(CMP_STRIDE):
                rows = jnp.concatenate([p[j * per_page:(j + 1) * per_page] for p in pair], axis=0)
                rk_ref[kind, k * per_page:(k + 2) * per_page, j * LANES:(j + 1) * LANES] = rows.astype(BF16)
    row0 = pl.multiple_of(c * (pc * per_page), pc * per_page)
    for kind in range(2):
        ab_ref[kind, pl.ds(row0, pc * per_page), :] = _dot(rk_ref[kind], wk_ref[kind])

    @pl.when(c == n_chunks - 1)
    def _():
        kc = _compress_finish(ab_ref[0], b1_ref[0], w2_ref[0]).astype(BF16)
        vc = _compress_finish(ab_ref[1], b1_ref[1], w2_ref[1]).astype(BF16)
        ncp = kc.shape[0]
        row = lax.broadcasted_iota(I32, (NSA_HEADS, LANES), 0)
        lane = lax.broadcasted_iota(I32, (NSA_HEADS, LANES), 1)
        qbd = jnp.where((lane >= NSA_DH) == (row >= NSA_REP), q_ref[0], jnp.zeros((), BF16))
        s = _dot_nt(qbd, kc)
        n = lax.broadcasted_iota(I32, (1, ncp), 1)
        probs = _masked_probs(s, n * CMP_STRIDE + (CMP_BLOCK - 1) <= ncp * CMP_STRIDE)
        oc_ref[0] = _dot(probs.astype(BF16), vc)
        hi, lo = _split_hi_lo(probs)
        imp8 = _dot(hi, m_ref[...]) + _dot(lo, m_ref[...])
        r8 = lax.broadcasted_iota(I32, imp8.shape, 0)
        imp_ref[0, 0:1, :] = jnp.sum(jnp.where(r8 < NSA_REP, imp8, 0.0), axis=0, keepdims=True)
        imp_ref[0, 1:2, :] = jnp.sum(jnp.where(r8 >= NSA_REP, imp8, 0.0), axis=0, keepdims=True)


def _sample_cmp(pt_flat, cache_t, qrep, wk, b1, w2, m, n_pages):
    nb = qrep.shape[0]
    pc = PAGES_PER_STEP
    n_chunks = n_pages // pc
    ncp = n_pages * 128 // CMP_STRIDE
    ns = m.shape[1]

    def page(k):
        return pl.BlockSpec((1, 2 * LANES, 128), lambda b, c, pt: (pt[b * n_pages + c * pc + k], 0, 0))

    grid_spec = pltpu.PrefetchScalarGridSpec(
        num_scalar_prefetch=1, grid=(nb, n_chunks),
        in_specs=[page(k) for k in range(pc)] + [
            pl.BlockSpec((1, NSA_HEADS, LANES), lambda b, c, pt: (b, 0, 0)),
            pl.BlockSpec(wk.shape, lambda b, c, pt: (0, 0, 0)), pl.BlockSpec(b1.shape, lambda b, c, pt: (0, 0, 0)),
            pl.BlockSpec(w2.shape, lambda b, c, pt: (0, 0, 0)), pl.BlockSpec(m.shape, lambda b, c, pt: (0, 0)),
            pl.BlockSpec((128, 128), lambda b, c, pt: (0, 0))],
        out_specs=[pl.BlockSpec((1, NSA_HEADS, LANES), lambda b, c, pt: (b, 0, 0)),
                   pl.BlockSpec((1, NSA_KV, ns), lambda b, c, pt: (b, 0, 0))],
        scratch_shapes=[pltpu.VMEM((2, pc * 8, CMP_STRIDE * LANES), BF16), pltpu.VMEM((2, ncp, 2 * LANES), F32)])
    pos = np.arange(128)
    perm = np.zeros((128, 128), np.float32)
    perm[pos, (pos % CMP_STRIDE) * (128 // CMP_STRIDE) + pos // CMP_STRIDE] = 1.0
    return pl.pallas_call(
        functools.partial(_sample_cmp_body, n_chunks=n_chunks), grid_spec=grid_spec,
        out_shape=[jax.ShapeDtypeStruct((nb, NSA_HEADS, LANES), F32), jax.ShapeDtypeStruct((nb, NSA_KV, ns), F32)],
        compiler_params=_params(2), name="sample_compress")(
            pt_flat, *([cache_t] * pc), qrep, wk, b1, w2, m, jnp.asarray(perm, BF16))


def _sample_topk_body(imp_ref, idx_ref, *, k):
    ns = imp_ref.shape[1]
    idx_ref[...] = jnp.zeros(idx_ref.shape, I32)
    for g in range(NSA_KV):
        jidx = lax.broadcasted_iota(I32, imp_ref.shape[1:], 0)
        w = jnp.where((jidx == 0) | (jidx > ns - N_LOCAL), jnp.inf, imp_ref[g])

        def on_pick(r, first, pick, g=g):
            idx_ref[g, r:r + 1, :] = first

        _topk_rows(w, k, on_pick)


def _sample_topk(imp_t, k):
    _, ns, nb = imp_t.shape
    return pl.pallas_call(
        functools.partial(_sample_topk_body, k=k), grid=(1,),
        in_specs=[_full(imp_t.shape)], out_specs=_full((NSA_KV, SEL_TOPK, nb)),
        out_shape=jax.ShapeDtypeStruct((NSA_KV, SEL_TOPK, nb), I32),
        compiler_params=_params(1), name="sample_topk")(imp_t)


def _attend_with_self(qg, keys_t, vals_t, k_self, v_self, mask):
    s = _dot(qg, keys_t)
    if mask is not None:
        s = jnp.where(mask, s, -jnp.inf)
    s_self = jnp.sum(qg.astype(F32) * k_self.astype(BF16).astype(F32), axis=-1, keepdims=True)
    m = jnp.maximum(jnp.max(s, axis=-1, keepdims=True), s_self)
    p = jnp.exp(s - m)
    p_self = jnp.exp(s_self - m)
    den = jnp.sum(p, axis=-1, keepdims=True) + p_self
    num = _dot_nt(p.astype(BF16), vals_t) + p_self.astype(BF16).astype(F32) * v_self.astype(BF16).astype(F32)
    return num / den


def _sample_attn_body(idx_ref, pt_ref, *refs, n_sel):
    n_blk = NSA_KV * n_sel
    kblocks, vblocks = refs[:n_blk], refs[n_blk:2 * n_blk]
    (win_ref, q_ref, kvn_ref, gate_ref, oc_ref, mq_ref, mem_ref,
     ob_ref, om_ref, wn_ref, ks_ref, vs_ref) = refs[2 * n_blk:]
    b = pl.program_id(0)
    row = lax.broadcasted_iota(I32, (NSA_HEADS, NSA_DH), 0)
    q8 = q_ref[0][:, :NSA_DH]
    qg = [jnp.where((row >= NSA_REP) == (g == 1), q8, jnp.zeros((), BF16)) for g in range(NSA_KV)]
    kvn = kvn_ref[0]

    def new_row(kind, g):
        lo = kind * NSA_KVW + g * NSA_DH
        return kvn[:, lo:lo + NSA_DH]

    lane = lax.broadcasted_iota(I32, (1, n_sel * LANES), 1)
    upper = (lane & SEL_BLOCK) != 0
    slot = lax.shift_right_logical(lane, 7)
    o_sel = []
    for g in range(NSA_KV):
        want_upper = jnp.zeros((1, n_sel * LANES), I32)
        for k in range(n_sel):
            ks_ref[g, :, k * LANES:(k + 1) * LANES] = kblocks[g * n_sel + k][0].astype(BF16)
            vs_ref[g, :, k * LANES:(k + 1) * LANES] = vblocks[g * n_sel + k][0].astype(BF16)
            j = idx_ref[b * (NSA_KV * SEL_TOPK) + g * SEL_TOPK + k]
            want_upper = jnp.where(slot == k, j & 1, want_upper)
        o_sel.append(_attend_with_self(qg[g], ks_ref[g], vs_ref[g], new_row(2, g), new_row(3, g),
                                       upper == (want_upper != 0)))
    o_s = jnp.where(row < NSA_REP, o_sel[0], o_sel[1])

    win = win_ref[0]
    wbuf = win.shape[1]
    wcol = lax.broadcasted_iota(I32, (1, wbuf), 1)
    o_win = []
    for g in range(NSA_KV):
        kt = win[g * NSA_DH:(g + 1) * NSA_DH].astype(BF16)
        vt = win[NSA_KVW + g * NSA_DH:NSA_KVW + (g + 1) * NSA_DH].astype(BF16)
        o_win.append(_attend_with_self(qg[g], kt, vt, new_row(4, g), new_row(5, g), wcol >= 1))
    o_w = jnp.where(row < NSA_REP, o_win[0], o_win[1])
    rows_w = 2 * NSA_KVW
    eye = lax.broadcasted_iota(I32, (rows_w, rows_w), 0) == lax.broadcasted_iota(I32, (rows_w, rows_w), 1)
    new_col = jnp.sum(jnp.where(eye, kvn[:, 4 * NSA_KVW:], 0.0), axis=-1, keepdims=True)
    wn_ref[0] = jnp.where(lax.broadcasted_iota(I32, win.shape, 1) == wbuf - 1, new_col, pltpu.roll(win, wbuf - 1, 1))

    gate = gate_ref[0]
    grow = lax.broadcasted_iota(I32, (NSA_HEADS, LANES), 0)
    glane = lax.broadcasted_iota(I32, (NSA_HEADS, LANES), 1)
    oc = jnp.where((glane >= NSA_DH) == (grow >= NSA_REP), oc_ref[0], 0.0)
    o = jnp.zeros((NSA_HEADS, NSA_DH), F32)
    for j, branch in enumerate((oc[:, :NSA_DH] + oc[:, NSA_DH:], o_s, o_w)):
        gj = jnp.sum(jnp.where(glane == j * NSA_HEADS + grow, gate, 0.0), axis=-1, keepdims=True)
        o = o + gj * branch
    ob_ref[0] = o

    mq = mq_ref[0]
    per_tok = 2 * MEM_HEADS
    for h in range(MEM_HEADS):
        qh = jnp.broadcast_to(mq[:, h * MEM_DH:(h + 1) * MEM_DH], (NSA_HEADS, MEM_DH)).astype(BF16)
        km = mem_ref[0, pl.ds(h, MEM_TOKENS, stride=per_tok), :].astype(BF16)
        vm = mem_ref[0, pl.ds(MEM_HEADS + h, MEM_TOKENS, stride=per_tok), :].astype(BF16)
        p = _softmax(_dot_nt(qh, km) * (MEM_DH ** -0.5))
        om_ref[0, :, h * MEM_DH:(h + 1) * MEM_DH] = _dot(p.astype(BF16), vm)[0:1]


def _sample_attn(idx_flat, pt_flat, cache_t, win_t, qrep, kvn, gate, oc, mq, memkv, n_pages, n_sel):
    nb, rows_w, wbuf = win_t.shape

    def sel_block(kind, g, k):
        def index(b, idx, pt):
            j = idx[b * (NSA_KV * SEL_TOPK) + g * SEL_TOPK + k]
            return (pt[b * n_pages + lax.shift_right_logical(j, 1)], kind * NSA_KV + g, 0)
        return pl.BlockSpec((1, NSA_DH, 128), index)

    def per_seq(shape):
        return pl.BlockSpec((1,) + shape, lambda b, idx, pt: (b,) + (0,) * len(shape))

    grid_spec = pltpu.PrefetchScalarGridSpec(
        num_scalar_prefetch=2, grid=(nb,),
        in_specs=[sel_block(kind, g, k) for kind in (2, 3) for g in range(NSA_KV) for k in range(n_sel)] + [
            per_seq((rows_w, wbuf)), per_seq((NSA_HEADS, LANES)), per_seq((1, 768)), per_seq((1, LANES)),
            per_seq((NSA_HEADS, LANES)), per_seq((1, MEM_Q)), per_seq(memkv.shape[1:])],
        out_specs=[per_seq((NSA_HEADS, NSA_DH)), per_seq((1, MEM_Q)), per_seq((rows_w, wbuf))],
        scratch_shapes=[pltpu.VMEM((NSA_KV, NSA_DH, n_sel * LANES), BF16)] * 2)
    return pl.pallas_call(
        functools.partial(_sample_attn_body, n_sel=n_sel), grid_spec=grid_spec,
        out_shape=[jax.ShapeDtypeStruct((nb, NSA_HEADS, NSA_DH), F32), jax.ShapeDtypeStruct((nb, 1, MEM_Q), F32),
                   jax.ShapeDtypeStruct((nb, rows_w, wbuf), F32)],
        compiler_params=_params(1), name="sample_attn")(
            idx_flat, pt_flat, *([cache_t] * (2 * NSA_KV * n_sel)), win_t, qrep, kvn, gate, oc, mq, memkv)


def _merge_body(x_ref, oa_ref, ob_ref, om_ref, g1_ref, wmg_ref, wb_ref, wo_ref, g2_ref, rwh_ref, rwl_ref, rb_ref,
                h1_ref, hn_ref, gt_ref):
    x = x_ref[...]
    ab = _rms(x, g1_ref[...]).astype(BF16)
    s = None
    for k, o_ref in enumerate((oa_ref, ob_ref, om_ref)):
        gk = jax.nn.sigmoid(_dot(ab, wmg_ref[:, k * D_MODEL:(k + 1) * D_MODEL]))
        term = gk * _dot(o_ref[...], wb_ref[k])
        s = term if s is None else s + term
    h1 = x + _dot(s.astype(BF16), wo_ref[...])
    h1_ref[...] = h1
    hn = _rms(h1, g2_ref[...])
    hi, lo = _split_hi_lo(hn)
    hn_ref[...] = hi

    logit = _dot_nt(rwh_ref[...], hi) + _dot_nt(rwh_ref[...], lo) + _dot_nt(rwl_ref[...], hi) + rb_ref[...]
    picked = [jnp.zeros(logit.shape, F32)]

    def on_pick(r, first, pick):
        picked[0] = jnp.where(pick, 1.0, picked[0])

    _topk_rows(logit, TOP_K, on_pick)
    top = jnp.max(logit, axis=0, keepdims=True)
    ex = jnp.where(picked[0] > 0.0, jnp.exp(logit - top), 0.0)
    gt_ref[...] = ex / jnp.sum(ex, axis=0, keepdims=True)


def _merge(x, oa, ob, om, g1, wmg, wb, wo, g2, rwh, rwl, rb, tm):
    n = x.shape[0]

    def row(width):
        return pl.BlockSpec((tm, width), lambda i: (i, 0))

    return pl.pallas_call(
        _merge_body, grid=(n // tm,),
        in_specs=[row(D_MODEL), row(512), row(512), row(512), _full((1, D_MODEL)), _full(wmg.shape), _full(wb.shape),
                  _full(wo.shape), _full((1, D_MODEL)), _full(rwh.shape), _full(rwl.shape), _full(rb.shape)],
        out_specs=[row(D_MODEL), row(D_MODEL), pl.BlockSpec((N_EXPERTS, tm), lambda i: (0, i))],
        out_shape=[jax.ShapeDtypeStruct((n, D_MODEL), F32), jax.ShapeDtypeStruct((n, D_MODEL), BF16),
                   jax.ShapeDtypeStruct((N_EXPERTS, n), F32)],
        compiler_params=_params(1), name="merge")(x, oa, ob, om, g1, wmg, wb, wo, g2, rwh, rwl, rb)


def _moe_body(cnt_ref, hn_ref, h1_ref, gt_ref, u_ref, wgu_ref, bgu_ref, wd_ref, bd_ref, fg_ref, y_ref,
              rank_ref, *, sub, cap):
    i, e = pl.program_id(0), pl.program_id(1)
    n_sub = hn_ref.shape[0] // sub

    @pl.when(e == 0)
    def _():
        for s in range(n_sub):
            cols = slice(s * sub, (s + 1) * sub)
            sel = jnp.where(gt_ref[:, cols] > 0.0, 1.0, 0.0).astype(BF16)
            rank_ref[:, cols] = _dot(sel, u_ref[...])
        y_ref[...] = h1_ref[...]

    slot = lax.broadcasted_iota(I32, (cap, sub), 0).astype(F32)
    for s in range(n_sub):
        rows = slice(s * sub, (s + 1) * sub)
        g_row = gt_ref[pl.ds(e, 1), rows]
        r_row = rank_ref[pl.ds(e, 1), rows]

        def chunk(ch, carry, rows=rows, g_row=g_row, r_row=r_row):
            want = slot + (ch * cap).astype(F32)
            hit = (g_row > 0.0) & (r_row == want)
            onehot = jnp.where(hit, 1.0, 0.0).astype(BF16)
            xe = _dot(onehot, hn_ref[rows, :]).astype(BF16)
            gu = _dot(xe, wgu_ref[0]) + bgu_ref[0]
            gate = jnp.minimum(gu[:, :D_EXPERT], SWIGLU_LIMIT)
            up = jnp.clip(gu[:, D_EXPERT:], -SWIGLU_LIMIT, SWIGLU_LIMIT)
            act = (up + 1.0) * gate * jax.nn.sigmoid(SWIGLU_ALPHA * gate)
            ye = _dot(act.astype(BF16), wd_ref[0]) + bd_ref[0]
            ye = ye * jnp.sum(jnp.where(hit, g_row, 0.0), axis=-1, keepdims=True)
            y_ref[rows, :] += _dot_tn(onehot, ye.astype(BF16))
            return carry

        n_chunks = (cnt_ref[(i * n_sub + s) * N_EXPERTS + e] + cap - 1) // cap
        lax.fori_loop(0, n_chunks, chunk, 0)

    @pl.when(e == N_EXPERTS - 1)
    def _():
        y_ref[...] = _rms(y_ref[...], fg_ref[...])


def _moe(counts, hn, h1, gate_t, upper, wgu, bgu, wd, bd, fg, tile, sub, cap):
    n = hn.shape[0]
    once = pl.Buffered(1)
    grid_spec = pltpu.PrefetchScalarGridSpec(
        num_scalar_prefetch=1, grid=(n // tile, N_EXPERTS),
        in_specs=[pl.BlockSpec((tile, D_MODEL), lambda i, e, c: (i, 0)),
                  pl.BlockSpec((tile, D_MODEL), lambda i, e, c: (i, 0), pipeline_mode=once),
                  pl.BlockSpec((N_EXPERTS, tile), lambda i, e, c: (0, i)),
                  pl.BlockSpec((sub, sub), lambda i, e, c: (0, 0), pipeline_mode=once),
                  pl.BlockSpec((1, D_MODEL, 2 * D_EXPERT), lambda i, e, c: (e, 0, 0)),
                  pl.BlockSpec((1, 1, 2 * D_EXPERT), lambda i, e, c: (e, 0, 0)),
                  pl.BlockSpec((1, D_EXPERT, D_MODEL), lambda i, e, c: (e, 0, 0)),
                  pl.BlockSpec((1, 1, D_MODEL), lambda i, e, c: (e, 0, 0)),
                  pl.BlockSpec((1, D_MODEL), lambda i, e, c: (0, 0))],
        out_specs=pl.BlockSpec((tile, D_MODEL), lambda i, e, c: (i, 0)),
        scratch_shapes=[pltpu.VMEM((N_EXPERTS, tile), F32)])
    return pl.pallas_call(
        functools.partial(_moe_body, sub=sub, cap=cap), grid_spec=grid_spec,
        out_shape=jax.ShapeDtypeStruct((n, D_MODEL), F32),
        compiler_params=_params(2), name="moe")(counts, hn, h1, gate_t, upper, wgu, bgu, wd, bd, fg)


def _gate_columns(order):
    idx = np.zeros((LANES,), np.int32)
    ok = np.zeros((LANES,), bool)
    for c, (h, j) in enumerate(order):
        idx[c], ok[c] = C_NG + h * 3 + j, True
    return idx, ok


def _prep_in_weights(w_in):
    def sect(lo, hi):
        return w_in[:, lo:hi]

    def gates(order):
        idx, ok = _gate_columns(order)
        return jnp.where(jnp.asarray(ok)[None, :], w_in[:, idx], 0.0)

    w_q = sect(C_Q, C_KV) * (NSA_DH ** -0.5)
    g_prompt = [gates([(NSA_REP * g + r, j) for j in range(3) for r in range(NSA_REP)]) for g in range(NSA_KV)]
    w_prompt = jnp.concatenate([sect(C_U, C_Q), w_q, sect(C_KV, C_NG)] + g_prompt + [sect(C_MQ, C_MG)], axis=1)
    q_rep = jnp.concatenate([w_q.reshape(D_MODEL, NSA_HEADS, 1, NSA_DH)] * 2, axis=2).reshape(D_MODEL, 2 * NSA_Q)
    g_sample = gates([(h, j) for j in range(3) for h in range(NSA_HEADS)])
    w_sample = jnp.concatenate([sect(C_U, C_Q), q_rep, sect(C_KV, C_NG), g_sample, sect(C_MQ, C_MG)], axis=1)
    w_kv_t = w_in.T[C_KV:C_NG]
    return w_prompt.astype(BF16), w_sample.astype(BF16), sect(C_MG, w_in.shape[1]).astype(BF16), w_kv_t.astype(BF16)


def _prep_compress(w1, b1, w2):
    eye = jnp.eye(NSA_KV, dtype=F32)
    w = w1.reshape(2, 2, CMP_STRIDE, NSA_DH, NSA_DH)
    wk = jnp.einsum('khjde,xg->kjxdhge', w, eye).reshape(2, CMP_STRIDE * LANES, 2 * LANES)
    b = jnp.concatenate([b1, b1], axis=-1).reshape(2, 1, LANES)
    w2d = jnp.einsum('kde,xg->kxdge', w2, eye).reshape(2, LANES, LANES)
    return wk.astype(BF16), b, w2d.astype(BF16)


def _cmp_to_sel_t(ncp, ns):
    cs = np.arange(ncp) * CMP_STRIDE
    ss = np.arange(ns) * SEL_BLOCK
    m = (cs[None, :] <= ss[:, None] + SEL_BLOCK - 1) & (cs[None, :] + CMP_BLOCK - 1 >= ss[:, None])
    m[:, ncp - 1] = False
    return m.astype(np.float32)


def _block_onehot(t):
    e = (np.arange(t // SEL_BLOCK)[:, None] == (np.arange(t) // SEL_BLOCK)[None, :]).astype(np.float32)
    return e.reshape(t // SEL_BLOCK, t // KV_TILE, KV_TILE).transpose(1, 0, 2)


def _moe_group(h1_and_gates, weights, tile, sub, cap):
    h1, hn, gate_t = h1_and_gates
    n = h1.shape[0]
    counts = jnp.sum((gate_t > 0.0).reshape(N_EXPERTS, n // sub, sub), axis=-1, dtype=I32).T.reshape(-1)
    upper = jnp.asarray(np.triu(np.ones((sub, sub), np.float32), 1), BF16)
    return _moe(counts, hn, h1, gate_t, upper, *weights, tile, sub, cap)


def kernel(x_prompt, x_sample, mem_prompt, cache_nsa_kv, cache_nsa_win, cache_mem_kv, page_table,
           norm1_g, w_in, sgu_ln_g, sgu_ln_b, sgu_w, sgu_b, cmp_w1, cmp_b1, cmp_w2, w_mem_kv,
           w_branch, w_out, norm2_g, router_w, router_b, exp_w_gate_up, exp_b_gate_up,
           exp_w_down, exp_b_down, final_norm_g):
    assert norm1_g.shape[0] == 1, "one layer"
    bp, t, _ = x_prompt.shape
    nb = x_sample.shape[0]
    n_pool, page = cache_nsa_kv.shape[1:3]
    n_pages = page_table.shape[1]
    past = n_pages * page
    wbuf = cache_nsa_win.shape[2]
    assert page == 128 and wbuf == WINDOW and n_pages % PAGES_PER_STEP == 0 and t % KV_TILE == 0 and t >= WINDOW + Q_BLOCK

    g1, g2, fg = norm1_g[0][None], norm2_g[0][None], final_norm_g[None]
    lng, lnb = sgu_ln_g[0][None], sgu_ln_b[0][None]
    w_prompt, w_sample, w_mg, w_kv_t = _prep_in_weights(w_in[0])
    wk, b1c, w2c = _prep_compress(cmp_w1[0], cmp_b1[0], cmp_w2[0])
    wb, wo = w_branch[0].astype(BF16), w_out[0].astype(BF16)
    rwh, rwl = _split_hi_lo(router_w[0].T)
    rb = router_b[0][:, None]
    moe_w = (exp_w_gate_up[0].astype(BF16), exp_b_gate_up[0][:, None, :], exp_w_down[0].astype(BF16),
             exp_b_down[0][:, None, :], fg)
    merge_w = (g1, w_mg, wb, wo, g2, rwh, rwl, rb)

    mkv, mkv_bf = _mem_kv(mem_prompt.reshape(bp * MEM_TOKENS, D_MODEL), w_mem_kv[0].astype(BF16))
    oa, qh, kv4t, kvwt, kt, v, kcv_rows, gates, om = _inproj_prompt(
        x_prompt, g1, w_prompt, w_kv_t, lng, lnb, sgu_w[0], sgu_b[0].T, mkv_bf.reshape(bp, MEM_TOKENS, 2 * MEM_Q))
    ncp, ns = t // CMP_STRIDE, t // SEL_BLOCK
    r = jnp.stack([kcv_rows[:, :, k * LANES:(k + 1) * LANES].reshape(bp, ncp, CMP_STRIDE * LANES) for k in range(2)])
    kcv = _compress_prompt(r, wk, b1c, w2c)
    mt = jnp.asarray(_cmp_to_sel_t(ncp, ns), BF16)
    ob = _nsa_prompt(qh, gates, kcv, kt, v, mt, jnp.asarray(_block_onehot(t), BF16))
    n = bp * t
    x2 = x_prompt.reshape(n, D_MODEL)
    y_prompt = _moe_group(_merge(x2, oa.reshape(n, 512), ob.reshape(n, 512), om.reshape(n, 512), *merge_w, tm=min(512, n)),
                          moe_w, min(MOE_BLOCK, n), min(MOE_TILE, n), MOE_CAP)

    xs = x_sample.reshape(nb, D_MODEL)
    wsc = jnp.repeat(sgu_w[0][:, 0, 0], SGU_GROUP_DIM)[None]
    bsc = jnp.repeat(sgu_b[0][:, 0], SGU_GROUP_DIM)[None]
    oa_s, z2_s, q_s, kv_s, gate_s, mq_s = _inproj_sample(xs, g1, w_sample, lng, lnb, wsc, bsc)
    qrep = q_s.reshape(nb, NSA_HEADS, LANES)
    pt_flat = page_table.reshape(-1)
    ncp_s, ns_s = past // CMP_STRIDE, past // SEL_BLOCK
    m_s = jnp.asarray(_cmp_to_sel_t(ncp_s, ns_s).T, BF16)
    cache_t = jnp.transpose(cache_nsa_kv[0], (0, 2, 3, 4, 1)).reshape(n_pool, 4 * NSA_KVW, page)
    win_t = jnp.transpose(cache_nsa_win[0], (0, 2, 3, 4, 1)).reshape(nb, 2 * NSA_KVW, wbuf)
    oc_s, imp_s = _sample_cmp(pt_flat, cache_t, qrep, wk, b1c, w2c, m_s, n_pages)
    n_sel = min(SEL_TOPK, ns_s + 1) - 1
    idx = _sample_topk(imp_s.transpose(1, 2, 0), n_sel)
    idx_flat = idx.transpose(2, 0, 1).reshape(-1)
    ob_s, om_s, win_new = _sample_attn(
        idx_flat, pt_flat, cache_t, win_t, qrep, kv_s[:, None, :], gate_s[:, None, :], oc_s, mq_s[:, None, :],
        cache_mem_kv[0].reshape(nb, MEM_TOKENS * 2 * MEM_HEADS, MEM_DH), n_pages, n_sel)
    y_sample = _moe_group(_merge(xs, oa_s, ob_s.reshape(nb, NSA_Q).astype(BF16), om_s.reshape(nb, MEM_Q).astype(BF16),
                                 *merge_w, tm=nb), moe_w, nb, nb, 32)

    def positions_last(x_t, n_kinds):
        lead, _, npos = x_t.shape
        return jnp.transpose(x_t.reshape(lead, n_kinds, NSA_KV, NSA_DH, npos), (0, 4, 1, 2, 3))[None]

    return (y_prompt.reshape(bp, t, D_MODEL), y_sample.reshape(nb, 1, D_MODEL),
            positions_last(kv4t, 4), positions_last(kvwt[:, :, t - WINDOW:], 2),
            mkv.reshape(1, bp, MEM_TOKENS, 2, MEM_HEADS, MEM_DH), kv_s[:, :512].reshape(1, nb, 1, 4, NSA_KV, NSA_DH),
            positions_last(win_new, 2), z2_s.reshape(1, nb, 1, SGU_DIM))
```

```python
import functools

import numpy as np
import jax
import jax.numpy as jnp
from jax import lax
from jax.experimental import pallas as pl
from jax.experimental.pallas import tpu as pltpu

F32, BF16, I32 = jnp.float32, jnp.bfloat16, jnp.int32

D_MODEL = 1024
SGU_GROUPS, SGU_GROUP_DIM, SGU_DIM, CHUNK = 4, 128, 512, 128
NSA_HEADS, NSA_KV, NSA_REP, NSA_DH = 8, 2, 4, 64
CMP_BLOCK, CMP_STRIDE, SEL_BLOCK, SEL_TOPK, N_LOCAL, WINDOW, Q_BLOCK = 32, 16, 64, 16, 2, 512, 128
MEM_TOKENS, MEM_HEADS, MEM_DH = 256, 4, 128
N_EXPERTS, TOP_K, D_EXPERT = 32, 4, 1024
SWIGLU_LIMIT, SWIGLU_ALPHA, EPS = 7.0, 1.702, 1e-5
NSA_Q, NSA_KVW, MEM_Q = NSA_HEADS * NSA_DH, NSA_KV * NSA_DH, MEM_HEADS * MEM_DH
C_U, C_V, C_Q, C_KV, C_NG, C_MQ, C_MG = 0, 512, 1024, 1536, 2304, 2328, 2840

LANES = 128
VMEM_LIMIT = 56 * 1024 * 1024
MASKED = -1e9
KV_TILE = 512
HEAD_ROWS = 4
PAGES_PER_STEP = 16
MOE_TILE, MOE_CAP = 1024, 160
MOE_BLOCK = 2048


def _dot(a, b):
    return jnp.dot(a, b, preferred_element_type=F32)


def _dot_nt(a, b):
    return lax.dot_general(a, b, (((1,), (1,)), ((), ())), preferred_element_type=F32)


def _dot_tn(a, b):
    return lax.dot_general(a, b, (((0,), (0,)), ((), ())), preferred_element_type=F32)


def _rms(x, g):
    return x * lax.rsqrt(jnp.mean(x * x, axis=-1, keepdims=True) + EPS) * g


def _layernorm(x, g, b):
    mu = jnp.mean(x, axis=-1, keepdims=True)
    xc = x - mu
    return xc * lax.rsqrt(jnp.mean(xc * xc, axis=-1, keepdims=True) + EPS) * g + b


def _masked_exp(s, mask):
    s = jnp.where(mask, s, -jnp.inf)
    m = jnp.max(s, axis=-1, keepdims=True)
    m = jnp.where(jnp.isfinite(m), m, 0.0)
    return jnp.exp(s - m)


def _masked_probs(s, mask):
    p = _masked_exp(s, mask)
    return p * (1.0 / jnp.maximum(jnp.sum(p, axis=-1, keepdims=True), 1e-30))


def _softmax(s):
    p = jnp.exp(s - jnp.max(s, axis=-1, keepdims=True))
    return p / jnp.sum(p, axis=-1, keepdims=True)


def _split_hi_lo(x):
    hi = x.astype(BF16)
    return hi, (x - hi.astype(F32)).astype(BF16)


def _params(n_axes):
    return pltpu.CompilerParams(dimension_semantics=("arbitrary",) * n_axes, vmem_limit_bytes=VMEM_LIMIT)


def _full(shape):
    return pl.BlockSpec(shape, lambda *_: (0,) * len(shape))


def _mkv_body(x_ref, w_ref, o_ref, ob_ref):
    r = _dot(x_ref[...].astype(BF16), w_ref[...])
    o_ref[...] = r
    ob_ref[...] = r.astype(BF16)


def _mem_kv(mem2d, w_bf):
    rows, tm = mem2d.shape[0], MEM_TOKENS
    return pl.pallas_call(
        _mkv_body, grid=(rows // tm,),
        in_specs=[pl.BlockSpec((tm, D_MODEL), lambda i: (i, 0)), _full((D_MODEL, 2 * MEM_Q))],
        out_specs=[pl.BlockSpec((tm, 2 * MEM_Q), lambda i: (i, 0))] * 2,
        out_shape=[jax.ShapeDtypeStruct((rows, 2 * MEM_Q), F32), jax.ShapeDtypeStruct((rows, 2 * MEM_Q), BF16)],
        compiler_params=_params(1), name="mem_kv")(mem2d, w_bf)


def _inproj_prompt_body(x_ref, g_ref, w_ref, wkvt_ref, lng_ref, lnb_ref, ws_ref, bst_ref, mkv_ref,
                        oa_ref, qh_ref, kv4t_ref, kvwt_ref, kt_ref, v_ref, kcv_ref, gate_ref, om_ref, *, tm):
    ab = _rms(x_ref[0], g_ref[...]).astype(BF16)

    def proj(lo, hi):
        return _dot(ab, w_ref[:, lo:hi])

    z1 = jax.nn.gelu(proj(0, 512))
    z2 = _layernorm(jax.nn.gelu(proj(512, 1024)), lng_ref[...], lnb_ref[...])
    tril = lax.broadcasted_iota(I32, (CHUNK, CHUNK), 0) >= lax.broadcasted_iota(I32, (CHUNK, CHUNK), 1)
    for g in range(SGU_GROUPS):
        cols = slice(g * SGU_GROUP_DIM, (g + 1) * SGU_GROUP_DIM)
        w = jnp.where(tril, ws_ref[g], 0.0).astype(BF16)
        for c in range(tm // CHUNK):
            rows = slice(c * CHUNK, (c + 1) * CHUNK)
            mixed = _dot(w, z2[rows, cols].astype(BF16)) + bst_ref[:, g:g + 1]
            oa_ref[0, rows, cols] = (z1[rows, cols] * mixed).astype(BF16)

    q = proj(1024, 1536)
    for h in range(NSA_HEADS):
        qh_ref[0, h] = q[:, h * NSA_DH:(h + 1) * NSA_DH].astype(BF16)

    kvt = _dot_nt(wkvt_ref[...], ab)
    kv4t_ref[0] = kvt[:512]
    kvwt_ref[0] = kvt[512:]
    for c, r0 in enumerate((256, 320, 512, 576)):
        for u in range(tm // LANES):
            kt_ref[0, c, u] = kvt[r0:r0 + NSA_DH, u * LANES:(u + 1) * LANES].astype(BF16)
    kv = proj(1536, 2304)
    kcv_ref[0] = kv[:, :256].astype(BF16)
    v_ref[0, 0] = kv[:, 384:512].astype(BF16)
    v_ref[0, 1] = kv[:, 640:768].astype(BF16)

    gate_ref[0, 0] = jax.nn.sigmoid(proj(2304, 2432))
    gate_ref[0, 1] = jax.nn.sigmoid(proj(2432, 2560))

    mq = proj(2560, 3072).astype(BF16)
    for h in range(MEM_HEADS):
        cols = slice(h * MEM_DH, (h + 1) * MEM_DH)
        p = _softmax(_dot_nt(mq[:, cols], mkv_ref[0, :, cols]) * (MEM_DH ** -0.5))
        om_ref[0, :, cols] = _dot(p.astype(BF16), mkv_ref[0, :, MEM_Q + h * MEM_DH:MEM_Q + (h + 1) * MEM_DH]).astype(BF16)


def _inproj_prompt(x, g1, w, wkvt, lng, lnb, ws, bst, mkv_bf):
    b, t, _ = x.shape
    tm = min(512, t)

    def row(width):
        return pl.BlockSpec((1, tm, width), lambda bi, i: (bi, i, 0))

    def col(height):
        return pl.BlockSpec((1, height, tm), lambda bi, i: (bi, 0, i))

    def heads(n):
        return pl.BlockSpec((1, n, tm, NSA_DH), lambda bi, i: (bi, 0, i, 0))

    return pl.pallas_call(
        functools.partial(_inproj_prompt_body, tm=tm), grid=(b, t // tm),
        in_specs=[row(D_MODEL), _full((1, D_MODEL)), _full(w.shape), _full(wkvt.shape),
                  _full((1, SGU_DIM)), _full((1, SGU_DIM)),
                  _full((SGU_GROUPS, CHUNK, CHUNK)), _full((CHUNK, SGU_GROUPS)),
                  pl.BlockSpec((1, MEM_TOKENS, 2 * MEM_Q), lambda bi, i: (bi, 0, 0))],
        out_specs=[row(SGU_DIM), heads(NSA_HEADS), col(512), col(256),
                   pl.BlockSpec((1, 4, tm // LANES, NSA_DH, LANES), lambda bi, i: (bi, 0, i, 0, 0)),
                   pl.BlockSpec((1, 2, tm, NSA_KVW), lambda bi, i: (bi, 0, i, 0)), row(256),
                   pl.BlockSpec((1, 2, tm, LANES), lambda bi, i: (bi, 0, i, 0)), row(MEM_Q)],
        out_shape=[jax.ShapeDtypeStruct((b, t, SGU_DIM), BF16), jax.ShapeDtypeStruct((b, NSA_HEADS, t, NSA_DH), BF16),
                   jax.ShapeDtypeStruct((b, 512, t), F32), jax.ShapeDtypeStruct((b, 256, t), F32),
                   jax.ShapeDtypeStruct((b, 4, t // LANES, NSA_DH, LANES), BF16),
                   jax.ShapeDtypeStruct((b, 2, t, NSA_KVW), BF16), jax.ShapeDtypeStruct((b, t, 256), BF16),
                   jax.ShapeDtypeStruct((b, 2, t, LANES), F32), jax.ShapeDtypeStruct((b, t, MEM_Q), BF16)],
        compiler_params=_params(2), name="inproj_prompt")(x, g1, w, wkvt, lng, lnb, ws, bst, mkv_bf)


def _inproj_sample_body(x_ref, g_ref, w_ref, lng_ref, lnb_ref, wsc_ref, bsc_ref,
                        oa_ref, z2_ref, q_ref, kv_ref, gate_ref, mq_ref):
    ab = _rms(x_ref[...], g_ref[...]).astype(BF16)

    def proj(lo, hi):
        return _dot(ab, w_ref[:, lo:hi])

    z1 = jax.nn.gelu(proj(0, 512))
    z2 = _layernorm(jax.nn.gelu(proj(512, 1024)), lng_ref[...], lnb_ref[...])
    z2_ref[...] = z2
    oa_ref[...] = (z1 * (wsc_ref[...] * z2 + bsc_ref[...])).astype(BF16)
    q_ref[...] = proj(1024, 2048).astype(BF16)
    kv_ref[...] = proj(2048, 2816)
    gate_ref[...] = jax.nn.sigmoid(proj(2816, 2944))
    mq_ref[...] = proj(2944, 3456)


def _inproj_sample(x, g1, w, lng, lnb, wsc, bsc):
    n = x.shape[0]
    shapes = [((n, SGU_DIM), BF16), ((n, SGU_DIM), F32), ((n, 2 * NSA_Q), BF16), ((n, 768), F32),
              ((n, LANES), F32), ((n, MEM_Q), F32)]
    return pl.pallas_call(
        _inproj_sample_body, grid=(1,),
        in_specs=[_full(x.shape), _full((1, D_MODEL)), _full(w.shape), _full((1, SGU_DIM)), _full((1, SGU_DIM)),
                  _full((1, SGU_DIM)), _full((1, SGU_DIM))],
        out_specs=[_full(s) for s, _ in shapes],
        out_shape=[jax.ShapeDtypeStruct(s, d) for s, d in shapes],
        compiler_params=_params(1), name="inproj_sample")(x, g1, w, lng, lnb, wsc, bsc)


def _compress_finish(ab, b1, w2):
    n = ab.shape[0]
    c = ab[:, :LANES] + pltpu.roll(ab[:, LANES:], n - 1, 0) + b1
    return _dot(jax.nn.gelu(c).astype(BF16), w2)


def _compress_prompt_body(r_ref, wk_ref, b1_ref, w2_ref, o_ref):
    out = _compress_finish(_dot(r_ref[0, 0], wk_ref[0]), b1_ref[0], w2_ref[0])
    for g in range(NSA_KV):
        o_ref[0, 0, g] = out[:, g * NSA_DH:(g + 1) * NSA_DH].astype(BF16)


def _compress_prompt(r, wk, b1, w2):
    _, b, n, width = r.shape
    return pl.pallas_call(
        _compress_prompt_body, grid=(2, b),
        in_specs=[pl.BlockSpec((1, 1, n, width), lambda k, bi: (k, bi, 0, 0)),
                  pl.BlockSpec((1, width, 2 * LANES), lambda k, bi: (k, 0, 0)),
                  pl.BlockSpec((1, 1, LANES), lambda k, bi: (k, 0, 0)),
                  pl.BlockSpec((1, LANES, LANES), lambda k, bi: (k, 0, 0))],
        out_specs=pl.BlockSpec((1, 1, NSA_KV, n, NSA_DH), lambda k, bi: (k, bi, 0, 0, 0)),
        out_shape=jax.ShapeDtypeStruct((2, b, NSA_KV, n, NSA_DH), BF16),
        compiler_params=_params(2), name="compress_prompt")(r, wk, b1, w2)


def _topk_rows(w, k, on_pick):
    n = w.shape[0]
    jidx = lax.broadcasted_iota(I32, w.shape, 0)
    for r in range(k):
        m = jnp.max(w, axis=0, keepdims=True)
        first = jnp.min(jnp.where(w == m, jidx, n), axis=0, keepdims=True)
        pick = jidx == first
        on_pick(r, first, pick)
        w = jnp.where(pick, -jnp.inf, w)


def _nsa_prompt_body(q_ref, g_ref, kc_ref, vc_ref, kst_ref, vs_ref, kwt_ref, vw_ref, mt_ref, es_ref, o_ref):
    ns = mt_ref.shape[0]
    hr = HEAD_ROWS
    n_units, rows = NSA_REP // hr, hr * Q_BLOCK
    blk = pl.program_id(2)
    s0 = blk * Q_BLOCK
    tq = s0 + lax.broadcasted_iota(I32, (Q_BLOCK, 1), 0)
    tcol = jnp.concatenate([tq] * hr, axis=0)
    q = [q_ref[0, u * hr:(u + 1) * hr].reshape(rows, NSA_DH) for u in range(n_units)]

    ncp = kc_ref.shape[-2]
    cend = lax.broadcasted_iota(I32, (1, ncp), 1) * CMP_STRIDE + (CMP_BLOCK - 1)
    cmask = cend <= tcol
    o_c, psum = [], None
    for u in range(n_units):
        probs = _masked_probs(_dot_nt(q[u], kc_ref[0, 0, 0]), cmask)
        o_c.append(_dot(probs.astype(BF16), vc_ref[0, 0, 0]))
        for r in range(hr):
            part = probs[r * Q_BLOCK:(r + 1) * Q_BLOCK]
            psum = part if psum is None else psum + part

    hi, lo = _split_hi_lo(psum)
    imp = _dot_nt(mt_ref[...], hi) + _dot_nt(mt_ref[...], lo)
    jidx = lax.broadcasted_iota(I32, (ns, Q_BLOCK), 0)
    cur = lax.shift_right_logical(s0 + lax.broadcasted_iota(I32, (ns, Q_BLOCK), 1), 6)
    valid = jidx <= cur
    forced = valid & ((jidx == 0) | (jidx > cur - N_LOCAL))
    w = jnp.where(forced, jnp.inf, jnp.where(valid, imp, -jnp.inf))
    picked = [jnp.zeros((ns, Q_BLOCK), F32)]

    def on_pick(r, first, pick):
        picked[0] = jnp.where(pick, 1.0, picked[0])

    _topk_rows(w, min(SEL_TOPK, ns), on_pick)
    bias = jnp.where(valid, jnp.where(picked[0] > 0.0, 0.0, MASKED), MASKED)
    bias_t = bias.T.astype(BF16)

    bias_rows = jnp.concatenate([bias_t] * hr, axis=0)
    q_aug = [jnp.concatenate([bias_rows, q[u]], axis=1) for u in range(n_units)]
    tiles_per = KV_TILE // LANES

    def keys_aug(kt):
        kt4 = jnp.concatenate([kst_ref[0, 0, kt * tiles_per + j] for j in range(tiles_per)], axis=1)
        return jnp.concatenate([es_ref[kt], kt4], axis=0)

    second = pl.program_id(1) == 1

    def vals_aug(v_both):
        own = (lax.broadcasted_iota(I32, v_both.shape, 1) >= NSA_DH) == second
        return jnp.where(own, v_both, jnp.ones((), BF16))

    def normalised(acc):
        acc = jnp.where(second, pltpu.roll(acc, NSA_DH, 1), acc)
        return acc[:, :NSA_DH] * (1.0 / jnp.maximum(acc[:, NSA_DH:NSA_DH + 1], 1e-30))

    def update(carry, qa, k_aug, v, mask):
        m, acc = carry
        sc = _dot(qa, k_aug)
        if mask is not None:
            sc = jnp.where(mask, sc, MASKED)
        m_new = jnp.maximum(m, jnp.max(sc, axis=-1, keepdims=True))
        p = jnp.exp(sc - m_new)
        return m_new, jnp.exp(m - m_new) * acc + _dot(p.astype(BF16), v)

    def step(kt, carry):
        k_aug = keys_aug(kt)
        v = vals_aug(vs_ref[0, 0, pl.ds(pl.multiple_of(kt * KV_TILE, KV_TILE), KV_TILE), :])
        return tuple(update(carry[u], q_aug[u], k_aug, v, None) for u in range(n_units))

    one = (jnp.full((rows, 1), -jnp.inf, F32), jnp.zeros((rows, LANES), F32))
    last = (s0 + Q_BLOCK - 1) // KV_TILE
    carry = lax.fori_loop(0, last // 4, lambda p, c: step(4 * p + 3, step(4 * p + 2, step(4 * p + 1, step(4 * p, c)))),
                          (one,) * n_units)
    carry = lax.fori_loop((last // 4) * 4, last, step, carry)
    k0 = pl.multiple_of(last * KV_TILE, KV_TILE)
    causal = k0 + lax.broadcasted_iota(I32, (1, KV_TILE), 1) <= tcol
    k_aug, v = keys_aug(last), vals_aug(vs_ref[0, 0, pl.ds(k0, KV_TILE), :])
    o_s = [normalised(update(carry[u], q_aug[u], k_aug, v, causal)[1]) for u in range(n_units)]

    span = WINDOW + Q_BLOCK
    w0 = jnp.maximum(blk - WINDOW // Q_BLOCK, 0)
    start = pl.multiple_of(w0 * Q_BLOCK, Q_BLOCK)
    kw = jnp.concatenate([kwt_ref[0, 0, w0 + j] for j in range(span // LANES)], axis=1)
    vw = vals_aug(vw_ref[0, 0, pl.ds(start, span), :])
    wpos = start + lax.broadcasted_iota(I32, (1, span), 1)
    wmask = (wpos <= tcol) & (wpos > tcol - WINDOW)

    gates = g_ref[0, 0]
    outs = []
    for u in range(n_units):
        o_w = normalised(_dot(_masked_exp(_dot(q[u], kw), wmask).astype(BF16), vw))
        for r in range(hr):
            h, sl = u * hr + r, slice(r * Q_BLOCK, (r + 1) * Q_BLOCK)
            outs.append(gates[:, h:h + 1] * o_c[u][sl] + gates[:, 4 + h:5 + h] * o_s[u][sl] + gates[:, 8 + h:9 + h] * o_w[sl])
    o_ref[0] = jnp.concatenate(outs, axis=1).astype(BF16)


def _nsa_prompt(qh, gates, kcv, kt, v, mt, es):
    b, _, t, _ = qh.shape
    ncp = kcv.shape[-2]

    def keys(c):
        return pl.BlockSpec((1, 1, t // LANES, NSA_DH, LANES), lambda bi, g, i: (bi, c + g, 0, 0, 0))

    def vals(kind):
        return pl.BlockSpec((1, 1, t, NSA_KVW), lambda bi, g, i: (bi, kind, 0, 0))

    def cmp(k):
        return pl.BlockSpec((1, 1, 1, ncp, NSA_DH), lambda bi, g, i: (k, bi, g, 0, 0))

    return pl.pallas_call(
        _nsa_prompt_body, grid=(b, NSA_KV, t // Q_BLOCK),
        in_specs=[pl.BlockSpec((1, NSA_REP, Q_BLOCK, NSA_DH), lambda bi, g, i: (bi, g, i, 0)),
                  pl.BlockSpec((1, 1, Q_BLOCK, LANES), lambda bi, g, i: (bi, g, i, 0)),
                  cmp(0), cmp(1), keys(0), vals(0), keys(2), vals(1), _full(mt.shape), _full(es.shape)],
        out_specs=pl.BlockSpec((1, Q_BLOCK, NSA_REP * NSA_DH), lambda bi, g, i: (bi, i, g)),
        out_shape=jax.ShapeDtypeStruct((b, t, NSA_Q), BF16),
        compiler_params=_params(3), name="nsa_prompt")(qh, gates, kcv, kcv, kt, v, kt, v, mt, es)


def _sample_cmp_body(pt_ref, *refs, n_chunks):
    pc = PAGES_PER_STEP
    pages = refs[:pc]
    q_ref, wk_ref, b1_ref, w2_ref, m_ref, perm_ref, oc_ref, imp_ref, rk_ref, ab_ref = refs[pc:]
    c = pl.program_id(1)
    per_page = 128 // CMP_STRIDE

    def by_offset(k, kind):
        x = jnp.concatenate([pages[k + d][0, kind * LANES:(kind + 1) * LANES, :].astype(BF16) for d in range(2)], axis=1)
        return _dot(x, perm_ref[...]).astype(BF16).T

    for kind in range(2):
        for k in range(0, pc, 2):
            pair = by_offset(k, kind)
            for j in range(CMP_STRIDE):
                rk_ref[kind, k * per_page:(k + 2) * per_page, j * LANES:(j + 1) * LANES] = (
                    pair[2 * j * per_page:2 * (j + 1) * per_page])
    row0 = pl.multiple_of(c * (pc * per_page), pc * per_page)
    for kind in range(2):
        ab_ref[kind, pl.ds(row0, pc * per_page), :] = _dot(rk_ref[kind], wk_ref[kind])

    @pl.when(c == n_chunks - 1)
    def _():
        kc = _compress_finish(ab_ref[0], b1_ref[0], w2_ref[0]).astype(BF16)
        vc = _compress_finish(ab_ref[1], b1_ref[1], w2_ref[1]).astype(BF16)
        ncp = kc.shape[0]
        row = lax.broadcasted_iota(I32, (NSA_HEADS, LANES), 0)
        lane = lax.broadcasted_iota(I32, (NSA_HEADS, LANES), 1)
        qbd = jnp.where((lane >= NSA_DH) == (row >= NSA_REP), q_ref[0], jnp.zeros((), BF16))
        s = _dot_nt(qbd, kc)
        n = lax.broadcasted_iota(I32, (1, ncp), 1)
        probs = _masked_probs(s, n * CMP_STRIDE + (CMP_BLOCK - 1) <= ncp * CMP_STRIDE)
        oc_ref[0] = _dot(probs.astype(BF16), vc)
        hi, lo = _split_hi_lo(probs)
        imp8 = _dot(hi, m_ref[...]) + _dot(lo, m_ref[...])
        r8 = lax.broadcasted_iota(I32, imp8.shape, 0)
        imp_ref[0, 0:1, :] = jnp.sum(jnp.where(r8 < NSA_REP, imp8, 0.0), axis=0, keepdims=True)
        imp_ref[0, 1:2, :] = jnp.sum(jnp.where(r8 >= NSA_REP, imp8, 0.0), axis=0, keepdims=True)


def _sample_cmp(pt_flat, cache_t, qrep, wk, b1, w2, m, n_pages):
    nb = qrep.shape[0]
    pc = PAGES_PER_STEP
    n_chunks = n_pages // pc
    ncp = n_pages * 128 // CMP_STRIDE
    ns = m.shape[1]

    def page(k):
        return pl.BlockSpec((1, 2 * LANES, 128), lambda b, c, pt: (pt[b * n_pages + c * pc + k], 0, 0))

    grid_spec = pltpu.PrefetchScalarGridSpec(
        num_scalar_prefetch=1, grid=(nb, n_chunks),
        in_specs=[page(k) for k in range(pc)] + [
            pl.BlockSpec((1, NSA_HEADS, LANES), lambda b, c, pt: (b, 0, 0)),
            pl.BlockSpec(wk.shape, lambda b, c, pt: (0, 0, 0)), pl.BlockSpec(b1.shape, lambda b, c, pt: (0, 0, 0)),
            pl.BlockSpec(w2.shape, lambda b, c, pt: (0, 0, 0)), pl.BlockSpec(m.shape, lambda b, c, pt: (0, 0)),
            pl.BlockSpec((256, 256), lambda b, c, pt: (0, 0))],
        out_specs=[pl.BlockSpec((1, NSA_HEADS, LANES), lambda b, c, pt: (b, 0, 0)),
                   pl.BlockSpec((1, NSA_KV, ns), lambda b, c, pt: (b, 0, 0))],
        scratch_shapes=[pltpu.VMEM((2, pc * 8, CMP_STRIDE * LANES), BF16), pltpu.VMEM((2, ncp, 2 * LANES), F32)])
    pos = np.arange(256)
    perm = np.zeros((256, 256), np.float32)
    perm[pos, (pos % CMP_STRIDE) * 16 + (pos // 128) * 8 + (pos % 128) // CMP_STRIDE] = 1.0
    return pl.pallas_call(
        functools.partial(_sample_cmp_body, n_chunks=n_chunks), grid_spec=grid_spec,
        out_shape=[jax.ShapeDtypeStruct((nb, NSA_HEADS, LANES), F32), jax.ShapeDtypeStruct((nb, NSA_KV, ns), F32)],
        compiler_params=_params(2), name="sample_compress")(
            pt_flat, *([cache_t] * pc), qrep, wk, b1, w2, m, jnp.asarray(perm, BF16))


def _sample_topk_body(imp_ref, idx_ref, *, k):
    ns = imp_ref.shape[1]
    idx_ref[...] = jnp.zeros(idx_ref.shape, I32)
    for g in range(NSA_KV):
        jidx = lax.broadcasted_iota(I32, imp_ref.shape[1:], 0)
        w = jnp.where((jidx == 0) | (jidx > ns - N_LOCAL), jnp.inf, imp_ref[g])

        def on_pick(r, first, pick, g=g):
            idx_ref[g, r:r + 1, :] = first

        _topk_rows(w, k, on_pick)


def _sample_topk(imp_t, k):
    _, ns, nb = imp_t.shape
    return pl.pallas_call(
        functools.partial(_sample_topk_body, k=k), grid=(1,),
        in_specs=[_full(imp_t.shape)], out_specs=_full((NSA_KV, SEL_TOPK, nb)),
        out_shape=jax.ShapeDtypeStruct((NSA_KV, SEL_TOPK, nb), I32),
        compiler_params=_params(1), name="sample_topk")(imp_t)


def _attend_with_self(qg, keys_t, vals_t, k_self, v_self, mask):
    s = _dot(qg, keys_t)
    if mask is not None:
        s = jnp.where(mask, s, -jnp.inf)
    s_self = jnp.sum(qg.astype(F32) * k_self.astype(BF16).astype(F32), axis=-1, keepdims=True)
    m = jnp.maximum(jnp.max(s, axis=-1, keepdims=True), s_self)
    p = jnp.exp(s - m)
    p_self = jnp.exp(s_self - m)
    den = jnp.sum(p, axis=-1, keepdims=True) + p_self
    num = _dot_nt(p.astype(BF16), vals_t) + p_self.astype(BF16).astype(F32) * v_self.astype(BF16).astype(F32)
    return num / den


def _sample_attn_body(idx_ref, pt_ref, *refs, n_sel):
    n_blk = NSA_KV * n_sel
    blocks = refs[:n_blk]
    (win_ref, q_ref, kvn_ref, gate_ref, oc_ref, mq_ref, mem_ref,
     ob_ref, om_ref, wn_ref, ks_ref, vs_ref) = refs[n_blk:]
    b = pl.program_id(0)
    row = lax.broadcasted_iota(I32, (NSA_HEADS, NSA_DH), 0)
    q8 = q_ref[0][:, :NSA_DH]
    qg = [jnp.where((row >= NSA_REP) == (g == 1), q8, jnp.zeros((), BF16)) for g in range(NSA_KV)]
    kvn = kvn_ref[0]

    def new_row(kind, g):
        lo = kind * NSA_KVW + g * NSA_DH
        return kvn[:, lo:lo + NSA_DH]

    lane = lax.broadcasted_iota(I32, (1, n_sel * LANES), 1)
    upper = (lane & SEL_BLOCK) != 0
    slot = lax.shift_right_logical(lane, 7)
    o_sel = []
    for g in range(NSA_KV):
        want_upper = jnp.zeros((1, n_sel * LANES), I32)
        for k in range(n_sel):
            blk = blocks[g * n_sel + k]
            ks_ref[g, :, k * LANES:(k + 1) * LANES] = blk[0, g * NSA_DH:(g + 1) * NSA_DH, :].astype(BF16)
            vs_ref[g, :, k * LANES:(k + 1) * LANES] = blk[0, NSA_KVW + g * NSA_DH:NSA_KVW + (g + 1) * NSA_DH, :].astype(BF16)
            j = idx_ref[b * (NSA_KV * SEL_TOPK) + g * SEL_TOPK + k]
            want_upper = jnp.where(slot == k, j & 1, want_upper)
        o_sel.append(_attend_with_self(qg[g], ks_ref[g], vs_ref[g], new_row(2, g), new_row(3, g),
                                       upper == (want_upper != 0)))
    o_s = jnp.where(row < NSA_REP, o_sel[0], o_sel[1])

    win = win_ref[0]
    wbuf = win.shape[1]
    wcol = lax.broadcasted_iota(I32, (1, wbuf), 1)
    o_win = []
    for g in range(NSA_KV):
        kt = win[g * NSA_DH:(g + 1) * NSA_DH].astype(BF16)
        vt = win[NSA_KVW + g * NSA_DH:NSA_KVW + (g + 1) * NSA_DH].astype(BF16)
        o_win.append(_attend_with_self(qg[g], kt, vt, new_row(4, g), new_row(5, g), wcol >= 1))
    o_w = jnp.where(row < NSA_REP, o_win[0], o_win[1])
    rows_w = 2 * NSA_KVW
    eye = lax.broadcasted_iota(I32, (rows_w, rows_w), 0) == lax.broadcasted_iota(I32, (rows_w, rows_w), 1)
    new_col = jnp.sum(jnp.where(eye, kvn[:, 4 * NSA_KVW:], 0.0), axis=-1, keepdims=True)
    wn_ref[0] = jnp.where(lax.broadcasted_iota(I32, win.shape, 1) == wbuf - 1, new_col, pltpu.roll(win, wbuf - 1, 1))

    gate = gate_ref[0]
    grow = lax.broadcasted_iota(I32, (NSA_HEADS, LANES), 0)
    glane = lax.broadcasted_iota(I32, (NSA_HEADS, LANES), 1)
    oc = jnp.where((glane >= NSA_DH) == (grow >= NSA_REP), oc_ref[0], 0.0)
    o = jnp.zeros((NSA_HEADS, NSA_DH), F32)
    for j, branch in enumerate((oc[:, :NSA_DH] + oc[:, NSA_DH:], o_s, o_w)):
        gj = jnp.sum(jnp.where(glane == j * NSA_HEADS + grow, gate, 0.0), axis=-1, keepdims=True)
        o = o + gj * branch
    ob_ref[0] = o

    mq = mq_ref[0]
    per_tok = 2 * MEM_HEADS
    for h in range(MEM_HEADS):
        qh = jnp.broadcast_to(mq[:, h * MEM_DH:(h + 1) * MEM_DH], (NSA_HEADS, MEM_DH)).astype(BF16)
        km = mem_ref[0, pl.ds(h, MEM_TOKENS, stride=per_tok), :].astype(BF16)
        vm = mem_ref[0, pl.ds(MEM_HEADS + h, MEM_TOKENS, stride=per_tok), :].astype(BF16)
        p = _softmax(_dot_nt(qh, km) * (MEM_DH ** -0.5))
        om_ref[0, :, h * MEM_DH:(h + 1) * MEM_DH] = _dot(p.astype(BF16), vm)[0:1]


def _sample_attn(idx_flat, pt_flat, cache_t, win_t, qrep, kvn, gate, oc, mq, memkv, n_pages, n_sel):
    nb, rows_w, wbuf = win_t.shape

    def sel_block(g, k):
        def index(b, idx, pt):
            j = idx[b * (NSA_KV * SEL_TOPK) + g * SEL_TOPK + k]
            return (pt[b * n_pages + lax.shift_right_logical(j, 1)], 1, 0)
        return pl.BlockSpec((1, 2 * NSA_KVW, 128), index)

    def per_seq(shape):
        return pl.BlockSpec((1,) + shape, lambda b, idx, pt: (b,) + (0,) * len(shape))

    grid_spec = pltpu.PrefetchScalarGridSpec(
        num_scalar_prefetch=2, grid=(nb,),
        in_specs=[sel_block(g, k) for g in range(NSA_KV) for k in range(n_sel)] + [
            per_seq((rows_w, wbuf)), per_seq((NSA_HEADS, LANES)), per_seq((1, 768)), per_seq((1, LANES)),
            per_seq((NSA_HEADS, LANES)), per_seq((1, MEM_Q)), per_seq(memkv.shape[1:])],
        out_specs=[per_seq((NSA_HEADS, NSA_DH)), per_seq((1, MEM_Q)), per_seq((rows_w, wbuf))],
        scratch_shapes=[pltpu.VMEM((NSA_KV, NSA_DH, n_sel * LANES), BF16)] * 2)
    return pl.pallas_call(
        functools.partial(_sample_attn_body, n_sel=n_sel), grid_spec=grid_spec,
        out_shape=[jax.ShapeDtypeStruct((nb, NSA_HEADS, NSA_DH), F32), jax.ShapeDtypeStruct((nb, 1, MEM_Q), F32),
                   jax.ShapeDtypeStruct((nb, rows_w, wbuf), F32)],
        compiler_params=_params(1), name="sample_attn")(
            idx_flat, pt_flat, *([cache_t] * (NSA_KV * n_sel)), win_t, qrep, kvn, gate, oc, mq, memkv)


def _merge_body(x_ref, oa_ref, ob_ref, om_ref, g1_ref, wmg_ref, wb_ref, wo_ref, g2_ref, rwh_ref, rwl_ref, rb_ref,
                h1_ref, hn_ref, gt_ref):
    x = x_ref[...]
    ab = _rms(x, g1_ref[...]).astype(BF16)
    s = None
    for k, o_ref in enumerate((oa_ref, ob_ref, om_ref)):
        gk = jax.nn.sigmoid(_dot(ab, wmg_ref[:, k * D_MODEL:(k + 1) * D_MODEL]))
        term = gk * _dot(o_ref[...], wb_ref[k])
        s = term if s is None else s + term
    h1 = x + _dot(s.astype(BF16), wo_ref[...])
    h1_ref[...] = h1
    hn = _rms(h1, g2_ref[...])
    hi, lo = _split_hi_lo(hn)
    hn_ref[...] = hi

    logit = _dot_nt(rwh_ref[...], hi) + _dot_nt(rwh_ref[...], lo) + _dot_nt(rwl_ref[...], hi) + rb_ref[...]
    picked = [jnp.zeros(logit.shape, F32)]

    def on_pick(r, first, pick):
        picked[0] = jnp.where(pick, 1.0, picked[0])

    _topk_rows(logit, TOP_K, on_pick)
    top = jnp.max(logit, axis=0, keepdims=True)
    ex = jnp.where(picked[0] > 0.0, jnp.exp(logit - top), 0.0)
    gt_ref[...] = ex / jnp.sum(ex, axis=0, keepdims=True)


def _merge(x, oa, ob, om, g1, wmg, wb, wo, g2, rwh, rwl, rb, tm):
    n = x.shape[0]

    def row(width):
        return pl.BlockSpec((tm, width), lambda i: (i, 0))

    return pl.pallas_call(
        _merge_body, grid=(n // tm,),
        in_specs=[row(D_MODEL), row(512), row(512), row(512), _full((1, D_MODEL)), _full(wmg.shape), _full(wb.shape),
                  _full(wo.shape), _full((1, D_MODEL)), _full(rwh.shape), _full(rwl.shape), _full(rb.shape)],
        out_specs=[row(D_MODEL), row(D_MODEL), pl.BlockSpec((N_EXPERTS, tm), lambda i: (0, i))],
        out_shape=[jax.ShapeDtypeStruct((n, D_MODEL), F32), jax.ShapeDtypeStruct((n, D_MODEL), BF16),
                   jax.ShapeDtypeStruct((N_EXPERTS, n), F32)],
        compiler_params=_params(1), name="merge")(x, oa, ob, om, g1, wmg, wb, wo, g2, rwh, rwl, rb)


def _moe_body(cnt_ref, hn_ref, h1_ref, gt_ref, u_ref, wgu_ref, bgu_ref, wd_ref, bd_ref, fg_ref, y_ref,
              rank_ref, *, sub, cap):
    i, e = pl.program_id(0), pl.program_id(1)
    n_sub = hn_ref.shape[0] // sub

    @pl.when(e == 0)
    def _():
        for s in range(n_sub):
            cols = slice(s * sub, (s + 1) * sub)
            sel = jnp.where(gt_ref[:, cols] > 0.0, 1.0, 0.0).astype(BF16)
            rank_ref[:, cols] = _dot(sel, u_ref[...])
        y_ref[...] = h1_ref[...]

    slot = lax.broadcasted_iota(I32, (cap, sub), 0).astype(F32)
    spans = [slice(s * sub, (s + 1) * sub) for s in range(n_sub)]
    g_rows = [gt_ref[pl.ds(e, 1), sp] for sp in spans]
    r_rows = [rank_ref[pl.ds(e, 1), sp] for sp in spans]

    def chunk(ch, carry):
        want = slot + (ch * cap).astype(F32)
        hits = [(g_rows[s] > 0.0) & (r_rows[s] == want) for s in range(n_sub)]
        onehots = [jnp.where(h, 1.0, 0.0).astype(BF16) for h in hits]
        xe = jnp.concatenate([_dot(onehots[s], hn_ref[spans[s], :]).astype(BF16) for s in range(n_sub)], axis=0)
        gu = _dot(xe, wgu_ref[0]) + bgu_ref[0]
        gate = jnp.minimum(gu[:, :D_EXPERT], SWIGLU_LIMIT)
        up = jnp.clip(gu[:, D_EXPERT:], -SWIGLU_LIMIT, SWIGLU_LIMIT)
        act = (up + 1.0) * gate * jax.nn.sigmoid(SWIGLU_ALPHA * gate)
        ye = _dot(act.astype(BF16), wd_ref[0]) + bd_ref[0]
        for s in range(n_sub):
            w = jnp.sum(jnp.where(hits[s], g_rows[s], 0.0), axis=-1, keepdims=True)
            y_ref[spans[s], :] += _dot_tn(onehots[s], (ye[s * cap:(s + 1) * cap] * w).astype(BF16))
        return carry

    most = cnt_ref[i * n_sub * N_EXPERTS + e]
    for s in range(1, n_sub):
        most = jnp.maximum(most, cnt_ref[(i * n_sub + s) * N_EXPERTS + e])
    lax.fori_loop(0, (most + cap - 1) // cap, chunk, 0)

    @pl.when(e == N_EXPERTS - 1)
    def _():
        y_ref[...] = _rms(y_ref[...], fg_ref[...])


def _moe(counts, hn, h1, gate_t, upper, wgu, bgu, wd, bd, fg, tile, sub, cap):
    n = hn.shape[0]
    once = pl.Buffered(1)
    grid_spec = pltpu.PrefetchScalarGridSpec(
        num_scalar_prefetch=1, grid=(n // tile, N_EXPERTS),
        in_specs=[pl.BlockSpec((tile, D_MODEL), lambda i, e, c: (i, 0)),
                  pl.BlockSpec((tile, D_MODEL), lambda i, e, c: (i, 0), pipeline_mode=once),
                  pl.BlockSpec((N_EXPERTS, tile), lambda i, e, c: (0, i)),
                  pl.BlockSpec((sub, sub), lambda i, e, c: (0, 0), pipeline_mode=once),
                  pl.BlockSpec((1, D_MODEL, 2 * D_EXPERT), lambda i, e, c: (e, 0, 0)),
                  pl.BlockSpec((1, 1, 2 * D_EXPERT), lambda i, e, c: (e, 0, 0)),
                  pl.BlockSpec((1, D_EXPERT, D_MODEL), lambda i, e, c: (e, 0, 0)),
                  pl.BlockSpec((1, 1, D_MODEL), lambda i, e, c: (e, 0, 0)),
                  pl.BlockSpec((1, D_MODEL), lambda i, e, c: (0, 0))],
        out_specs=pl.BlockSpec((tile, D_MODEL), lambda i, e, c: (i, 0)),
        scratch_shapes=[pltpu.VMEM((N_EXPERTS, tile), F32)])
    return pl.pallas_call(
        functools.partial(_moe_body, sub=sub, cap=cap), grid_spec=grid_spec,
        out_shape=jax.ShapeDtypeStruct((n, D_MODEL), F32),
        compiler_params=_params(2), name="moe")(counts, hn, h1, gate_t, upper, wgu, bgu, wd, bd, fg)


def _gate_columns(order):
    idx = np.zeros((LANES,), np.int32)
    ok = np.zeros((LANES,), bool)
    for c, (h, j) in enumerate(order):
        idx[c], ok[c] = C_NG + h * 3 + j, True
    return idx, ok


def _prep_in_weights(w_in):
    def sect(lo, hi):
        return w_in[:, lo:hi]

    def gates(order):
        idx, ok = _gate_columns(order)
        return jnp.where(jnp.asarray(ok)[None, :], w_in[:, idx], 0.0)

    w_q = sect(C_Q, C_KV) * (NSA_DH ** -0.5)
    g_prompt = [gates([(NSA_REP * g + r, j) for j in range(3) for r in range(NSA_REP)]) for g in range(NSA_KV)]
    w_prompt = jnp.concatenate([sect(C_U, C_Q), w_q, sect(C_KV, C_NG)] + g_prompt + [sect(C_MQ, C_MG)], axis=1)
    q_rep = jnp.concatenate([w_q.reshape(D_MODEL, NSA_HEADS, 1, NSA_DH)] * 2, axis=2).reshape(D_MODEL, 2 * NSA_Q)
    g_sample = gates([(h, j) for j in range(3) for h in range(NSA_HEADS)])
    w_sample = jnp.concatenate([sect(C_U, C_Q), q_rep, sect(C_KV, C_NG), g_sample, sect(C_MQ, C_MG)], axis=1)
    w_kv_t = w_in.T[C_KV:C_NG]
    return w_prompt.astype(BF16), w_sample.astype(BF16), sect(C_MG, w_in.shape[1]).astype(BF16), w_kv_t.astype(BF16)


def _prep_compress(w1, b1, w2):
    eye = jnp.eye(NSA_KV, dtype=F32)
    w = w1.reshape(2, 2, CMP_STRIDE, NSA_DH, NSA_DH)
    wk = jnp.einsum('khjde,xg->kjxdhge', w, eye).reshape(2, CMP_STRIDE * LANES, 2 * LANES)
    b = jnp.concatenate([b1, b1], axis=-1).reshape(2, 1, LANES)
    w2d = jnp.einsum('kde,xg->kxdge', w2, eye).reshape(2, LANES, LANES)
    return wk.astype(BF16), b, w2d.astype(BF16)


def _cmp_to_sel_t(ncp, ns):
    cs = np.arange(ncp) * CMP_STRIDE
    ss = np.arange(ns) * SEL_BLOCK
    m = (cs[None, :] <= ss[:, None] + SEL_BLOCK - 1) & (cs[None, :] + CMP_BLOCK - 1 >= ss[:, None])
    m[:, ncp - 1] = False
    return m.astype(np.float32)


def _block_onehot(t):
    e = (np.arange(t // SEL_BLOCK)[:, None] == (np.arange(t) // SEL_BLOCK)[None, :]).astype(np.float32)
    return e.reshape(t // SEL_BLOCK, t // KV_TILE, KV_TILE).transpose(1, 0, 2)


def _moe_group(h1_and_gates, weights, tile, sub, cap):
    h1, hn, gate_t = h1_and_gates
    n = h1.shape[0]
    counts = jnp.sum((gate_t > 0.0).reshape(N_EXPERTS, n // sub, sub), axis=-1, dtype=I32).T.reshape(-1)
    upper = jnp.asarray(np.triu(np.ones((sub, sub), np.float32), 1), BF16)
    return _moe(counts, hn, h1, gate_t, upper, *weights, tile, sub, cap)


def kernel(x_prompt, x_sample, mem_prompt, cache_nsa_kv, cache_nsa_win, cache_mem_kv, page_table,
           norm1_g, w_in, sgu_ln_g, sgu_ln_b, sgu_w, sgu_b, cmp_w1, cmp_b1, cmp_w2, w_mem_kv,
           w_branch, w_out, norm2_g, router_w, router_b, exp_w_gate_up, exp_b_gate_up,
           exp_w_down, exp_b_down, final_norm_g):
    assert norm1_g.shape[0] == 1, "one layer"
    bp, t, _ = x_prompt.shape
    nb = x_sample.shape[0]
    n_pool, page = cache_nsa_kv.shape[1:3]
    n_pages = page_table.shape[1]
    past = n_pages * page
    wbuf = cache_nsa_win.shape[2]
    assert page == 128 and wbuf == WINDOW and n_pages % PAGES_PER_STEP == 0 and t % KV_TILE == 0 and t >= WINDOW + Q_BLOCK

    g1, g2, fg = norm1_g[0][None], norm2_g[0][None], final_norm_g[None]
    lng, lnb = sgu_ln_g[0][None], sgu_ln_b[0][None]
    w_prompt, w_sample, w_mg, w_kv_t = _prep_in_weights(w_in[0])
    wk, b1c, w2c = _prep_compress(cmp_w1[0], cmp_b1[0], cmp_w2[0])
    wb, wo = w_branch[0].astype(BF16), w_out[0].astype(BF16)
    rwh, rwl = _split_hi_lo(router_w[0].T)
    rb = router_b[0][:, None]
    moe_w = (exp_w_gate_up[0].astype(BF16), exp_b_gate_up[0][:, None, :], exp_w_down[0].astype(BF16),
             exp_b_down[0][:, None, :], fg)
    merge_w = (g1, w_mg, wb, wo, g2, rwh, rwl, rb)

    mkv, mkv_bf = _mem_kv(mem_prompt.reshape(bp * MEM_TOKENS, D_MODEL), w_mem_kv[0].astype(BF16))
    oa, qh, kv4t, kvwt, kt, v, kcv_rows, gates, om = _inproj_prompt(
        x_prompt, g1, w_prompt, w_kv_t, lng, lnb, sgu_w[0], sgu_b[0].T, mkv_bf.reshape(bp, MEM_TOKENS, 2 * MEM_Q))
    ncp, ns = t // CMP_STRIDE, t // SEL_BLOCK
    r = jnp.stack([kcv_rows[:, :, k * LANES:(k + 1) * LANES].reshape(bp, ncp, CMP_STRIDE * LANES) for k in range(2)])
    kcv = _compress_prompt(r, wk, b1c, w2c)
    mt = jnp.asarray(_cmp_to_sel_t(ncp, ns), BF16)
    ob = _nsa_prompt(qh, gates, kcv, kt, v, mt, jnp.asarray(_block_onehot(t), BF16))
    n = bp * t
    x2 = x_prompt.reshape(n, D_MODEL)
    y_prompt = _moe_group(_merge(x2, oa.reshape(n, 512), ob.reshape(n, 512), om.reshape(n, 512), *merge_w, tm=min(512, n)),
                          moe_w, min(MOE_BLOCK, n), min(MOE_TILE, n), MOE_CAP)

    xs = x_sample.reshape(nb, D_MODEL)
    wsc = jnp.repeat(sgu_w[0][:, 0, 0], SGU_GROUP_DIM)[None]
    bsc = jnp.repeat(sgu_b[0][:, 0], SGU_GROUP_DIM)[None]
    oa_s, z2_s, q_s, kv_s, gate_s, mq_s = _inproj_sample(xs, g1, w_sample, lng, lnb, wsc, bsc)
    qrep = q_s.reshape(nb, NSA_HEADS, LANES)
    pt_flat = page_table.reshape(-1)
    ncp_s, ns_s = past // CMP_STRIDE, past // SEL_BLOCK
    m_s = jnp.asarray(_cmp_to_sel_t(ncp_s, ns_s).T, BF16)
    cache_t = jnp.transpose(cache_nsa_kv[0], (0, 2, 3, 4, 1)).reshape(n_pool, 4 * NSA_KVW, page)
    win_t = jnp.transpose(cache_nsa_win[0], (0, 2, 3, 4, 1)).reshape(nb, 2 * NSA_KVW, wbuf)
    oc_s, imp_s = _sample_cmp(pt_flat, cache_t, qrep, wk, b1c, w2c, m_s, n_pages)
    n_sel = min(SEL_TOPK, ns_s + 1) - 1
    idx = _sample_topk(imp_s.transpose(1, 2, 0), n_sel)
    idx_flat = idx.transpose(2, 0, 1).reshape(-1)
    ob_s, om_s, win_new = _sample_attn(
        idx_flat, pt_flat, cache_t, win_t, qrep, kv_s[:, None, :], gate_s[:, None, :], oc_s, mq_s[:, None, :],
        cache_mem_kv[0].reshape(nb, MEM_TOKENS * 2 * MEM_HEADS, MEM_DH), n_pages, n_sel)
    y_sample = _moe_group(_merge(xs, oa_s, ob_s.reshape(nb, NSA_Q).astype(BF16), om_s.reshape(nb, MEM_Q).astype(BF16),
                                 *merge_w, tm=nb), moe_w, nb, nb, 32)

    def positions_last(x_t, n_kinds):
        lead, _, npos = x_t.shape
        return jnp.transpose(x_t.reshape(lead, n_kinds, NSA_KV, NSA_DH, npos), (0, 4, 1, 2, 3))[None]

    return (y_prompt.reshape(bp, t, D_MODEL), y_sample.reshape(nb, 1, D_MODEL),
            positions_last(kv4t, 4), positions_last(kvwt[:, :, t - WINDOW:], 2),
            mkv.reshape(1, bp, MEM_TOKENS, 2, MEM_HEADS, MEM_DH), kv_s[:, :512].reshape(1, nb, 1, 4, NSA_KV, NSA_DH),
            positions_last(win_new, 2), z2_s.reshape(1, nb, 1, SGU_DIM))
```

```python
import functools

import numpy as np
import jax
import jax.numpy as jnp
from jax import lax
from jax.experimental import pallas as pl
from jax.experimental.pallas import tpu as pltpu

F32, BF16, I32 = jnp.float32, jnp.bfloat16, jnp.int32

D_MODEL = 1024
SGU_GROUPS, SGU_GROUP_DIM, SGU_DIM, CHUNK = 4, 128, 512, 128
NSA_HEADS, NSA_KV, NSA_REP, NSA_DH = 8, 2, 4, 64
CMP_BLOCK, CMP_STRIDE, SEL_BLOCK, SEL_TOPK, N_LOCAL, WINDOW, Q_BLOCK = 32, 16, 64, 16, 2, 512, 128
MEM_TOKENS, MEM_HEADS, MEM_DH = 256, 4, 128
N_EXPERTS, TOP_K, D_EXPERT = 32, 4, 1024
SWIGLU_LIMIT, SWIGLU_ALPHA, EPS = 7.0, 1.702, 1e-5
NSA_Q, NSA_KVW, MEM_Q = NSA_HEADS * NSA_DH, NSA_KV * NSA_DH, MEM_HEADS * MEM_DH
C_U, C_V, C_Q, C_KV, C_NG, C_MQ, C_MG = 0, 512, 1024, 1536, 2304, 2328, 2840

LANES = 128
VMEM_LIMIT = 56 * 1024 * 1024
MASKED = -1e9
KV_TILE = 512
HEAD_ROWS = 4
PAGES_PER_STEP = 32
MOE_TILE, MOE_CAP = 1024, 160
MOE_BLOCK = 2048


def _dot(a, b):
    return jnp.dot(a, b, preferred_element_type=F32)


def _dot_nt(a, b):
    return lax.dot_general(a, b, (((1,), (1,)), ((), ())), preferred_element_type=F32)


def _dot_tn(a, b):
    return lax.dot_general(a, b, (((0,), (0,)), ((), ())), preferred_element_type=F32)


def _rms(x, g):
    return x * lax.rsqrt(jnp.mean(x * x, axis=-1, keepdims=True) + EPS) * g


def _layernorm(x, g, b):
    mu = jnp.mean(x, axis=-1, keepdims=True)
    xc = x - mu
    return xc * lax.rsqrt(jnp.mean(xc * xc, axis=-1, keepdims=True) + EPS) * g + b


def _masked_exp(s, mask):
    s = jnp.where(mask, s, -jnp.inf)
    m = jnp.max(s, axis=-1, keepdims=True)
    m = jnp.where(jnp.isfinite(m), m, 0.0)
    return jnp.exp(s - m)


def _masked_probs(s, mask):
    p = _masked_exp(s, mask)
    return p * (1.0 / jnp.maximum(jnp.sum(p, axis=-1, keepdims=True), 1e-30))


def _softmax(s):
    p = jnp.exp(s - jnp.max(s, axis=-1, keepdims=True))
    return p / jnp.sum(p, axis=-1, keepdims=True)


def _split_hi_lo(x):
    hi = x.astype(BF16)
    return hi, (x - hi.astype(F32)).astype(BF16)


def _params(n_axes):
    return pltpu.CompilerParams(dimension_semantics=("arbitrary",) * n_axes, vmem_limit_bytes=VMEM_LIMIT)


def _full(shape):
    return pl.BlockSpec(shape, lambda *_: (0,) * len(shape))


def _mkv_body(x_ref, w_ref, o_ref, ob_ref):
    r = _dot(x_ref[...].astype(BF16), w_ref[...])
    o_ref[...] = r
    ob_ref[...] = r.astype(BF16)


def _mem_kv(mem2d, w_bf):
    rows, tm = mem2d.shape[0], MEM_TOKENS
    return pl.pallas_call(
        _mkv_body, grid=(rows // tm,),
        in_specs=[pl.BlockSpec((tm, D_MODEL), lambda i: (i, 0)), _full((D_MODEL, 2 * MEM_Q))],
        out_specs=[pl.BlockSpec((tm, 2 * MEM_Q), lambda i: (i, 0))] * 2,
        out_shape=[jax.ShapeDtypeStruct((rows, 2 * MEM_Q), F32), jax.ShapeDtypeStruct((rows, 2 * MEM_Q), BF16)],
        compiler_params=_params(1), name="mem_kv")(mem2d, w_bf)


def _inproj_prompt_body(x_ref, g_ref, w_ref, wkvt_ref, lng_ref, lnb_ref, ws_ref, bst_ref, mkv_ref,
                        oa_ref, qh_ref, kv4t_ref, kvwt_ref, kt_ref, v_ref, kcv_ref, gate_ref, om_ref, *, tm):
    ab = _rms(x_ref[0], g_ref[...]).astype(BF16)

    def proj(lo, hi):
        return _dot(ab, w_ref[:, lo:hi])

    z1 = jax.nn.gelu(proj(0, 512))
    z2 = _layernorm(jax.nn.gelu(proj(512, 1024)), lng_ref[...], lnb_ref[...])
    tril = lax.broadcasted_iota(I32, (CHUNK, CHUNK), 0) >= lax.broadcasted_iota(I32, (CHUNK, CHUNK), 1)
    for g in range(SGU_GROUPS):
        cols = slice(g * SGU_GROUP_DIM, (g + 1) * SGU_GROUP_DIM)
        w = jnp.where(tril, ws_ref[g], 0.0).astype(BF16)
        for c in range(tm // CHUNK):
            rows = slice(c * CHUNK, (c + 1) * CHUNK)
            mixed = _dot(w, z2[rows, cols].astype(BF16)) + bst_ref[:, g:g + 1]
            oa_ref[0, rows, cols] = (z1[rows, cols] * mixed).astype(BF16)

    q = proj(1024, 1536)
    for h in range(NSA_HEADS):
        qh_ref[0, h] = q[:, h * NSA_DH:(h + 1) * NSA_DH].astype(BF16)

    kvt = _dot_nt(wkvt_ref[...], ab)
    kv4t_ref[0] = kvt[:512]
    kvwt_ref[0] = kvt[512:]
    for c, r0 in enumerate((256, 320, 512, 576)):
        for u in range(tm // LANES):
            kt_ref[0, c, u] = kvt[r0:r0 + NSA_DH, u * LANES:(u + 1) * LANES].astype(BF16)
    kcv_ref[0] = proj(1536, 1792).astype(BF16)
    v_ref[0, 0] = proj(1920, 2048).astype(BF16)
    v_ref[0, 1] = proj(2176, 2304).astype(BF16)

    gate_ref[0, 0] = jax.nn.sigmoid(proj(2304, 2432))
    gate_ref[0, 1] = jax.nn.sigmoid(proj(2432, 2560))

    mq = proj(2560, 3072).astype(BF16)
    for h in range(MEM_HEADS):
        cols = slice(h * MEM_DH, (h + 1) * MEM_DH)
        p = _softmax(_dot_nt(mq[:, cols], mkv_ref[0, :, cols]) * (MEM_DH ** -0.5))
        om_ref[0, :, cols] = _dot(p.astype(BF16), mkv_ref[0, :, MEM_Q + h * MEM_DH:MEM_Q + (h + 1) * MEM_DH]).astype(BF16)


def _inproj_prompt(x, g1, w, wkvt, lng, lnb, ws, bst, mkv_bf):
    b, t, _ = x.shape
    tm = min(512, t)

    def row(width):
        return pl.BlockSpec((1, tm, width), lambda bi, i: (bi, i, 0))

    def col(height):
        return pl.BlockSpec((1, height, tm), lambda bi, i: (bi, 0, i))

    def heads(n):
        return pl.BlockSpec((1, n, tm, NSA_DH), lambda bi, i: (bi, 0, i, 0))

    return pl.pallas_call(
        functools.partial(_inproj_prompt_body, tm=tm), grid=(b, t // tm),
        in_specs=[row(D_MODEL), _full((1, D_MODEL)), _full(w.shape), _full(wkvt.shape),
                  _full((1, SGU_DIM)), _full((1, SGU_DIM)),
                  _full((SGU_GROUPS, CHUNK, CHUNK)), _full((CHUNK, SGU_GROUPS)),
                  pl.BlockSpec((1, MEM_TOKENS, 2 * MEM_Q), lambda bi, i: (bi, 0, 0))],
        out_specs=[row(SGU_DIM), heads(NSA_HEADS), col(512), col(256),
                   pl.BlockSpec((1, 4, tm // LANES, NSA_DH, LANES), lambda bi, i: (bi, 0, i, 0, 0)),
                   pl.BlockSpec((1, 2, tm, NSA_KVW), lambda bi, i: (bi, 0, i, 0)), row(256),
                   pl.BlockSpec((1, 2, tm, LANES), lambda bi, i: (bi, 0, i, 0)), row(MEM_Q)],
        out_shape=[jax.ShapeDtypeStruct((b, t, SGU_DIM), BF16), jax.ShapeDtypeStruct((b, NSA_HEADS, t, NSA_DH), BF16),
                   jax.ShapeDtypeStruct((b, 512, t), F32), jax.ShapeDtypeStruct((b, 256, t), F32),
                   jax.ShapeDtypeStruct((b, 4, t // LANES, NSA_DH, LANES), BF16),
                   jax.ShapeDtypeStruct((b, 2, t, NSA_KVW), BF16), jax.ShapeDtypeStruct((b, t, 256), BF16),
                   jax.ShapeDtypeStruct((b, 2, t, LANES), F32), jax.ShapeDtypeStruct((b, t, MEM_Q), BF16)],
        compiler_params=_params(2), name="inproj_prompt")(x, g1, w, wkvt, lng, lnb, ws, bst, mkv_bf)


def _inproj_sample_body(x_ref, g_ref, w_ref, lng_ref, lnb_ref, wsc_ref, bsc_ref,
                        oa_ref, z2_ref, q_ref, kv_ref, gate_ref, mq_ref):
    ab = _rms(x_ref[...], g_ref[...]).astype(BF16)

    def proj(lo, hi):
        return _dot(ab, w_ref[:, lo:hi])

    z1 = jax.nn.gelu(proj(0, 512))
    z2 = _layernorm(jax.nn.gelu(proj(512, 1024)), lng_ref[...], lnb_ref[...])
    z2_ref[...] = z2
    oa_ref[...] = (z1 * (wsc_ref[...] * z2 + bsc_ref[...])).astype(BF16)
    q_ref[...] = proj(1024, 2048).astype(BF16)
    kv_ref[...] = proj(2048, 2816)
    gate_ref[...] = jax.nn.sigmoid(proj(2816, 2944))
    mq_ref[...] = proj(2944, 3456)


def _inproj_sample(x, g1, w, lng, lnb, wsc, bsc):
    n = x.shape[0]
    shapes = [((n, SGU_DIM), BF16), ((n, SGU_DIM), F32), ((n, 2 * NSA_Q), BF16), ((n, 768), F32),
              ((n, LANES), F32), ((n, MEM_Q), F32)]
    return pl.pallas_call(
        _inproj_sample_body, grid=(1,),
        in_specs=[_full(x.shape), _full((1, D_MODEL)), _full(w.shape), _full((1, SGU_DIM)), _full((1, SGU_DIM)),
                  _full((1, SGU_DIM)), _full((1, SGU_DIM))],
        out_specs=[_full(s) for s, _ in shapes],
        out_shape=[jax.ShapeDtypeStruct(s, d) for s, d in shapes],
        compiler_params=_params(1), name="inproj_sample")(x, g1, w, lng, lnb, wsc, bsc)


def _compress_finish(ab, b1, w2):
    n = ab.shape[0]
    c = ab[:, :LANES] + pltpu.roll(ab[:, LANES:], n - 1, 0) + b1
    return _dot(jax.nn.gelu(c).astype(BF16), w2)


def _compress_prompt_body(r_ref, wk_ref, b1_ref, w2_ref, o_ref):
    out = _compress_finish(_dot(r_ref[0, 0], wk_ref[0]), b1_ref[0], w2_ref[0])
    for g in range(NSA_KV):
        o_ref[0, 0, g] = out[:, g * NSA_DH:(g + 1) * NSA_DH].astype(BF16)


def _compress_prompt(r, wk, b1, w2):
    _, b, n, width = r.shape
    return pl.pallas_call(
        _compress_prompt_body, grid=(2, b),
        in_specs=[pl.BlockSpec((1, 1, n, width), lambda k, bi: (k, bi, 0, 0)),
                  pl.BlockSpec((1, width, 2 * LANES), lambda k, bi: (k, 0, 0)),
                  pl.BlockSpec((1, 1, LANES), lambda k, bi: (k, 0, 0)),
                  pl.BlockSpec((1, LANES, LANES), lambda k, bi: (k, 0, 0))],
        out_specs=pl.BlockSpec((1, 1, NSA_KV, n, NSA_DH), lambda k, bi: (k, bi, 0, 0, 0)),
        out_shape=jax.ShapeDtypeStruct((2, b, NSA_KV, n, NSA_DH), BF16),
        compiler_params=_params(2), name="compress_prompt")(r, wk, b1, w2)


def _topk_rows(w, k, on_pick):
    n = w.shape[0]
    jidx = lax.broadcasted_iota(I32, w.shape, 0)
    for r in range(k):
        m = jnp.max(w, axis=0, keepdims=True)
        first = jnp.min(jnp.where(w == m, jidx, n), axis=0, keepdims=True)
        pick = jidx == first
        on_pick(r, first, pick)
        w = jnp.where(pick, -jnp.inf, w)


def _nsa_prompt_body(q_ref, g_ref, kc_ref, vc_ref, kst_ref, vs_ref, kwt_ref, vw_ref, mt_ref, es_ref, o_ref):
    ns = mt_ref.shape[0]
    hr = HEAD_ROWS
    n_units, rows = NSA_REP // hr, hr * Q_BLOCK
    blk = pl.program_id(2)
    s0 = blk * Q_BLOCK
    tq = s0 + lax.broadcasted_iota(I32, (Q_BLOCK, 1), 0)
    tcol = jnp.concatenate([tq] * hr, axis=0)
    q = [q_ref[0, u * hr:(u + 1) * hr].reshape(rows, NSA_DH) for u in range(n_units)]

    ncp = kc_ref.shape[-2]

    def compressed(width):
        def run():
            cend = lax.broadcasted_iota(I32, (1, width), 1) * CMP_STRIDE + (CMP_BLOCK - 1)
            outs, psum = [], None
            for u in range(n_units):
                probs = _masked_probs(_dot_nt(q[u], kc_ref[0, 0, 0, :width, :]), cend <= tcol)
                outs.append(_dot(probs.astype(BF16), vc_ref[0, 0, 0, :width, :]))
                for r in range(hr):
                    part = probs[r * Q_BLOCK:(r + 1) * Q_BLOCK]
                    psum = part if psum is None else psum + part
            hi, lo = _split_hi_lo(psum)
            return tuple(outs), _dot_nt(mt_ref[:, :width], hi) + _dot_nt(mt_ref[:, :width], lo)
        return run

    widths = list(range(LANES, ncp + 1, LANES)) or [ncp]
    needed = (s0 + Q_BLOCK - CMP_BLOCK) // CMP_STRIDE + 1
    o_c, imp = lax.switch(jnp.minimum((needed - 1) // LANES, len(widths) - 1), [compressed(w) for w in widths])

    jidx = lax.broadcasted_iota(I32, (ns, Q_BLOCK), 0)
    cur = lax.shift_right_logical(s0 + lax.broadcasted_iota(I32, (ns, Q_BLOCK), 1), 6)
    valid = jidx <= cur
    forced = valid & ((jidx == 0) | (jidx > cur - N_LOCAL))
    w = jnp.where(forced, jnp.inf, jnp.where(valid, imp, -jnp.inf))
    picked = [jnp.zeros((ns, Q_BLOCK), F32)]

    def on_pick(r, first, pick):
        picked[0] = jnp.where(pick, 1.0, picked[0])

    _topk_rows(w, min(SEL_TOPK, ns), on_pick)
    bias = jnp.where(valid, jnp.where(picked[0] > 0.0, 0.0, MASKED), MASKED)
    bias_t = bias.T.astype(BF16)

    bias_rows = jnp.concatenate([bias_t] * hr, axis=0)
    q_aug = [jnp.concatenate([bias_rows, q[u]], axis=1) for u in range(n_units)]
    tiles_per = KV_TILE // LANES

    def keys_aug(kt):
        kt4 = jnp.concatenate([kst_ref[0, 0, kt * tiles_per + j] for j in range(tiles_per)], axis=1)
        return jnp.concatenate([es_ref[kt], kt4], axis=0)

    second = pl.program_id(1) == 1

    def vals_aug(v_both):
        own = (lax.broadcasted_iota(I32, v_both.shape, 1) >= NSA_DH) == second
        return jnp.where(own, v_both, jnp.ones((), BF16))

    def normalised(acc):
        acc = jnp.where(second, pltpu.roll(acc, NSA_DH, 1), acc)
        return acc[:, :NSA_DH] * (1.0 / jnp.maximum(acc[:, NSA_DH:NSA_DH + 1], 1e-30))

    def update(carry, qa, k_aug, v, mask):
        m, acc = carry
        sc = _dot(qa, k_aug)
        if mask is not None:
            sc = jnp.where(mask, sc, MASKED)
        m_new = jnp.maximum(m, jnp.max(sc, axis=-1, keepdims=True))
        p = jnp.exp(sc - m_new)
        return m_new, jnp.exp(m - m_new) * acc + _dot(p.astype(BF16), v)

    def step(kt, carry):
        k_aug = keys_aug(kt)
        v = vals_aug(vs_ref[0, 0, pl.ds(pl.multiple_of(kt * KV_TILE, KV_TILE), KV_TILE), :])
        return tuple(update(carry[u], q_aug[u], k_aug, v, None) for u in range(n_units))

    one = (jnp.full((rows, 1), -jnp.inf, F32), jnp.zeros((rows, LANES), F32))
    last = (s0 + Q_BLOCK - 1) // KV_TILE
    carry = lax.fori_loop(0, last // 4, lambda p, c: step(4 * p + 3, step(4 * p + 2, step(4 * p + 1, step(4 * p, c)))),
                          (one,) * n_units)
    carry = lax.fori_loop((last // 4) * 4, last, step, carry)
    k0 = pl.multiple_of(last * KV_TILE, KV_TILE)
    causal = k0 + lax.broadcasted_iota(I32, (1, KV_TILE), 1) <= tcol
    k_aug, v = keys_aug(last), vals_aug(vs_ref[0, 0, pl.ds(k0, KV_TILE), :])
    o_s = [normalised(update(carry[u], q_aug[u], k_aug, v, causal)[1]) for u in range(n_units)]

    span = WINDOW + Q_BLOCK
    w0 = jnp.maximum(blk - WINDOW // Q_BLOCK, 0)
    start = pl.multiple_of(w0 * Q_BLOCK, Q_BLOCK)
    kw = jnp.concatenate([kwt_ref[0, 0, w0 + j] for j in range(span // LANES)], axis=1)
    vw = vals_aug(vw_ref[0, 0, pl.ds(start, span), :])
    wpos = start + lax.broadcasted_iota(I32, (1, span), 1)
    wmask = (wpos <= tcol) & (wpos > tcol - WINDOW)

    gates = g_ref[0, 0]
    outs = []
    for u in range(n_units):
        o_w = normalised(_dot(_masked_exp(_dot(q[u], kw), wmask).astype(BF16), vw))
        for r in range(hr):
            h, sl = u * hr + r, slice(r * Q_BLOCK, (r + 1) * Q_BLOCK)
            outs.append(gates[:, h:h + 1] * o_c[u][sl] + gates[:, 4 + h:5 + h] * o_s[u][sl] + gates[:, 8 + h:9 + h] * o_w[sl])
    o_ref[0] = jnp.concatenate(outs, axis=1).astype(BF16)


def _nsa_prompt(qh, gates, kcv, kt, v, mt, es):
    b, _, t, _ = qh.shape
    ncp = kcv.shape[-2]

    def keys(c):
        return pl.BlockSpec((1, 1, t // LANES, NSA_DH, LANES), lambda bi, g, i: (bi, c + g, 0, 0, 0))

    def vals(kind):
        return pl.BlockSpec((1, 1, t, NSA_KVW), lambda bi, g, i: (bi, kind, 0, 0))

    def cmp(k):
        return pl.BlockSpec((1, 1, 1, ncp, NSA_DH), lambda bi, g, i: (k, bi, g, 0, 0))

    return pl.pallas_call(
        _nsa_prompt_body, grid=(b, NSA_KV, t // Q_BLOCK),
        in_specs=[pl.BlockSpec((1, NSA_REP, Q_BLOCK, NSA_DH), lambda bi, g, i: (bi, g, i, 0)),
                  pl.BlockSpec((1, 1, Q_BLOCK, LANES), lambda bi, g, i: (bi, g, i, 0)),
                  cmp(0), cmp(1), keys(0), vals(0), keys(2), vals(1), _full(mt.shape), _full(es.shape)],
        out_specs=pl.BlockSpec((1, Q_BLOCK, NSA_REP * NSA_DH), lambda bi, g, i: (bi, i, g)),
        out_shape=jax.ShapeDtypeStruct((b, t, NSA_Q), BF16),
        compiler_params=_params(3), name="nsa_prompt")(qh, gates, kcv, kcv, kt, v, kt, v, mt, es)


def _sample_cmp_body(pt_ref, *refs, n_chunks):
    pc = PAGES_PER_STEP
    pages = refs[:pc]
    q_ref, wk_ref, b1_ref, w2_ref, m_ref, perm_ref, oc_ref, imp_ref, rk_ref, ab_ref = refs[pc:]
    c = pl.program_id(1)
    per_page = 128 // CMP_STRIDE

    def by_offset(k, kind):
        x = jnp.concatenate([pages[k + d][0, kind * LANES:(kind + 1) * LANES, :].astype(BF16) for d in range(2)], axis=1)
        return _dot(x, perm_ref[...]).astype(BF16).T

    for kind in range(2):
        for k in range(0, pc, 2):
            pair = by_offset(k, kind)
            for j in range(CMP_STRIDE):
                rk_ref[kind, k * per_page:(k + 2) * per_page, j * LANES:(j + 1) * LANES] = (
                    pair[2 * j * per_page:2 * (j + 1) * per_page])
    row0 = pl.multiple_of(c * (pc * per_page), pc * per_page)
    for kind in range(2):
        ab_ref[kind, pl.ds(row0, pc * per_page), :] = _dot(rk_ref[kind], wk_ref[kind])

    @pl.when(c == n_chunks - 1)
    def _():
        kc = _compress_finish(ab_ref[0], b1_ref[0], w2_ref[0]).astype(BF16)
        vc = _compress_finish(ab_ref[1], b1_ref[1], w2_ref[1]).astype(BF16)
        ncp = kc.shape[0]
        row = lax.broadcasted_iota(I32, (NSA_HEADS, LANES), 0)
        lane = lax.broadcasted_iota(I32, (NSA_HEADS, LANES), 1)
        qbd = jnp.where((lane >= NSA_DH) == (row >= NSA_REP), q_ref[0], jnp.zeros((), BF16))
        s = _dot_nt(qbd, kc)
        n = lax.broadcasted_iota(I32, (1, ncp), 1)
        probs = _masked_probs(s, n * CMP_STRIDE + (CMP_BLOCK - 1) <= ncp * CMP_STRIDE)
        oc_ref[0] = _dot(probs.astype(BF16), vc)
        hi, lo = _split_hi_lo(probs)
        imp8 = _dot(hi, m_ref[...]) + _dot(lo, m_ref[...])
        r8 = lax.broadcasted_iota(I32, imp8.shape, 0)
        imp_ref[0, 0:1, :] = jnp.sum(jnp.where(r8 < NSA_REP, imp8, 0.0), axis=0, keepdims=True)
        imp_ref[0, 1:2, :] = jnp.sum(jnp.where(r8 >= NSA_REP, imp8, 0.0), axis=0, keepdims=True)


def _sample_cmp(pt_flat, cache_t, qrep, wk, b1, w2, m, n_pages):
    nb = qrep.shape[0]
    pc = PAGES_PER_STEP
    n_chunks = n_pages // pc
    ncp = n_pages * 128 // CMP_STRIDE
    ns = m.shape[1]

    def page(k):
        return pl.BlockSpec((1, 2 * LANES, 128), lambda b, c, pt: (pt[b * n_pages + c * pc + k], 0, 0))

    grid_spec = pltpu.PrefetchScalarGridSpec(
        num_scalar_prefetch=1, grid=(nb, n_chunks),
        in_specs=[page(k) for k in range(pc)] + [
            pl.BlockSpec((1, NSA_HEADS, LANES), lambda b, c, pt: (b, 0, 0)),
            pl.BlockSpec(wk.shape, lambda b, c, pt: (0, 0, 0)), pl.BlockSpec(b1.shape, lambda b, c, pt: (0, 0, 0)),
            pl.BlockSpec(w2.shape, lambda b, c, pt: (0, 0, 0)), pl.BlockSpec(m.shape, lambda b, c, pt: (0, 0)),
            pl.BlockSpec((256, 256), lambda b, c, pt: (0, 0))],
        out_specs=[pl.BlockSpec((1, NSA_HEADS, LANES), lambda b, c, pt: (b, 0, 0)),
                   pl.BlockSpec((1, NSA_KV, ns), lambda b, c, pt: (b, 0, 0))],
        scratch_shapes=[pltpu.VMEM((2, pc * 8, CMP_STRIDE * LANES), BF16), pltpu.VMEM((2, ncp, 2 * LANES), F32)])
    pos = np.arange(256)
    perm = np.zeros((256, 256), np.float32)
    perm[pos, (pos % CMP_STRIDE) * 16 + (pos // 128) * 8 + (pos % 128) // CMP_STRIDE] = 1.0
    return pl.pallas_call(
        functools.partial(_sample_cmp_body, n_chunks=n_chunks), grid_spec=grid_spec,
        out_shape=[jax.ShapeDtypeStruct((nb, NSA_HEADS, LANES), F32), jax.ShapeDtypeStruct((nb, NSA_KV, ns), F32)],
        compiler_params=_params(2), name="sample_compress")(
            pt_flat, *([cache_t] * pc), qrep, wk, b1, w2, m, jnp.asarray(perm, BF16))


def _sample_topk_body(imp_ref, idx_ref, *, k):
    ns = imp_ref.shape[1]
    idx_ref[...] = jnp.zeros(idx_ref.shape, I32)
    for g in range(NSA_KV):
        jidx = lax.broadcasted_iota(I32, imp_ref.shape[1:], 0)
        w = jnp.where((jidx == 0) | (jidx > ns - N_LOCAL), jnp.inf, imp_ref[g])

        def on_pick(r, first, pick, g=g):
            idx_ref[g, r:r + 1, :] = first

        _topk_rows(w, k, on_pick)


def _sample_topk(imp_t, k):
    _, ns, nb = imp_t.shape
    return pl.pallas_call(
        functools.partial(_sample_topk_body, k=k), grid=(1,),
        in_specs=[_full(imp_t.shape)], out_specs=_full((NSA_KV, SEL_TOPK, nb)),
        out_shape=jax.ShapeDtypeStruct((NSA_KV, SEL_TOPK, nb), I32),
        compiler_params=_params(1), name="sample_topk")(imp_t)


def _attend_with_self(qg, keys_t, vals_t, k_self, v_self, mask):
    s = _dot(qg, keys_t)
    if mask is not None:
        s = jnp.where(mask, s, -jnp.inf)
    s_self = jnp.sum(qg.astype(F32) * k_self.astype(BF16).astype(F32), axis=-1, keepdims=True)
    m = jnp.maximum(jnp.max(s, axis=-1, keepdims=True), s_self)
    p = jnp.exp(s - m)
    p_self = jnp.exp(s_self - m)
    den = jnp.sum(p, axis=-1, keepdims=True) + p_self
    num = _dot_nt(p.astype(BF16), vals_t) + p_self.astype(BF16).astype(F32) * v_self.astype(BF16).astype(F32)
    return num / den


def _sample_attn_body(idx_ref, pt_ref, *refs, n_sel):
    n_blk = NSA_KV * n_sel
    blocks = refs[:n_blk]
    (win_ref, q_ref, kvn_ref, gate_ref, oc_ref, mq_ref, mem_ref,
     ob_ref, om_ref, wn_ref, ks_ref, vs_ref) = refs[n_blk:]
    b = pl.program_id(0)
    row = lax.broadcasted_iota(I32, (NSA_HEADS, NSA_DH), 0)
    q8 = q_ref[0][:, :NSA_DH]
    qg = [jnp.where((row >= NSA_REP) == (g == 1), q8, jnp.zeros((), BF16)) for g in range(NSA_KV)]
    kvn = kvn_ref[0]

    def new_row(kind, g):
        lo = kind * NSA_KVW + g * NSA_DH
        return kvn[:, lo:lo + NSA_DH]

    lane = lax.broadcasted_iota(I32, (1, n_sel * LANES), 1)
    upper = (lane & SEL_BLOCK) != 0
    slot = lax.shift_right_logical(lane, 7)
    o_sel = []
    for g in range(NSA_KV):
        want_upper = jnp.zeros((1, n_sel * LANES), I32)
        for k in range(n_sel):
            blk = blocks[g * n_sel + k]
            ks_ref[g, :, k * LANES:(k + 1) * LANES] = blk[0, g * NSA_DH:(g + 1) * NSA_DH, :].astype(BF16)
            vs_ref[g, :, k * LANES:(k + 1) * LANES] = blk[0, NSA_KVW + g * NSA_DH:NSA_KVW + (g + 1) * NSA_DH, :].astype(BF16)
            j = idx_ref[b * (NSA_KV * SEL_TOPK) + g * SEL_TOPK + k]
            want_upper = jnp.where(slot == k, j & 1, want_upper)
        o_sel.append(_attend_with_self(qg[g], ks_ref[g], vs_ref[g], new_row(2, g), new_row(3, g),
                                       upper == (want_upper != 0)))
    o_s = jnp.where(row < NSA_REP, o_sel[0], o_sel[1])

    win = win_ref[0]
    wbuf = win.shape[1]
    wcol = lax.broadcasted_iota(I32, (1, wbuf), 1)
    o_win = []
    for g in range(NSA_KV):
        kt = win[g * NSA_DH:(g + 1) * NSA_DH].astype(BF16)
        vt = win[NSA_KVW + g * NSA_DH:NSA_KVW + (g + 1) * NSA_DH].astype(BF16)
        o_win.append(_attend_with_self(qg[g], kt, vt, new_row(4, g), new_row(5, g), wcol >= 1))
    o_w = jnp.where(row < NSA_REP, o_win[0], o_win[1])
    rows_w = 2 * NSA_KVW
    eye = lax.broadcasted_iota(I32, (rows_w, rows_w), 0) == lax.broadcasted_iota(I32, (rows_w, rows_w), 1)
    new_col = jnp.sum(jnp.where(eye, kvn[:, 4 * NSA_KVW:], 0.0), axis=-1, keepdims=True)
    wn_ref[0] = jnp.where(lax.broadcasted_iota(I32, win.shape, 1) == wbuf - 1, new_col, pltpu.roll(win, wbuf - 1, 1))

    gate = gate_ref[0]
    grow = lax.broadcasted_iota(I32, (NSA_HEADS, LANES), 0)
    glane = lax.broadcasted_iota(I32, (NSA_HEADS, LANES), 1)
    oc = jnp.where((glane >= NSA_DH) == (grow >= NSA_REP), oc_ref[0], 0.0)
    o = jnp.zeros((NSA_HEADS, NSA_DH), F32)
    for j, branch in enumerate((oc[:, :NSA_DH] + oc[:, NSA_DH:], o_s, o_w)):
        gj = jnp.sum(jnp.where(glane == j * NSA_HEADS + grow, gate, 0.0), axis=-1, keepdims=True)
        o = o + gj * branch
    ob_ref[0] = o

    mq = mq_ref[0]
    per_tok = 2 * MEM_HEADS
    for h in range(MEM_HEADS):
        qh = jnp.broadcast_to(mq[:, h * MEM_DH:(h + 1) * MEM_DH], (NSA_HEADS, MEM_DH)).astype(BF16)
        km = mem_ref[0, pl.ds(h, MEM_TOKENS, stride=per_tok), :].astype(BF16)
        vm = mem_ref[0, pl.ds(MEM_HEADS + h, MEM_TOKENS, stride=per_tok), :].astype(BF16)
        p = _softmax(_dot_nt(qh, km) * (MEM_DH ** -0.5))
        om_ref[0, :, h * MEM_DH:(h + 1) * MEM_DH] = _dot(p.astype(BF16), vm)[0:1]


def _sample_attn(idx_flat, pt_flat, cache_t, win_t, qrep, kvn, gate, oc, mq, memkv, n_pages, n_sel):
    nb, rows_w, wbuf = win_t.shape

    def sel_block(g, k):
        def index(b, idx, pt):
            j = idx[b * (NSA_KV * SEL_TOPK) + g * SEL_TOPK + k]
            return (pt[b * n_pages + lax.shift_right_logical(j, 1)], 1, 0)
        return pl.BlockSpec((1, 2 * NSA_KVW, 128), index)

    def per_seq(shape):
        return pl.BlockSpec((1,) + shape, lambda b, idx, pt: (b,) + (0,) * len(shape))

    grid_spec = pltpu.PrefetchScalarGridSpec(
        num_scalar_prefetch=2, grid=(nb,),
        in_specs=[sel_block(g, k) for g in range(NSA_KV) for k in range(n_sel)] + [
            per_seq((rows_w, wbuf)), per_seq((NSA_HEADS, LANES)), per_seq((1, 768)), per_seq((1, LANES)),
            per_seq((NSA_HEADS, LANES)), per_seq((1, MEM_Q)), per_seq(memkv.shape[1:])],
        out_specs=[per_seq((NSA_HEADS, NSA_DH)), per_seq((1, MEM_Q)), per_seq((rows_w, wbuf))],
        scratch_shapes=[pltpu.VMEM((NSA_KV, NSA_DH, n_sel * LANES), BF16)] * 2)
    return pl.pallas_call(
        functools.partial(_sample_attn_body, n_sel=n_sel), grid_spec=grid_spec,
        out_shape=[jax.ShapeDtypeStruct((nb, NSA_HEADS, NSA_DH), F32), jax.ShapeDtypeStruct((nb, 1, MEM_Q), F32),
                   jax.ShapeDtypeStruct((nb, rows_w, wbuf), F32)],
        compiler_params=_params(1), name="sample_attn")(
            idx_flat, pt_flat, *([cache_t] * (NSA_KV * n_sel)), win_t, qrep, kvn, gate, oc, mq, memkv)


def _merge_body(x_ref, oa_ref, ob_ref, om_ref, g1_ref, wmg_ref, wb_ref, wo_ref, g2_ref, rwh_ref, rwl_ref, rb_ref,
                h1_ref, hn_ref, gt_ref):
    x = x_ref[...]
    ab = _rms(x, g1_ref[...]).astype(BF16)
    s = None
    for k, o_ref in enumerate((oa_ref, ob_ref, om_ref)):
        gk = jax.nn.sigmoid(_dot(ab, wmg_ref[:, k * D_MODEL:(k + 1) * D_MODEL]))
        term = gk * _dot(o_ref[...], wb_ref[k])
        s = term if s is None else s + term
    h1 = x + _dot(s.astype(BF16), wo_ref[...])
    h1_ref[...] = h1
    hn = _rms(h1, g2_ref[...])
    hi, lo = _split_hi_lo(hn)
    hn_ref[...] = hi

    logit = _dot_nt(rwh_ref[...], hi) + _dot_nt(rwh_ref[...], lo) + _dot_nt(rwl_ref[...], hi) + rb_ref[...]
    picked = [jnp.zeros(logit.shape, F32)]

    def on_pick(r, first, pick):
        picked[0] = jnp.where(pick, 1.0, picked[0])

    _topk_rows(logit, TOP_K, on_pick)
    top = jnp.max(logit, axis=0, keepdims=True)
    ex = jnp.where(picked[0] > 0.0, jnp.exp(logit - top), 0.0)
    gt_ref[...] = ex / jnp.sum(ex, axis=0, keepdims=True)


def _merge(x, oa, ob, om, g1, wmg, wb, wo, g2, rwh, rwl, rb, tm):
    n = x.shape[0]

    def row(width):
        return pl.BlockSpec((tm, width), lambda i: (i, 0))

    return pl.pallas_call(
        _merge_body, grid=(n // tm,),
        in_specs=[row(D_MODEL), row(512), row(512), row(512), _full((1, D_MODEL)), _full(wmg.shape), _full(wb.shape),
                  _full(wo.shape), _full((1, D_MODEL)), _full(rwh.shape), _full(rwl.shape), _full(rb.shape)],
        out_specs=[row(D_MODEL), row(D_MODEL), pl.BlockSpec((N_EXPERTS, tm), lambda i: (0, i))],
        out_shape=[jax.ShapeDtypeStruct((n, D_MODEL), F32), jax.ShapeDtypeStruct((n, D_MODEL), BF16),
                   jax.ShapeDtypeStruct((N_EXPERTS, n), F32)],
        compiler_params=_params(1), name="merge")(x, oa, ob, om, g1, wmg, wb, wo, g2, rwh, rwl, rb)


def _moe_body(cnt_ref, hn_ref, h1_ref, gt_ref, u_ref, wgu_ref, bgu_ref, wd_ref, bd_ref, fg_ref, y_ref,
              rank_ref, *, sub, cap):
    i, e = pl.program_id(0), pl.program_id(1)
    n_sub = hn_ref.shape[0] // sub

    @pl.when(e == 0)
    def _():
        for s in range(n_sub):
            cols = slice(s * sub, (s + 1) * sub)
            sel = jnp.where(gt_ref[:, cols] > 0.0, 1.0, 0.0).astype(BF16)
            rank_ref[:, cols] = _dot(sel, u_ref[...])
        y_ref[...] = h1_ref[...]

    slot = lax.broadcasted_iota(I32, (cap, sub), 0).astype(F32)
    spans = [slice(s * sub, (s + 1) * sub) for s in range(n_sub)]
    g_rows = [gt_ref[pl.ds(e, 1), sp] for sp in spans]
    r_rows = [rank_ref[pl.ds(e, 1), sp] for sp in spans]

    def chunk(ch, carry):
        want = slot + (ch * cap).astype(F32)
        hits = [(g_rows[s] > 0.0) & (r_rows[s] == want) for s in range(n_sub)]
        onehots = [jnp.where(h, 1.0, 0.0).astype(BF16) for h in hits]
        xe = jnp.concatenate([_dot(onehots[s], hn_ref[spans[s], :]).astype(BF16) for s in range(n_sub)], axis=0)
        gu = _dot(xe, wgu_ref[0]) + bgu_ref[0]
        gate = jnp.minimum(gu[:, :D_EXPERT], SWIGLU_LIMIT)
        up = jnp.clip(gu[:, D_EXPERT:], -SWIGLU_LIMIT, SWIGLU_LIMIT)
        act = (up + 1.0) * gate * jax.nn.sigmoid(SWIGLU_ALPHA * gate)
        ye = _dot(act.astype(BF16), wd_ref[0]) + bd_ref[0]
        for s in range(n_sub):
            w = jnp.sum(jnp.where(hits[s], g_rows[s], 0.0), axis=-1, keepdims=True)
            y_ref[spans[s], :] += _dot_tn(onehots[s], (ye[s * cap:(s + 1) * cap] * w).astype(BF16))
        return carry

    most = cnt_ref[i * n_sub * N_EXPERTS + e]
    for s in range(1, n_sub):
        most = jnp.maximum(most, cnt_ref[(i * n_sub + s) * N_EXPERTS + e])
    lax.fori_loop(0, (most + cap - 1) // cap, chunk, 0)

    @pl.when(e == N_EXPERTS - 1)
    def _():
        y_ref[...] = _rms(y_ref[...], fg_ref[...])


def _moe(counts, hn, h1, gate_t, upper, wgu, bgu, wd, bd, fg, tile, sub, cap):
    n = hn.shape[0]
    once = pl.Buffered(1)
    grid_spec = pltpu.PrefetchScalarGridSpec(
        num_scalar_prefetch=1, grid=(n // tile, N_EXPERTS),
        in_specs=[pl.BlockSpec((tile, D_MODEL), lambda i, e, c: (i, 0)),
                  pl.BlockSpec((tile, D_MODEL), lambda i, e, c: (i, 0), pipeline_mode=once),
                  pl.BlockSpec((N_EXPERTS, tile), lambda i, e, c: (0, i)),
                  pl.BlockSpec((sub, sub), lambda i, e, c: (0, 0), pipeline_mode=once),
                  pl.BlockSpec((1, D_MODEL, 2 * D_EXPERT), lambda i, e, c: (e, 0, 0)),
                  pl.BlockSpec((1, 1, 2 * D_EXPERT), lambda i, e, c: (e, 0, 0)),
                  pl.BlockSpec((1, D_EXPERT, D_MODEL), lambda i, e, c: (e, 0, 0)),
                  pl.BlockSpec((1, 1, D_MODEL), lambda i, e, c: (e, 0, 0)),
                  pl.BlockSpec((1, D_MODEL), lambda i, e, c: (0, 0))],
        out_specs=pl.BlockSpec((tile, D_MODEL), lambda i, e, c: (i, 0)),
        scratch_shapes=[pltpu.VMEM((N_EXPERTS, tile), F32)])
    return pl.pallas_call(
        functools.partial(_moe_body, sub=sub, cap=cap), grid_spec=grid_spec,
        out_shape=jax.ShapeDtypeStruct((n, D_MODEL), F32),
        compiler_params=_params(2), name="moe")(counts, hn, h1, gate_t, upper, wgu, bgu, wd, bd, fg)


def _gate_columns(order):
    idx = np.zeros((LANES,), np.int32)
    ok = np.zeros((LANES,), bool)
    for c, (h, j) in enumerate(order):
        idx[c], ok[c] = C_NG + h * 3 + j, True
    return idx, ok


def _prep_in_weights(w_in):
    def sect(lo, hi):
        return w_in[:, lo:hi]

    def gates(order):
        idx, ok = _gate_columns(order)
        return jnp.where(jnp.asarray(ok)[None, :], w_in[:, idx], 0.0)

    w_q = sect(C_Q, C_KV) * (NSA_DH ** -0.5)
    g_prompt = [gates([(NSA_REP * g + r, j) for j in range(3) for r in range(NSA_REP)]) for g in range(NSA_KV)]
    w_prompt = jnp.concatenate([sect(C_U, C_Q), w_q, sect(C_KV, C_NG)] + g_prompt + [sect(C_MQ, C_MG)], axis=1)
    q_rep = jnp.concatenate([w_q.reshape(D_MODEL, NSA_HEADS, 1, NSA_DH)] * 2, axis=2).reshape(D_MODEL, 2 * NSA_Q)
    g_sample = gates([(h, j) for j in range(3) for h in range(NSA_HEADS)])
    w_sample = jnp.concatenate([sect(C_U, C_Q), q_rep, sect(C_KV, C_NG), g_sample, sect(C_MQ, C_MG)], axis=1)
    w_kv_t = w_in.T[C_KV:C_NG]
    return w_prompt.astype(BF16), w_sample.astype(BF16), sect(C_MG, w_in.shape[1]).astype(BF16), w_kv_t.astype(BF16)


def _prep_compress(w1, b1, w2):
    eye = jnp.eye(NSA_KV, dtype=F32)
    w = w1.reshape(2, 2, CMP_STRIDE, NSA_DH, NSA_DH)
    wk = jnp.einsum('khjde,xg->kjxdhge', w, eye).reshape(2, CMP_STRIDE * LANES, 2 * LANES)
    b = jnp.concatenate([b1, b1], axis=-1).reshape(2, 1, LANES)
    w2d = jnp.einsum('kde,xg->kxdge', w2, eye).reshape(2, LANES, LANES)
    return wk.astype(BF16), b, w2d.astype(BF16)


def _cmp_to_sel_t(ncp, ns):
    cs = np.arange(ncp) * CMP_STRIDE
    ss = np.arange(ns) * SEL_BLOCK
    m = (cs[None, :] <= ss[:, None] + SEL_BLOCK - 1) & (cs[None, :] + CMP_BLOCK - 1 >= ss[:, None])
    m[:, ncp - 1] = False
    return m.astype(np.float32)


def _block_onehot(t):
    e = (np.arange(t // SEL_BLOCK)[:, None] == (np.arange(t) // SEL_BLOCK)[None, :]).astype(np.float32)
    return e.reshape(t // SEL_BLOCK, t // KV_TILE, KV_TILE).transpose(1, 0, 2)


def _moe_group(h1_and_gates, weights, tile, sub, cap):
    h1, hn, gate_t = h1_and_gates
    n = h1.shape[0]
    counts = jnp.sum((gate_t > 0.0).reshape(N_EXPERTS, n // sub, sub), axis=-1, dtype=I32).T.reshape(-1)
    upper = jnp.asarray(np.triu(np.ones((sub, sub), np.float32), 1), BF16)
    return _moe(counts, hn, h1, gate_t, upper, *weights, tile, sub, cap)


def kernel(x_prompt, x_sample, mem_prompt, cache_nsa_kv, cache_nsa_win, cache_mem_kv, page_table,
           norm1_g, w_in, sgu_ln_g, sgu_ln_b, sgu_w, sgu_b, cmp_w1, cmp_b1, cmp_w2, w_mem_kv,
           w_branch, w_out, norm2_g, router_w, router_b, exp_w_gate_up, exp_b_gate_up,
           exp_w_down, exp_b_down, final_norm_g):
    assert norm1_g.shape[0] == 1, "one layer"
    bp, t, _ = x_prompt.shape
    nb = x_sample.shape[0]
    n_pool, page = cache_nsa_kv.shape[1:3]
    n_pages = page_table.shape[1]
    past = n_pages * page
    wbuf = cache_nsa_win.shape[2]
    assert page == 128 and wbuf == WINDOW and n_pages % PAGES_PER_STEP == 0 and t % KV_TILE == 0 and t >= WINDOW + Q_BLOCK

    g1, g2, fg = norm1_g[0][None], norm2_g[0][None], final_norm_g[None]
    lng, lnb = sgu_ln_g[0][None], sgu_ln_b[0][None]
    w_prompt, w_sample, w_mg, w_kv_t = _prep_in_weights(w_in[0])
    wk, b1c, w2c = _prep_compress(cmp_w1[0], cmp_b1[0], cmp_w2[0])
    wb, wo = w_branch[0].astype(BF16), w_out[0].astype(BF16)
    rwh, rwl = _split_hi_lo(router_w[0].T)
    rb = router_b[0][:, None]
    moe_w = (exp_w_gate_up[0].astype(BF16), exp_b_gate_up[0][:, None, :], exp_w_down[0].astype(BF16),
             exp_b_down[0][:, None, :], fg)
    merge_w = (g1, w_mg, wb, wo, g2, rwh, rwl, rb)

    mkv, mkv_bf = _mem_kv(mem_prompt.reshape(bp * MEM_TOKENS, D_MODEL), w_mem_kv[0].astype(BF16))
    oa, qh, kv4t, kvwt, kt, v, kcv_rows, gates, om = _inproj_prompt(
        x_prompt, g1, w_prompt, w_kv_t, lng, lnb, sgu_w[0], sgu_b[0].T, mkv_bf.reshape(bp, MEM_TOKENS, 2 * MEM_Q))
    ncp, ns = t // CMP_STRIDE, t // SEL_BLOCK
    r = jnp.stack([kcv_rows[:, :, k * LANES:(k + 1) * LANES].reshape(bp, ncp, CMP_STRIDE * LANES) for k in range(2)])
    kcv = _compress_prompt(r, wk, b1c, w2c)
    mt = jnp.asarray(_cmp_to_sel_t(ncp, ns), BF16)
    ob = _nsa_prompt(qh, gates, kcv, kt, v, mt, jnp.asarray(_block_onehot(t), BF16))
    n = bp * t
    x2 = x_prompt.reshape(n, D_MODEL)
    y_prompt = _moe_group(_merge(x2, oa.reshape(n, 512), ob.reshape(n, 512), om.reshape(n, 512), *merge_w, tm=min(512, n)),
                          moe_w, min(MOE_BLOCK, n), min(MOE_TILE, n), MOE_CAP)

    xs = x_sample.reshape(nb, D_MODEL)
    wsc = jnp.repeat(sgu_w[0][:, 0, 0], SGU_GROUP_DIM)[None]
    bsc = jnp.repeat(sgu_b[0][:, 0], SGU_GROUP_DIM)[None]
    oa_s, z2_s, q_s, kv_s, gate_s, mq_s = _inproj_sample(xs, g1, w_sample, lng, lnb, wsc, bsc)
    qrep = q_s.reshape(nb, NSA_HEADS, LANES)
    pt_flat = page_table.reshape(-1)
    ncp_s, ns_s = past // CMP_STRIDE, past // SEL_BLOCK
    m_s = jnp.asarray(_cmp_to_sel_t(ncp_s, ns_s).T, BF16)
    cache_t = jnp.transpose(cache_nsa_kv[0], (0, 2, 3, 4, 1)).reshape(n_pool, 4 * NSA_KVW, page)
    win_t = jnp.transpose(cache_nsa_win[0], (0, 2, 3, 4, 1)).reshape(nb, 2 * NSA_KVW, wbuf)
    oc_s, imp_s = _sample_cmp(pt_flat, cache_t, qrep, wk, b1c, w2c, m_s, n_pages)
    n_sel = min(SEL_TOPK, ns_s + 1) - 1
    idx = _sample_topk(imp_s.transpose(1, 2, 0), n_sel)
    idx_flat = idx.transpose(2, 0, 1).reshape(-1)
    ob_s, om_s, win_new = _sample_attn(
        idx_flat, pt_flat, cache_t, win_t, qrep, kv_s[:, None, :], gate_s[:, None, :], oc_s, mq_s[:, None, :],
        cache_mem_kv[0].reshape(nb, MEM_TOKENS * 2 * MEM_HEADS, MEM_DH), n_pages, n_sel)
    y_sample = _moe_group(_merge(xs, oa_s, ob_s.reshape(nb, NSA_Q).astype(BF16), om_s.reshape(nb, MEM_Q).astype(BF16),
                                 *merge_w, tm=nb), moe_w, nb, nb, 32)

    def positions_last(x_t, n_kinds):
        lead, _, npos = x_t.shape
        return jnp.transpose(x_t.reshape(lead, n_kinds, NSA_KV, NSA_DH, npos), (0, 4, 1, 2, 3))[None]

    return (y_prompt.reshape(bp, t, D_MODEL), y_sample.reshape(nb, 1, D_MODEL),
            positions_last(kv4t, 4), positions_last(kvwt[:, :, t - WINDOW:], 2),
            mkv.reshape(1, bp, MEM_TOKENS, 2, MEM_HEADS, MEM_DH), kv_s[:, :512].reshape(1, nb, 1, 4, NSA_KV, NSA_DH),
            positions_last(win_new, 2), z2_s.reshape(1, nb, 1, SGU_DIM))
```

```python
import functools

import numpy as np
import jax
import jax.numpy as jnp
from jax import lax
from jax.experimental import pallas as pl
from jax.experimental.pallas import tpu as pltpu

F32, BF16, I32 = jnp.float32, jnp.bfloat16, jnp.int32

D_MODEL = 1024
SGU_GROUPS, SGU_GROUP_DIM, SGU_DIM, CHUNK = 4, 128, 512, 128
NSA_HEADS, NSA_KV, NSA_REP, NSA_DH = 8, 2, 4, 64
CMP_BLOCK, CMP_STRIDE, SEL_BLOCK, SEL_TOPK, N_LOCAL, WINDOW, Q_BLOCK = 32, 16, 64, 16, 2, 512, 128
MEM_TOKENS, MEM_HEADS, MEM_DH = 256, 4, 128
N_EXPERTS, TOP_K, D_EXPERT = 32, 4, 1024
SWIGLU_LIMIT, SWIGLU_ALPHA, EPS = 7.0, 1.702, 1e-5
NSA_Q, NSA_KVW, MEM_Q = NSA_HEADS * NSA_DH, NSA_KV * NSA_DH, MEM_HEADS * MEM_DH
C_U, C_V, C_Q, C_KV, C_NG, C_MQ, C_MG = 0, 512, 1024, 1536, 2304, 2328, 2840

LANES = 128
VMEM_LIMIT = 56 * 1024 * 1024
MASKED = -1e9
KV_TILE = 512
HEAD_ROWS = 4
PAGES_PER_STEP = 64
MOE_TILE, MOE_CAP = 1024, 160
MOE_BLOCK = 2048


def _dot(a, b):
    return jnp.dot(a, b, preferred_element_type=F32)


def _dot_nt(a, b):
    return lax.dot_general(a, b, (((1,), (1,)), ((), ())), preferred_element_type=F32)


def _dot_tn(a, b):
    return lax.dot_general(a, b, (((0,), (0,)), ((), ())), preferred_element_type=F32)


def _rms(x, g):
    return x * lax.rsqrt(jnp.mean(x * x, axis=-1, keepdims=True) + EPS) * g


def _layernorm(x, g, b):
    mu = jnp.mean(x, axis=-1, keepdims=True)
    xc = x - mu
    return xc * lax.rsqrt(jnp.mean(xc * xc, axis=-1, keepdims=True) + EPS) * g + b


def _masked_exp(s, mask):
    s = jnp.where(mask, s, -jnp.inf)
    m = jnp.max(s, axis=-1, keepdims=True)
    m = jnp.where(jnp.isfinite(m), m, 0.0)
    return jnp.exp(s - m)


def _masked_probs(s, mask):
    p = _masked_exp(s, mask)
    return p * (1.0 / jnp.maximum(jnp.sum(p, axis=-1, keepdims=True), 1e-30))


def _softmax(s):
    p = jnp.exp(s - jnp.max(s, axis=-1, keepdims=True))
    return p / jnp.sum(p, axis=-1, keepdims=True)


def _split_hi_lo(x):
    hi = x.astype(BF16)
    return hi, (x - hi.astype(F32)).astype(BF16)


def _params(n_axes):
    return pltpu.CompilerParams(dimension_semantics=("arbitrary",) * n_axes, vmem_limit_bytes=VMEM_LIMIT)


def _full(shape):
    return pl.BlockSpec(shape, lambda *_: (0,) * len(shape))


def _mkv_body(x_ref, w_ref, o_ref, ob_ref):
    r = _dot(x_ref[...].astype(BF16), w_ref[...])
    o_ref[...] = r
    ob_ref[...] = r.astype(BF16)


def _mem_kv(mem2d, w_bf):
    rows, tm = mem2d.shape[0], MEM_TOKENS
    return pl.pallas_call(
        _mkv_body, grid=(rows // tm,),
        in_specs=[pl.BlockSpec((tm, D_MODEL), lambda i: (i, 0)), _full((D_MODEL, 2 * MEM_Q))],
        out_specs=[pl.BlockSpec((tm, 2 * MEM_Q), lambda i: (i, 0))] * 2,
        out_shape=[jax.ShapeDtypeStruct((rows, 2 * MEM_Q), F32), jax.ShapeDtypeStruct((rows, 2 * MEM_Q), BF16)],
        compiler_params=_params(1), name="mem_kv")(mem2d, w_bf)


def _inproj_prompt_body(x_ref, g_ref, w_ref, wkvt_ref, lng_ref, lnb_ref, ws_ref, bst_ref, mkv_ref,
                        oa_ref, qh_ref, kv4t_ref, kvwt_ref, kt_ref, v_ref, kcv_ref, gate_ref, om_ref, *, tm):
    ab = _rms(x_ref[0], g_ref[...]).astype(BF16)

    def proj(lo, hi):
        return _dot(ab, w_ref[:, lo:hi])

    z1 = jax.nn.gelu(proj(0, 512))
    z2 = _layernorm(jax.nn.gelu(proj(512, 1024)), lng_ref[...], lnb_ref[...])
    tril = lax.broadcasted_iota(I32, (CHUNK, CHUNK), 0) >= lax.broadcasted_iota(I32, (CHUNK, CHUNK), 1)
    for g in range(SGU_GROUPS):
        cols = slice(g * SGU_GROUP_DIM, (g + 1) * SGU_GROUP_DIM)
        w = jnp.where(tril, ws_ref[g], 0.0).astype(BF16)
        for c in range(tm // CHUNK):
            rows = slice(c * CHUNK, (c + 1) * CHUNK)
            mixed = _dot(w, z2[rows, cols].astype(BF16)) + bst_ref[:, g:g + 1]
            oa_ref[0, rows, cols] = (z1[rows, cols] * mixed).astype(BF16)

    q = proj(1024, 1536)
    for h in range(NSA_HEADS):
        qh_ref[0, h] = q[:, h * NSA_DH:(h + 1) * NSA_DH].astype(BF16)

    kvt = _dot_nt(wkvt_ref[...], ab)
    kv4t_ref[0] = kvt[:512]
    kvwt_ref[0] = kvt[512:]
    for c, r0 in enumerate((256, 320, 512, 576)):
        for u in range(tm // LANES):
            kt_ref[0, c, u] = kvt[r0:r0 + NSA_DH, u * LANES:(u + 1) * LANES].astype(BF16)
    kcv_ref[0] = proj(1536, 1792).astype(BF16)
    v_ref[0, 0] = proj(1920, 2048).astype(BF16)
    v_ref[0, 1] = proj(2176, 2304).astype(BF16)

    gate_ref[0, 0] = jax.nn.sigmoid(proj(2304, 2432))
    gate_ref[0, 1] = jax.nn.sigmoid(proj(2432, 2560))

    mq = proj(2560, 3072).astype(BF16)
    for h in range(MEM_HEADS):
        cols = slice(h * MEM_DH, (h + 1) * MEM_DH)
        p = _softmax(_dot_nt(mq[:, cols], mkv_ref[0, :, cols]) * (MEM_DH ** -0.5))
        om_ref[0, :, cols] = _dot(p.astype(BF16), mkv_ref[0, :, MEM_Q + h * MEM_DH:MEM_Q + (h + 1) * MEM_DH]).astype(BF16)


def _inproj_prompt(x, g1, w, wkvt, lng, lnb, ws, bst, mkv_bf):
    b, t, _ = x.shape
    tm = min(512, t)

    def row(width):
        return pl.BlockSpec((1, tm, width), lambda bi, i: (bi, i, 0))

    def col(height):
        return pl.BlockSpec((1, height, tm), lambda bi, i: (bi, 0, i))

    def heads(n):
        return pl.BlockSpec((1, n, tm, NSA_DH), lambda bi, i: (bi, 0, i, 0))

    return pl.pallas_call(
        functools.partial(_inproj_prompt_body, tm=tm), grid=(b, t // tm),
        in_specs=[row(D_MODEL), _full((1, D_MODEL)), _full(w.shape), _full(wkvt.shape),
                  _full((1, SGU_DIM)), _full((1, SGU_DIM)),
                  _full((SGU_GROUPS, CHUNK, CHUNK)), _full((CHUNK, SGU_GROUPS)),
                  pl.BlockSpec((1, MEM_TOKENS, 2 * MEM_Q), lambda bi, i: (bi, 0, 0))],
        out_specs=[row(SGU_DIM), heads(NSA_HEADS), col(512), col(256),
                   pl.BlockSpec((1, 4, tm // LANES, NSA_DH, LANES), lambda bi, i: (bi, 0, i, 0, 0)),
                   pl.BlockSpec((1, 2, tm, NSA_KVW), lambda bi, i: (bi, 0, i, 0)), row(256),
                   pl.BlockSpec((1, 2, tm, LANES), lambda bi, i: (bi, 0, i, 0)), row(MEM_Q)],
        out_shape=[jax.ShapeDtypeStruct((b, t, SGU_DIM), BF16), jax.ShapeDtypeStruct((b, NSA_HEADS, t, NSA_DH), BF16),
                   jax.ShapeDtypeStruct((b, 512, t), F32), jax.ShapeDtypeStruct((b, 256, t), F32),
                   jax.ShapeDtypeStruct((b, 4, t // LANES, NSA_DH, LANES), BF16),
                   jax.ShapeDtypeStruct((b, 2, t, NSA_KVW), BF16), jax.ShapeDtypeStruct((b, t, 256), BF16),
                   jax.ShapeDtypeStruct((b, 2, t, LANES), F32), jax.ShapeDtypeStruct((b, t, MEM_Q), BF16)],
        compiler_params=_params(2), name="inproj_prompt")(x, g1, w, wkvt, lng, lnb, ws, bst, mkv_bf)


def _inproj_sample_body(x_ref, g_ref, w_ref, lng_ref, lnb_ref, wsc_ref, bsc_ref,
                        oa_ref, z2_ref, q_ref, kv_ref, gate_ref, mq_ref):
    ab = _rms(x_ref[...], g_ref[...]).astype(BF16)

    def proj(lo, hi):
        return _dot(ab, w_ref[:, lo:hi])

    z1 = jax.nn.gelu(proj(0, 512))
    z2 = _layernorm(jax.nn.gelu(proj(512, 1024)), lng_ref[...], lnb_ref[...])
    z2_ref[...] = z2
    oa_ref[...] = (z1 * (wsc_ref[...] * z2 + bsc_ref[...])).astype(BF16)
    q_ref[...] = proj(1024, 2048).astype(BF16)
    kv_ref[...] = proj(2048, 2816)
    gate_ref[...] = jax.nn.sigmoid(proj(2816, 2944))
    mq_ref[...] = proj(2944, 3456)


def _inproj_sample(x, g1, w, lng, lnb, wsc, bsc):
    n = x.shape[0]
    shapes = [((n, SGU_DIM), BF16), ((n, SGU_DIM), F32), ((n, 2 * NSA_Q), BF16), ((n, 768), F32),
              ((n, LANES), F32), ((n, MEM_Q), F32)]
    return pl.pallas_call(
        _inproj_sample_body, grid=(1,),
        in_specs=[_full(x.shape), _full((1, D_MODEL)), _full(w.shape), _full((1, SGU_DIM)), _full((1, SGU_DIM)),
                  _full((1, SGU_DIM)), _full((1, SGU_DIM))],
        out_specs=[_full(s) for s, _ in shapes],
        out_shape=[jax.ShapeDtypeStruct(s, d) for s, d in shapes],
        compiler_params=_params(1), name="inproj_sample")(x, g1, w, lng, lnb, wsc, bsc)


def _compress_finish(ab, b1, w2):
    n = ab.shape[0]
    c = ab[:, :LANES] + pltpu.roll(ab[:, LANES:], n - 1, 0) + b1
    return _dot(jax.nn.gelu(c).astype(BF16), w2)


def _compress_prompt_body(r_ref, wk_ref, b1_ref, w2_ref, o_ref):
    out = _compress_finish(_dot(r_ref[0, 0], wk_ref[0]), b1_ref[0], w2_ref[0])
    for g in range(NSA_KV):
        o_ref[0, 0, g] = out[:, g * NSA_DH:(g + 1) * NSA_DH].astype(BF16)


def _compress_prompt(r, wk, b1, w2):
    _, b, n, width = r.shape
    return pl.pallas_call(
        _compress_prompt_body, grid=(2, b),
        in_specs=[pl.BlockSpec((1, 1, n, width), lambda k, bi: (k, bi, 0, 0)),
                  pl.BlockSpec((1, width, 2 * LANES), lambda k, bi: (k, 0, 0)),
                  pl.BlockSpec((1, 1, LANES), lambda k, bi: (k, 0, 0)),
                  pl.BlockSpec((1, LANES, LANES), lambda k, bi: (k, 0, 0))],
        out_specs=pl.BlockSpec((1, 1, NSA_KV, n, NSA_DH), lambda k, bi: (k, bi, 0, 0, 0)),
        out_shape=jax.ShapeDtypeStruct((2, b, NSA_KV, n, NSA_DH), BF16),
        compiler_params=_params(2), name="compress_prompt")(r, wk, b1, w2)


def _topk_rows(w, k, on_pick):
    n = w.shape[0]
    jidx = lax.broadcasted_iota(I32, w.shape, 0)
    for r in range(k):
        m = jnp.max(w, axis=0, keepdims=True)
        first = jnp.min(jnp.where(w == m, jidx, n), axis=0, keepdims=True)
        pick = jidx == first
        on_pick(r, first, pick)
        w = jnp.where(pick, -jnp.inf, w)


def _nsa_prompt_body(q_ref, g_ref, kc_ref, vc_ref, kst_ref, vs_ref, kwt_ref, vw_ref, mt_ref, es_ref, o_ref):
    ns = mt_ref.shape[0]
    hr = HEAD_ROWS
    n_units, rows = NSA_REP // hr, hr * Q_BLOCK
    blk = pl.program_id(2)
    s0 = blk * Q_BLOCK
    tq = s0 + lax.broadcasted_iota(I32, (Q_BLOCK, 1), 0)
    tcol = jnp.concatenate([tq] * hr, axis=0)
    q = [q_ref[0, u * hr:(u + 1) * hr].reshape(rows, NSA_DH) for u in range(n_units)]

    ncp = kc_ref.shape[-2]

    def compressed(width):
        def run():
            cend = lax.broadcasted_iota(I32, (1, width), 1) * CMP_STRIDE + (CMP_BLOCK - 1)
            outs, psum = [], None
            for u in range(n_units):
                probs = _masked_probs(_dot_nt(q[u], kc_ref[0, 0, 0, :width, :]), cend <= tcol)
                outs.append(_dot(probs.astype(BF16), vc_ref[0, 0, 0, :width, :]))
                for r in range(hr):
                    part = probs[r * Q_BLOCK:(r + 1) * Q_BLOCK]
                    psum = part if psum is None else psum + part
            hi, lo = _split_hi_lo(psum)
            return tuple(outs), _dot_nt(mt_ref[:, :width], hi) + _dot_nt(mt_ref[:, :width], lo)
        return run

    o_c, imp = compressed(ncp)()

    jidx = lax.broadcasted_iota(I32, (ns, Q_BLOCK), 0)
    cur = lax.shift_right_logical(s0 + lax.broadcasted_iota(I32, (ns, Q_BLOCK), 1), 6)
    valid = jidx <= cur
    forced = valid & ((jidx == 0) | (jidx > cur - N_LOCAL))
    w = jnp.where(forced, jnp.inf, jnp.where(valid, imp, -jnp.inf))
    picked = [jnp.zeros((ns, Q_BLOCK), F32)]

    def on_pick(r, first, pick):
        picked[0] = jnp.where(pick, 1.0, picked[0])

    _topk_rows(w, min(SEL_TOPK, ns), on_pick)
    bias = jnp.where(valid, jnp.where(picked[0] > 0.0, 0.0, MASKED), MASKED)
    bias_t = bias.T.astype(BF16)

    bias_rows = jnp.concatenate([bias_t] * hr, axis=0)
    q_aug = [jnp.concatenate([bias_rows, q[u]], axis=1) for u in range(n_units)]
    tiles_per = KV_TILE // LANES

    def keys_aug(kt):
        kt4 = jnp.concatenate([kst_ref[0, 0, kt * tiles_per + j] for j in range(tiles_per)], axis=1)
        return jnp.concatenate([es_ref[kt], kt4], axis=0)

    second = pl.program_id(1) == 1

    def vals_aug(v_both):
        own = (lax.broadcasted_iota(I32, v_both.shape, 1) >= NSA_DH) == second
        return jnp.where(own, v_both, jnp.ones((), BF16))

    def normalised(acc):
        acc = jnp.where(second, pltpu.roll(acc, NSA_DH, 1), acc)
        return acc[:, :NSA_DH] * (1.0 / jnp.maximum(acc[:, NSA_DH:NSA_DH + 1], 1e-30))

    span = WINDOW + Q_BLOCK
    w0 = jnp.maximum(blk - WINDOW // Q_BLOCK, 0)
    start = pl.multiple_of(w0 * Q_BLOCK, Q_BLOCK)
    kw = jnp.concatenate([kwt_ref[0, 0, w0 + j] for j in range(span // LANES)], axis=1)
    vw = vals_aug(vw_ref[0, 0, pl.ds(start, span), :])
    wpos = start + lax.broadcasted_iota(I32, (1, span), 1)
    wmask = (wpos <= tcol) & (wpos > tcol - WINDOW)
    o_w = [normalised(_dot(_masked_exp(_dot(q[u], kw), wmask).astype(BF16), vw)) for u in range(n_units)]

    def update(carry, qa, k_aug, v, mask):
        m, acc = carry
        sc = _dot(qa, k_aug)
        if mask is not None:
            sc = jnp.where(mask, sc, MASKED)
        m_new = jnp.maximum(m, jnp.max(sc, axis=-1, keepdims=True))
        p = jnp.exp(sc - m_new)
        return m_new, jnp.exp(m - m_new) * acc + _dot(p.astype(BF16), v)

    def step(kt, carry):
        k_aug = keys_aug(kt)
        v = vals_aug(vs_ref[0, 0, pl.ds(pl.multiple_of(kt * KV_TILE, KV_TILE), KV_TILE), :])
        return tuple(update(carry[u], q_aug[u], k_aug, v, None) for u in range(n_units))

    one = (jnp.full((rows, 1), -jnp.inf, F32), jnp.zeros((rows, LANES), F32))
    last = (s0 + Q_BLOCK - 1) // KV_TILE
    carry = lax.fori_loop(0, last // 4, lambda p, c: step(4 * p + 3, step(4 * p + 2, step(4 * p + 1, step(4 * p, c)))),
                          (one,) * n_units)
    carry = lax.fori_loop((last // 4) * 4, last, step, carry)
    k0 = pl.multiple_of(last * KV_TILE, KV_TILE)
    causal = k0 + lax.broadcasted_iota(I32, (1, KV_TILE), 1) <= tcol
    k_aug, v = keys_aug(last), vals_aug(vs_ref[0, 0, pl.ds(k0, KV_TILE), :])
    o_s = [normalised(update(carry[u], q_aug[u], k_aug, v, causal)[1]) for u in range(n_units)]

    gates = g_ref[0, 0]
    outs = []
    for u in range(n_units):
        for r in range(hr):
            h, sl = u * hr + r, slice(r * Q_BLOCK, (r + 1) * Q_BLOCK)
            outs.append(gates[:, h:h + 1] * o_c[u][sl] + gates[:, 4 + h:5 + h] * o_s[u][sl] + gates[:, 8 + h:9 + h] * o_w[u][sl])
    o_ref[0] = jnp.concatenate(outs, axis=1).astype(BF16)


def _nsa_prompt(qh, gates, kcv, kt, v, mt, es):
    b, _, t, _ = qh.shape
    ncp = kcv.shape[-2]

    def keys(c):
        return pl.BlockSpec((1, 1, t // LANES, NSA_DH, LANES), lambda bi, g, i: (bi, c + g, 0, 0, 0))

    def vals(kind):
        return pl.BlockSpec((1, 1, t, NSA_KVW), lambda bi, g, i: (bi, kind, 0, 0))

    def cmp(k):
        return pl.BlockSpec((1, 1, 1, ncp, NSA_DH), lambda bi, g, i: (k, bi, g, 0, 0))

    return pl.pallas_call(
        _nsa_prompt_body, grid=(b, NSA_KV, t // Q_BLOCK),
        in_specs=[pl.BlockSpec((1, NSA_REP, Q_BLOCK, NSA_DH), lambda bi, g, i: (bi, g, i, 0)),
                  pl.BlockSpec((1, 1, Q_BLOCK, LANES), lambda bi, g, i: (bi, g, i, 0)),
                  cmp(0), cmp(1), keys(0), vals(0), keys(2), vals(1), _full(mt.shape), _full(es.shape)],
        out_specs=pl.BlockSpec((1, Q_BLOCK, NSA_REP * NSA_DH), lambda bi, g, i: (bi, i, g)),
        out_shape=jax.ShapeDtypeStruct((b, t, NSA_Q), BF16),
        compiler_params=_params(3), name="nsa_prompt")(qh, gates, kcv, kcv, kt, v, kt, v, mt, es)


def _sample_cmp_body(pt_ref, *refs, n_chunks):
    pc = PAGES_PER_STEP
    pages = refs[:pc]
    q_ref, wk_ref, b1_ref, w2_ref, m_ref, perm_ref, oc_ref, imp_ref, rk_ref, ab_ref = refs[pc:]
    c = pl.program_id(1)
    per_page = 128 // CMP_STRIDE

    def by_offset(k, kind):
        x = jnp.concatenate([pages[k + d][0, kind * LANES:(kind + 1) * LANES, :].astype(BF16) for d in range(2)], axis=1)
        return _dot(x, perm_ref[...]).astype(BF16).T

    for kind in range(2):
        for k in range(0, pc, 2):
            pair = by_offset(k, kind)
            for j in range(CMP_STRIDE):
                rk_ref[kind, k * per_page:(k + 2) * per_page, j * LANES:(j + 1) * LANES] = (
                    pair[2 * j * per_page:2 * (j + 1) * per_page])
    row0 = pl.multiple_of(c * (pc * per_page), pc * per_page)
    for kind in range(2):
        ab_ref[kind, pl.ds(row0, pc * per_page), :] = _dot(rk_ref[kind], wk_ref[kind])

    @pl.when(c == n_chunks - 1)
    def _():
        kc = _compress_finish(ab_ref[0], b1_ref[0], w2_ref[0]).astype(BF16)
        vc = _compress_finish(ab_ref[1], b1_ref[1], w2_ref[1]).astype(BF16)
        ncp = kc.shape[0]
        row = lax.broadcasted_iota(I32, (NSA_HEADS, LANES), 0)
        lane = lax.broadcasted_iota(I32, (NSA_HEADS, LANES), 1)
        qbd = jnp.where((lane >= NSA_DH) == (row >= NSA_REP), q_ref[0], jnp.zeros((), BF16))
        s = _dot_nt(qbd, kc)
        n = lax.broadcasted_iota(I32, (1, ncp), 1)
        probs = _masked_probs(s, n * CMP_STRIDE + (CMP_BLOCK - 1) <= ncp * CMP_STRIDE)
        oc_ref[0] = _dot(probs.astype(BF16), vc)
        hi, lo = _split_hi_lo(probs)
        imp8 = _dot(hi, m_ref[...]) + _dot(lo, m_ref[...])
        r8 = lax.broadcasted_iota(I32, imp8.shape, 0)
        imp_ref[0, 0:1, :] = jnp.sum(jnp.where(r8 < NSA_REP, imp8, 0.0), axis=0, keepdims=True)
        imp_ref[0, 1:2, :] = jnp.sum(jnp.where(r8 >= NSA_REP, imp8, 0.0), axis=0, keepdims=True)


def _sample_cmp(pt_flat, cache_t, qrep, wk, b1, w2, m, n_pages):
    nb = qrep.shape[0]
    pc = PAGES_PER_STEP
    n_chunks = n_pages // pc
    ncp = n_pages * 128 // CMP_STRIDE
    ns = m.shape[1]

    def page(k):
        return pl.BlockSpec((1, 2 * LANES, 128), lambda b, c, pt: (pt[b * n_pages + c * pc + k], 0, 0))

    grid_spec = pltpu.PrefetchScalarGridSpec(
        num_scalar_prefetch=1, grid=(nb, n_chunks),
        in_specs=[page(k) for k in range(pc)] + [
            pl.BlockSpec((1, NSA_HEADS, LANES), lambda b, c, pt: (b, 0, 0)),
            pl.BlockSpec(wk.shape, lambda b, c, pt: (0, 0, 0)), pl.BlockSpec(b1.shape, lambda b, c, pt: (0, 0, 0)),
            pl.BlockSpec(w2.shape, lambda b, c, pt: (0, 0, 0)), pl.BlockSpec(m.shape, lambda b, c, pt: (0, 0)),
            pl.BlockSpec((256, 256), lambda b, c, pt: (0, 0))],
        out_specs=[pl.BlockSpec((1, NSA_HEADS, LANES), lambda b, c, pt: (b, 0, 0)),
                   pl.BlockSpec((1, NSA_KV, ns), lambda b, c, pt: (b, 0, 0))],
        scratch_shapes=[pltpu.VMEM((2, pc * 8, CMP_STRIDE * LANES), BF16), pltpu.VMEM((2, ncp, 2 * LANES), F32)])
    pos = np.arange(256)
    perm = np.zeros((256, 256), np.float32)
    perm[pos, (pos % CMP_STRIDE) * 16 + (pos // 128) * 8 + (pos % 128) // CMP_STRIDE] = 1.0
    return pl.pallas_call(
        functools.partial(_sample_cmp_body, n_chunks=n_chunks), grid_spec=grid_spec,
        out_shape=[jax.ShapeDtypeStruct((nb, NSA_HEADS, LANES), F32), jax.ShapeDtypeStruct((nb, NSA_KV, ns), F32)],
        compiler_params=_params(2), name="sample_compress")(
            pt_flat, *([cache_t] * pc), qrep, wk, b1, w2, m, jnp.asarray(perm, BF16))


def _sample_topk_body(imp_ref, idx_ref, *, k):
    ns = imp_ref.shape[1]
    idx_ref[...] = jnp.zeros(idx_ref.shape, I32)
    for g in range(NSA_KV):
        jidx = lax.broadcasted_iota(I32, imp_ref.shape[1:], 0)
        w = jnp.where((jidx == 0) | (jidx > ns - N_LOCAL), jnp.inf, imp_ref[g])

        def on_pick(r, first, pick, g=g):
            idx_ref[g, r:r + 1, :] = first

        _topk_rows(w, k, on_pick)


def _sample_topk(imp_t, k):
    _, ns, nb = imp_t.shape
    return pl.pallas_call(
        functools.partial(_sample_topk_body, k=k), grid=(1,),
        in_specs=[_full(imp_t.shape)], out_specs=_full((NSA_KV, SEL_TOPK, nb)),
        out_shape=jax.ShapeDtypeStruct((NSA_KV, SEL_TOPK, nb), I32),
        compiler_params=_params(1), name="sample_topk")(imp_t)


def _attend_with_self(qg, keys_t, vals_t, k_self, v_self, mask):
    s = _dot(qg, keys_t)
    if mask is not None:
        s = jnp.where(mask, s, -jnp.inf)
    s_self = jnp.sum(qg.astype(F32) * k_self.astype(BF16).astype(F32), axis=-1, keepdims=True)
    m = jnp.maximum(jnp.max(s, axis=-1, keepdims=True), s_self)
    p = jnp.exp(s - m)
    p_self = jnp.exp(s_self - m)
    den = jnp.sum(p, axis=-1, keepdims=True) + p_self
    num = _dot_nt(p.astype(BF16), vals_t) + p_self.astype(BF16).astype(F32) * v_self.astype(BF16).astype(F32)
    return num / den


def _sample_attn_body(idx_ref, pt_ref, *refs, n_sel):
    n_blk = NSA_KV * n_sel
    blocks = refs[:n_blk]
    (win_ref, q_ref, kvn_ref, gate_ref, oc_ref, mq_ref, mem_ref,
     ob_ref, om_ref, wn_ref, ks_ref, vs_ref) = refs[n_blk:]
    b = pl.program_id(0)
    row = lax.broadcasted_iota(I32, (NSA_HEADS, NSA_DH), 0)
    q8 = q_ref[0][:, :NSA_DH]
    qg = [jnp.where((row >= NSA_REP) == (g == 1), q8, jnp.zeros((), BF16)) for g in range(NSA_KV)]
    kvn = kvn_ref[0]

    def new_row(kind, g):
        lo = kind * NSA_KVW + g * NSA_DH
        return kvn[:, lo:lo + NSA_DH]

    lane = lax.broadcasted_iota(I32, (1, n_sel * LANES), 1)
    upper = (lane & SEL_BLOCK) != 0
    slot = lax.shift_right_logical(lane, 7)
    o_sel = []
    for g in range(NSA_KV):
        want_upper = jnp.zeros((1, n_sel * LANES), I32)
        for k in range(n_sel):
            blk = blocks[g * n_sel + k]
            ks_ref[g, :, k * LANES:(k + 1) * LANES] = blk[0, g * NSA_DH:(g + 1) * NSA_DH, :].astype(BF16)
            vs_ref[g, :, k * LANES:(k + 1) * LANES] = blk[0, NSA_KVW + g * NSA_DH:NSA_KVW + (g + 1) * NSA_DH, :].astype(BF16)
            j = idx_ref[b * (NSA_KV * SEL_TOPK) + g * SEL_TOPK + k]
            want_upper = jnp.where(slot == k, j & 1, want_upper)
        o_sel.append(_attend_with_self(qg[g], ks_ref[g], vs_ref[g], new_row(2, g), new_row(3, g),
                                       upper == (want_upper != 0)))
    o_s = jnp.where(row < NSA_REP, o_sel[0], o_sel[1])

    win = win_ref[0]
    wbuf = win.shape[1]
    wcol = lax.broadcasted_iota(I32, (1, wbuf), 1)
    o_win = []
    for g in range(NSA_KV):
        kt = win[g * NSA_DH:(g + 1) * NSA_DH].astype(BF16)
        vt = win[NSA_KVW + g * NSA_DH:NSA_KVW + (g + 1) * NSA_DH].astype(BF16)
        o_win.append(_attend_with_self(qg[g], kt, vt, new_row(4, g), new_row(5, g), wcol >= 1))
    o_w = jnp.where(row < NSA_REP, o_win[0], o_win[1])
    rows_w = 2 * NSA_KVW
    eye = lax.broadcasted_iota(I32, (rows_w, rows_w), 0) == lax.broadcasted_iota(I32, (rows_w, rows_w), 1)
    new_col = jnp.sum(jnp.where(eye, kvn[:, 4 * NSA_KVW:], 0.0), axis=-1, keepdims=True)
    wn_ref[0] = jnp.where(lax.broadcasted_iota(I32, win.shape, 1) == wbuf - 1, new_col, pltpu.roll(win, wbuf - 1, 1))

    gate = gate_ref[0]
    grow = lax.broadcasted_iota(I32, (NSA_HEADS, LANES), 0)
    glane = lax.broadcasted_iota(I32, (NSA_HEADS, LANES), 1)
    oc = jnp.where((glane >= NSA_DH) == (grow >= NSA_REP), oc_ref[0], 0.0)
    o = jnp.zeros((NSA_HEADS, NSA_DH), F32)
    for j, branch in enumerate((oc[:, :NSA_DH] + oc[:, NSA_DH:], o_s, o_w)):
        gj = jnp.sum(jnp.where(glane == j * NSA_HEADS + grow, gate, 0.0), axis=-1, keepdims=True)
        o = o + gj * branch
    ob_ref[0] = o

    mq = mq_ref[0]
    per_tok = 2 * MEM_HEADS
    for h in range(MEM_HEADS):
        qh = jnp.broadcast_to(mq[:, h * MEM_DH:(h + 1) * MEM_DH], (NSA_HEADS, MEM_DH)).astype(BF16)
        km = mem_ref[0, pl.ds(h, MEM_TOKENS, stride=per_tok), :].astype(BF16)
        vm = mem_ref[0, pl.ds(MEM_HEADS + h, MEM_TOKENS, stride=per_tok), :].astype(BF16)
        p = _softmax(_dot_nt(qh, km) * (MEM_DH ** -0.5))
        om_ref[0, :, h * MEM_DH:(h + 1) * MEM_DH] = _dot(p.astype(BF16), vm)[0:1]


def _sample_attn(idx_flat, pt_flat, cache_t, win_t, qrep, kvn, gate, oc, mq, memkv, n_pages, n_sel):
    nb, rows_w, wbuf = win_t.shape

    def sel_block(g, k):
        def index(b, idx, pt):
            j = idx[b * (NSA_KV * SEL_TOPK) + g * SEL_TOPK + k]
            return (pt[b * n_pages + lax.shift_right_logical(j, 1)], 1, 0)
        return pl.BlockSpec((1, 2 * NSA_KVW, 128), index)

    def per_seq(shape):
        return pl.BlockSpec((1,) + shape, lambda b, idx, pt: (b,) + (0,) * len(shape))

    grid_spec = pltpu.PrefetchScalarGridSpec(
        num_scalar_prefetch=2, grid=(nb,),
        in_specs=[sel_block(g, k) for g in range(NSA_KV) for k in range(n_sel)] + [
            per_seq((rows_w, wbuf)), per_seq((NSA_HEADS, LANES)), per_seq((1, 768)), per_seq((1, LANES)),
            per_seq((NSA_HEADS, LANES)), per_seq((1, MEM_Q)), per_seq(memkv.shape[1:])],
        out_specs=[per_seq((NSA_HEADS, NSA_DH)), per_seq((1, MEM_Q)), per_seq((rows_w, wbuf))],
        scratch_shapes=[pltpu.VMEM((NSA_KV, NSA_DH, n_sel * LANES), BF16)] * 2)
    return pl.pallas_call(
        functools.partial(_sample_attn_body, n_sel=n_sel), grid_spec=grid_spec,
        out_shape=[jax.ShapeDtypeStruct((nb, NSA_HEADS, NSA_DH), F32), jax.ShapeDtypeStruct((nb, 1, MEM_Q), F32),
                   jax.ShapeDtypeStruct((nb, rows_w, wbuf), F32)],
        compiler_params=_params(1), name="sample_attn")(
            idx_flat, pt_flat, *([cache_t] * (NSA_KV * n_sel)), win_t, qrep, kvn, gate, oc, mq, memkv)


def _merge_body(x_ref, oa_ref, ob_ref, om_ref, g1_ref, wmg_ref, wb_ref, wo_ref, g2_ref, rwh_ref, rwl_ref, rb_ref,
                h1_ref, hn_ref, gt_ref):
    x = x_ref[...]
    ab = _rms(x, g1_ref[...]).astype(BF16)
    s = None
    for k, o_ref in enumerate((oa_ref, ob_ref, om_ref)):
        gk = jax.nn.sigmoid(_dot(ab, wmg_ref[:, k * D_MODEL:(k + 1) * D_MODEL]))
        term = gk * _dot(o_ref[...], wb_ref[k])
        s = term if s is None else s + term
    h1 = x + _dot(s.astype(BF16), wo_ref[...])
    h1_ref[...] = h1
    hn = _rms(h1, g2_ref[...])
    hi, lo = _split_hi_lo(hn)
    hn_ref[...] = hi

    logit = _dot_nt(rwh_ref[...], hi) + _dot_nt(rwh_ref[...], lo) + _dot_nt(rwl_ref[...], hi) + rb_ref[...]
    picked = [jnp.zeros(logit.shape, F32)]

    def on_pick(r, first, pick):
        picked[0] = jnp.where(pick, 1.0, picked[0])

    _topk_rows(logit, TOP_K, on_pick)
    top = jnp.max(logit, axis=0, keepdims=True)
    ex = jnp.where(picked[0] > 0.0, jnp.exp(logit - top), 0.0)
    gt_ref[...] = ex / jnp.sum(ex, axis=0, keepdims=True)


def _merge(x, oa, ob, om, g1, wmg, wb, wo, g2, rwh, rwl, rb, tm):
    n = x.shape[0]

    def row(width):
        return pl.BlockSpec((tm, width), lambda i: (i, 0))

    return pl.pallas_call(
        _merge_body, grid=(n // tm,),
        in_specs=[row(D_MODEL), row(512), row(512), row(512), _full((1, D_MODEL)), _full(wmg.shape), _full(wb.shape),
                  _full(wo.shape), _full((1, D_MODEL)), _full(rwh.shape), _full(rwl.shape), _full(rb.shape)],
        out_specs=[row(D_MODEL), row(D_MODEL), pl.BlockSpec((N_EXPERTS, tm), lambda i: (0, i))],
        out_shape=[jax.ShapeDtypeStruct((n, D_MODEL), F32), jax.ShapeDtypeStruct((n, D_MODEL), BF16),
                   jax.ShapeDtypeStruct((N_EXPERTS, n), F32)],
        compiler_params=_params(1), name="merge")(x, oa, ob, om, g1, wmg, wb, wo, g2, rwh, rwl, rb)


def _moe_body(cnt_ref, hn_ref, h1_ref, gt_ref, u_ref, wgu_ref, bgu_ref, wd_ref, bd_ref, fg_ref, y_ref,
              rank_ref, *, sub, cap):
    i, e = pl.program_id(0), pl.program_id(1)
    n_sub = hn_ref.shape[0] // sub

    @pl.when(e == 0)
    def _():
        for s in range(n_sub):
            cols = slice(s * sub, (s + 1) * sub)
            sel = jnp.where(gt_ref[:, cols] > 0.0, 1.0, 0.0).astype(BF16)
            rank_ref[:, cols] = _dot(sel, u_ref[...])
        y_ref[...] = h1_ref[...]

    slot = lax.broadcasted_iota(I32, (cap, sub), 0).astype(F32)
    spans = [slice(s * sub, (s + 1) * sub) for s in range(n_sub)]
    g_rows = [gt_ref[pl.ds(e, 1), sp] for sp in spans]
    r_rows = [rank_ref[pl.ds(e, 1), sp] for sp in spans]

    def chunk(ch, carry):
        want = slot + (ch * cap).astype(F32)
        hits = [(g_rows[s] > 0.0) & (r_rows[s] == want) for s in range(n_sub)]
        onehots = [jnp.where(h, 1.0, 0.0).astype(BF16) for h in hits]
        xe = jnp.concatenate([_dot(onehots[s], hn_ref[spans[s], :]).astype(BF16) for s in range(n_sub)], axis=0)
        gu = _dot(xe, wgu_ref[0]) + bgu_ref[0]
        gate = jnp.minimum(gu[:, :D_EXPERT], SWIGLU_LIMIT)
        up = jnp.clip(gu[:, D_EXPERT:], -SWIGLU_LIMIT, SWIGLU_LIMIT)
        act = (up + 1.0) * gate * jax.nn.sigmoid(SWIGLU_ALPHA * gate)
        ye = _dot(act.astype(BF16), wd_ref[0]) + bd_ref[0]
        for s in range(n_sub):
            w = jnp.sum(jnp.where(hits[s], g_rows[s], 0.0), axis=-1, keepdims=True)
            y_ref[spans[s], :] += _dot_tn(onehots[s], (ye[s * cap:(s + 1) * cap] * w).astype(BF16))
        return carry

    most = cnt_ref[i * n_sub * N_EXPERTS + e]
    for s in range(1, n_sub):
        most = jnp.maximum(most, cnt_ref[(i * n_sub + s) * N_EXPERTS + e])
    lax.fori_loop(0, (most + cap - 1) // cap, chunk, 0)

    @pl.when(e == N_EXPERTS - 1)
    def _():
        y_ref[...] = _rms(y_ref[...], fg_ref[...])


def _moe(counts, hn, h1, gate_t, upper, wgu, bgu, wd, bd, fg, tile, sub, cap):
    n = hn.shape[0]
    once = pl.Buffered(1)
    grid_spec = pltpu.PrefetchScalarGridSpec(
        num_scalar_prefetch=1, grid=(n // tile, N_EXPERTS),
        in_specs=[pl.BlockSpec((tile, D_MODEL), lambda i, e, c: (i, 0)),
                  pl.BlockSpec((tile, D_MODEL), lambda i, e, c: (i, 0), pipeline_mode=once),
                  pl.BlockSpec((N_EXPERTS, tile), lambda i, e, c: (0, i)),
                  pl.BlockSpec((sub, sub), lambda i, e, c: (0, 0), pipeline_mode=once),
                  pl.BlockSpec((1, D_MODEL, 2 * D_EXPERT), lambda i, e, c: (e, 0, 0)),
                  pl.BlockSpec((1, 1, 2 * D_EXPERT), lambda i, e, c: (e, 0, 0)),
                  pl.BlockSpec((1, D_EXPERT, D_MODEL), lambda i, e, c: (e, 0, 0)),
                  pl.BlockSpec((1, 1, D_MODEL), lambda i, e, c: (e, 0, 0)),
                  pl.BlockSpec((1, D_MODEL), lambda i, e, c: (0, 0))],
        out_specs=pl.BlockSpec((tile, D_MODEL), lambda i, e, c: (i, 0)),
        scratch_shapes=[pltpu.VMEM((N_EXPERTS, tile), F32)])
    return pl.pallas_call(
        functools.partial(_moe_body, sub=sub, cap=cap), grid_spec=grid_spec,
        out_shape=jax.ShapeDtypeStruct((n, D_MODEL), F32),
        compiler_params=_params(2), name="moe")(counts, hn, h1, gate_t, upper, wgu, bgu, wd, bd, fg)


def _gate_columns(order):
    idx = np.zeros((LANES,), np.int32)
    ok = np.zeros((LANES,), bool)
    for c, (h, j) in enumerate(order):
        idx[c], ok[c] = C_NG + h * 3 + j, True
    return idx, ok


def _prep_in_weights(w_in):
    def sect(lo, hi):
        return w_in[:, lo:hi]

    def gates(order):
        idx, ok = _gate_columns(order)
        return jnp.where(jnp.asarray(ok)[None, :], w_in[:, idx], 0.0)

    w_q = sect(C_Q, C_KV) * (NSA_DH ** -0.5)
    g_prompt = [gates([(NSA_REP * g + r, j) for j in range(3) for r in range(NSA_REP)]) for g in range(NSA_KV)]
    w_prompt = jnp.concatenate([sect(C_U, C_Q), w_q, sect(C_KV, C_NG)] + g_prompt + [sect(C_MQ, C_MG)], axis=1)
    q_rep = jnp.concatenate([w_q.reshape(D_MODEL, NSA_HEADS, 1, NSA_DH)] * 2, axis=2).reshape(D_MODEL, 2 * NSA_Q)
    g_sample = gates([(h, j) for j in range(3) for h in range(NSA_HEADS)])
    w_sample = jnp.concatenate([sect(C_U, C_Q), q_rep, sect(C_KV, C_NG), g_sample, sect(C_MQ, C_MG)], axis=1)
    w_kv_t = w_in.T[C_KV:C_NG]
    return w_prompt.astype(BF16), w_sample.astype(BF16), sect(C_MG, w_in.shape[1]).astype(BF16), w_kv_t.astype(BF16)


def _prep_compress(w1, b1, w2):
    eye = jnp.eye(NSA_KV, dtype=F32)
    w = w1.reshape(2, 2, CMP_STRIDE, NSA_DH, NSA_DH)
    wk = jnp.einsum('khjde,xg->kjxdhge', w, eye).reshape(2, CMP_STRIDE * LANES, 2 * LANES)
    b = jnp.concatenate([b1, b1], axis=-1).reshape(2, 1, LANES)
    w2d = jnp.einsum('kde,xg->kxdge', w2, eye).reshape(2, LANES, LANES)
    return wk.astype(BF16), b, w2d.astype(BF16)


def _cmp_to_sel_t(ncp, ns):
    cs = np.arange(ncp) * CMP_STRIDE
    ss = np.arange(ns) * SEL_BLOCK
    m = (cs[None, :] <= ss[:, None] + SEL_BLOCK - 1) & (cs[None, :] + CMP_BLOCK - 1 >= ss[:, None])
    m[:, ncp - 1] = False
    return m.astype(np.float32)


def _block_onehot(t):
    e = (np.arange(t // SEL_BLOCK)[:, None] == (np.arange(t) // SEL_BLOCK)[None, :]).astype(np.float32)
    return e.reshape(t // SEL_BLOCK, t // KV_TILE, KV_TILE).transpose(1, 0, 2)


def _moe_group(h1_and_gates, weights, tile, sub, cap):
    h1, hn, gate_t = h1_and_gates
    n = h1.shape[0]
    counts = jnp.sum((gate_t > 0.0).reshape(N_EXPERTS, n // sub, sub), axis=-1, dtype=I32).T.reshape(-1)
    upper = jnp.asarray(np.triu(np.ones((sub, sub), np.float32), 1), BF16)
    return _moe(counts, hn, h1, gate_t, upper, *weights, tile, sub, cap)


def kernel(x_prompt, x_sample, mem_prompt, cache_nsa_kv, cache_nsa_win, cache_mem_kv, page_table,
           norm1_g, w_in, sgu_ln_g, sgu_ln_b, sgu_w, sgu_b, cmp_w1, cmp_b1, cmp_w2, w_mem_kv,
           w_branch, w_out, norm2_g, router_w, router_b, exp_w_gate_up, exp_b_gate_up,
           exp_w_down, exp_b_down, final_norm_g):
    assert norm1_g.shape[0] == 1, "one layer"
    bp, t, _ = x_prompt.shape
    nb = x_sample.shape[0]
    n_pool, page = cache_nsa_kv.shape[1:3]
    n_pages = page_table.shape[1]
    past = n_pages * page
    wbuf = cache_nsa_win.shape[2]
    assert page == 128 and wbuf == WINDOW and n_pages % PAGES_PER_STEP == 0 and t % KV_TILE == 0 and t >= WINDOW + Q_BLOCK

    g1, g2, fg = norm1_g[0][None], norm2_g[0][None], final_norm_g[None]
    lng, lnb = sgu_ln_g[0][None], sgu_ln_b[0][None]
    w_prompt, w_sample, w_mg, w_kv_t = _prep_in_weights(w_in[0])
    wk, b1c, w2c = _prep_compress(cmp_w1[0], cmp_b1[0], cmp_w2[0])
    wb, wo = w_branch[0].astype(BF16), w_out[0].astype(BF16)
    rwh, rwl = _split_hi_lo(router_w[0].T)
    rb = router_b[0][:, None]
    moe_w = (exp_w_gate_up[0].astype(BF16), exp_b_gate_up[0][:, None, :], exp_w_down[0].astype(BF16),
             exp_b_down[0][:, None, :], fg)
    merge_w = (g1, w_mg, wb, wo, g2, rwh, rwl, rb)

    mkv, mkv_bf = _mem_kv(mem_prompt.reshape(bp * MEM_TOKENS, D_MODEL), w_mem_kv[0].astype(BF16))
    oa, qh, kv4t, kvwt, kt, v, kcv_rows, gates, om = _inproj_prompt(
        x_prompt, g1, w_prompt, w_kv_t, lng, lnb, sgu_w[0], sgu_b[0].T, mkv_bf.reshape(bp, MEM_TOKENS, 2 * MEM_Q))
    ncp, ns = t // CMP_STRIDE, t // SEL_BLOCK
    r = jnp.stack([kcv_rows[:, :, k * LANES:(k + 1) * LANES].reshape(bp, ncp, CMP_STRIDE * LANES) for k in range(2)])
    kcv = _compress_prompt(r, wk, b1c, w2c)
    mt = jnp.asarray(_cmp_to_sel_t(ncp, ns), BF16)
    ob = _nsa_prompt(qh, gates, kcv, kt, v, mt, jnp.asarray(_block_onehot(t), BF16))
    n = bp * t
    x2 = x_prompt.reshape(n, D_MODEL)
    y_prompt = _moe_group(_merge(x2, oa.reshape(n, 512), ob.reshape(n, 512), om.reshape(n, 512), *merge_w, tm=min(512, n)),
                          moe_w, min(MOE_BLOCK, n), min(MOE_TILE, n), MOE_CAP)

    xs = x_sample.reshape(nb, D_MODEL)
    wsc = jnp.repeat(sgu_w[0][:, 0, 0], SGU_GROUP_DIM)[None]
    bsc = jnp.repeat(sgu_b[0][:, 0], SGU_GROUP_DIM)[None]
    oa_s, z2_s, q_s, kv_s, gate_s, mq_s = _inproj_sample(xs, g1, w_sample, lng, lnb, wsc, bsc)
    qrep = q_s.reshape(nb, NSA_HEADS, LANES)
    pt_flat = page_table.reshape(-1)
    ncp_s, ns_s = past // CMP_STRIDE, past // SEL_BLOCK
    m_s = jnp.asarray(_cmp_to_sel_t(ncp_s, ns_s).T, BF16)
    cache_t = jnp.transpose(cache_nsa_kv[0], (0, 2, 3, 4, 1)).reshape(n_pool, 4 * NSA_KVW, page)
    win_t = jnp.transpose(cache_nsa_win[0], (0, 2, 3, 4, 1)).reshape(nb, 2 * NSA_KVW, wbuf)
    oc_s, imp_s = _sample_cmp(pt_flat, cache_t, qrep, wk, b1c, w2c, m_s, n_pages)
    n_sel = min(SEL_TOPK, ns_s + 1) - 1
    idx = _sample_topk(imp_s.transpose(1, 2, 0), n_sel)
    idx_flat = idx.transpose(2, 0, 1).reshape(-1)
    ob_s, om_s, win_new = _sample_attn(
        idx_flat, pt_flat, cache_t, win_t, qrep, kv_s[:, None, :], gate_s[:, None, :], oc_s, mq_s[:, None, :],
        cache_mem_kv[0].reshape(nb, MEM_TOKENS * 2 * MEM_HEADS, MEM_DH), n_pages, n_sel)
    y_sample = _moe_group(_merge(xs, oa_s, ob_s.reshape(nb, NSA_Q).astype(BF16), om_s.reshape(nb, MEM_Q).astype(BF16),
                                 *merge_w, tm=nb), moe_w, nb, nb, 32)

    def positions_last(x_t, n_kinds):
        lead, _, npos = x_t.shape
        return jnp.transpose(x_t.reshape(lead, n_kinds, NSA_KV, NSA_DH, npos), (0, 4, 1, 2, 3))[None]

    return (y_prompt.reshape(bp, t, D_MODEL), y_sample.reshape(nb, 1, D_MODEL),
            positions_last(kv4t, 4), positions_last(kvwt[:, :, t - WINDOW:], 2),
            mkv.reshape(1, bp, MEM_TOKENS, 2, MEM_HEADS, MEM_DH), kv_s[:, :512].reshape(1, nb, 1, 4, NSA_KV, NSA_DH),
            positions_last(win_new, 2), z2_s.reshape(1, nb, 1, SGU_DIM))
```

```python
import functools

import numpy as np
import jax
import jax.numpy as jnp
from jax import lax
from jax.experimental import pallas as pl
from jax.experimental.pallas import tpu as pltpu

F32, BF16, I32 = jnp.float32, jnp.bfloat16, jnp.int32

D_MODEL = 1024
SGU_GROUPS, SGU_GROUP_DIM, SGU_DIM, CHUNK = 4, 128, 512, 128
NSA_HEADS, NSA_KV, NSA_REP, NSA_DH = 8, 2, 4, 64
CMP_BLOCK, CMP_STRIDE, SEL_BLOCK, SEL_TOPK, N_LOCAL, WINDOW, Q_BLOCK = 32, 16, 64, 16, 2, 512, 128
MEM_TOKENS, MEM_HEADS, MEM_DH = 256, 4, 128
N_EXPERTS, TOP_K, D_EXPERT = 32, 4, 1024
SWIGLU_LIMIT, SWIGLU_ALPHA, EPS = 7.0, 1.702, 1e-5
NSA_Q, NSA_KVW, MEM_Q = NSA_HEADS * NSA_DH, NSA_KV * NSA_DH, MEM_HEADS * MEM_DH
C_U, C_V, C_Q, C_KV, C_NG, C_MQ, C_MG = 0, 512, 1024, 1536, 2304, 2328, 2840

LANES = 128
VMEM_LIMIT = 56 * 1024 * 1024
MASKED = -1e9
KV_TILE = 512
PAGES_PER_STEP = 64
MERGE_ROWS = 1024
MOE_TILE, MOE_CAP = 1024, 160
MOE_BLOCK = 2048


def _dot(a, b):
    return jnp.dot(a, b, preferred_element_type=F32)


def _dot_nt(a, b):
    return lax.dot_general(a, b, (((1,), (1,)), ((), ())), preferred_element_type=F32)


def _dot_tn(a, b):
    return lax.dot_general(a, b, (((0,), (0,)), ((), ())), preferred_element_type=F32)


def _rms(x, g):
    return x * lax.rsqrt(jnp.mean(x * x, axis=-1, keepdims=True) + EPS) * g


def _layernorm(x, g, b):
    mu = jnp.mean(x, axis=-1, keepdims=True)
    xc = x - mu
    return xc * lax.rsqrt(jnp.mean(xc * xc, axis=-1, keepdims=True) + EPS) * g + b


def _masked_exp(s, mask):
    s = jnp.where(mask, s, -jnp.inf)
    m = jnp.max(s, axis=-1, keepdims=True)
    m = jnp.where(jnp.isfinite(m), m, 0.0)
    return jnp.exp(s - m)


def _masked_probs(s, mask):
    p = _masked_exp(s, mask)
    return p * (1.0 / jnp.maximum(jnp.sum(p, axis=-1, keepdims=True), 1e-30))


def _softmax(s):
    p = jnp.exp(s - jnp.max(s, axis=-1, keepdims=True))
    return p / jnp.sum(p, axis=-1, keepdims=True)


def _split_hi_lo(x):
    hi = x.astype(BF16)
    return hi, (x - hi.astype(F32)).astype(BF16)


def _params(n_axes):
    return pltpu.CompilerParams(dimension_semantics=("arbitrary",) * n_axes, vmem_limit_bytes=VMEM_LIMIT)


def _full(shape):
    return pl.BlockSpec(shape, lambda *_: (0,) * len(shape))


def _const(shape):
    return pl.BlockSpec(shape, lambda *_: (0,) * len(shape), pipeline_mode=pl.Buffered(1))


def _mkv_body(x_ref, w_ref, o_ref, ob_ref):
    r = _dot(x_ref[...].astype(BF16), w_ref[...])
    o_ref[...] = r
    ob_ref[...] = r.astype(BF16)


def _mem_kv(mem2d, w_bf):
    rows, tm = mem2d.shape[0], MEM_TOKENS
    return pl.pallas_call(
        _mkv_body, grid=(rows // tm,),
        in_specs=[pl.BlockSpec((tm, D_MODEL), lambda i: (i, 0)), _full((D_MODEL, 2 * MEM_Q))],
        out_specs=[pl.BlockSpec((tm, 2 * MEM_Q), lambda i: (i, 0))] * 2,
        out_shape=[jax.ShapeDtypeStruct((rows, 2 * MEM_Q), F32), jax.ShapeDtypeStruct((rows, 2 * MEM_Q), BF16)],
        compiler_params=_params(1), name="mem_kv")(mem2d, w_bf)


def _inproj_prompt_body(x_ref, g_ref, w_ref, wkvt_ref, lng_ref, lnb_ref, ws_ref, bst_ref, mkv_ref,
                        oa_ref, qh_ref, kv4t_ref, kvwt_ref, kt_ref, v_ref, kcv_ref, gate_ref, om_ref, *, tm):
    ab = _rms(x_ref[0], g_ref[...]).astype(BF16)

    def proj(lo, hi):
        return _dot(ab, w_ref[:, lo:hi])

    z1 = jax.nn.gelu(proj(0, 512))
    z2 = _layernorm(jax.nn.gelu(proj(512, 1024)), lng_ref[...], lnb_ref[...])
    tril = lax.broadcasted_iota(I32, (CHUNK, CHUNK), 0) >= lax.broadcasted_iota(I32, (CHUNK, CHUNK), 1)
    for g in range(SGU_GROUPS):
        cols = slice(g * SGU_GROUP_DIM, (g + 1) * SGU_GROUP_DIM)
        w = jnp.where(tril, ws_ref[g], 0.0).astype(BF16)
        for c in range(tm // CHUNK):
            rows = slice(c * CHUNK, (c + 1) * CHUNK)
            mixed = _dot(w, z2[rows, cols].astype(BF16)) + bst_ref[:, g:g + 1]
            oa_ref[0, rows, cols] = (z1[rows, cols] * mixed).astype(BF16)

    q = proj(1024, 1536)
    for h in range(NSA_HEADS):
        qh_ref[0, h] = q[:, h * NSA_DH:(h + 1) * NSA_DH].astype(BF16)

    kvt = _dot_nt(wkvt_ref[...], ab)
    kv4t_ref[0] = kvt[:512]
    kvwt_ref[0] = kvt[512:]
    for c, r0 in enumerate((256, 320, 512, 576)):
        for u in range(tm // LANES):
            kt_ref[0, c, u] = kvt[r0:r0 + NSA_DH, u * LANES:(u + 1) * LANES].astype(BF16)
    kcv_ref[0] = proj(1536, 1792).astype(BF16)
    v_ref[0, 0] = proj(1920, 2048).astype(BF16)
    v_ref[0, 1] = proj(2176, 2304).astype(BF16)

    gate_ref[0, 0] = jax.nn.sigmoid(proj(2304, 2432))
    gate_ref[0, 1] = jax.nn.sigmoid(proj(2432, 2560))

    mq = proj(2560, 3072).astype(BF16)
    for h in range(MEM_HEADS):
        cols = slice(h * MEM_DH, (h + 1) * MEM_DH)
        p = _softmax(_dot_nt(mq[:, cols], mkv_ref[0, :, cols]) * (MEM_DH ** -0.5))
        om_ref[0, :, cols] = _dot(p.astype(BF16), mkv_ref[0, :, MEM_Q + h * MEM_DH:MEM_Q + (h + 1) * MEM_DH]).astype(BF16)


def _inproj_prompt(x, g1, w, wkvt, lng, lnb, ws, bst, mkv_bf):
    b, t, _ = x.shape
    tm = min(512, t)

    def row(width):
        return pl.BlockSpec((1, tm, width), lambda bi, i: (bi, i, 0))

    def col(height):
        return pl.BlockSpec((1, height, tm), lambda bi, i: (bi, 0, i))

    def heads(n):
        return pl.BlockSpec((1, n, tm, NSA_DH), lambda bi, i: (bi, 0, i, 0))

    return pl.pallas_call(
        functools.partial(_inproj_prompt_body, tm=tm), grid=(b, t // tm),
        in_specs=[row(D_MODEL), _full((1, D_MODEL)), _full(w.shape), _full(wkvt.shape),
                  _full((1, SGU_DIM)), _full((1, SGU_DIM)),
                  _full((SGU_GROUPS, CHUNK, CHUNK)), _full((CHUNK, SGU_GROUPS)),
                  pl.BlockSpec((1, MEM_TOKENS, 2 * MEM_Q), lambda bi, i: (bi, 0, 0))],
        out_specs=[row(SGU_DIM), heads(NSA_HEADS), col(512), col(256),
                   pl.BlockSpec((1, 4, tm // LANES, NSA_DH, LANES), lambda bi, i: (bi, 0, i, 0, 0)),
                   pl.BlockSpec((1, 2, tm, NSA_KVW), lambda bi, i: (bi, 0, i, 0)), row(256),
                   pl.BlockSpec((1, 2, tm, LANES), lambda bi, i: (bi, 0, i, 0)), row(MEM_Q)],
        out_shape=[jax.ShapeDtypeStruct((b, t, SGU_DIM), BF16), jax.ShapeDtypeStruct((b, NSA_HEADS, t, NSA_DH), BF16),
                   jax.ShapeDtypeStruct((b, 512, t), F32), jax.ShapeDtypeStruct((b, 256, t), F32),
                   jax.ShapeDtypeStruct((b, 4, t // LANES, NSA_DH, LANES), BF16),
                   jax.ShapeDtypeStruct((b, 2, t, NSA_KVW), BF16), jax.ShapeDtypeStruct((b, t, 256), BF16),
                   jax.ShapeDtypeStruct((b, 2, t, LANES), F32), jax.ShapeDtypeStruct((b, t, MEM_Q), BF16)],
        compiler_params=_params(2), name="inproj_prompt")(x, g1, w, wkvt, lng, lnb, ws, bst, mkv_bf)


def _inproj_sample_body(x_ref, g_ref, w_ref, lng_ref, lnb_ref, wsc_ref, bsc_ref,
                        oa_ref, z2_ref, q_ref, kv_ref, gate_ref, mq_ref):
    ab = _rms(x_ref[...], g_ref[...]).astype(BF16)

    def proj(lo, hi):
        return _dot(ab, w_ref[:, lo:hi])

    z1 = jax.nn.gelu(proj(0, 512))
    z2 = _layernorm(jax.nn.gelu(proj(512, 1024)), lng_ref[...], lnb_ref[...])
    z2_ref[...] = z2
    oa_ref[...] = (z1 * (wsc_ref[...] * z2 + bsc_ref[...])).astype(BF16)
    q_ref[...] = proj(1024, 2048).astype(BF16)
    kv_ref[...] = proj(2048, 2816)
    gate_ref[...] = jax.nn.sigmoid(proj(2816, 2944))
    mq_ref[...] = proj(2944, 3456)


def _inproj_sample(x, g1, w, lng, lnb, wsc, bsc):
    n = x.shape[0]
    shapes = [((n, SGU_DIM), BF16), ((n, SGU_DIM), F32), ((n, 2 * NSA_Q), BF16), ((n, 768), F32),
              ((n, LANES), F32), ((n, MEM_Q), F32)]
    return pl.pallas_call(
        _inproj_sample_body, grid=(1,),
        in_specs=[_full(x.shape), _full((1, D_MODEL)), _full(w.shape), _full((1, SGU_DIM)), _full((1, SGU_DIM)),
                  _full((1, SGU_DIM)), _full((1, SGU_DIM))],
        out_specs=[_full(s) for s, _ in shapes],
        out_shape=[jax.ShapeDtypeStruct(s, d) for s, d in shapes],
        compiler_params=_params(1), name="inproj_sample")(x, g1, w, lng, lnb, wsc, bsc)


def _compress_finish(ab, b1, w2):
    n = ab.shape[0]
    c = ab[:, :LANES] + pltpu.roll(ab[:, LANES:], n - 1, 0) + b1
    return _dot(jax.nn.gelu(c).astype(BF16), w2)


def _compress_prompt_body(r_ref, wk_ref, b1_ref, w2_ref, o_ref):
    out = _compress_finish(_dot(r_ref[0, 0], wk_ref[0]), b1_ref[0], w2_ref[0])
    for g in range(NSA_KV):
        o_ref[0, 0, g] = out[:, g * NSA_DH:(g + 1) * NSA_DH].astype(BF16)


def _compress_prompt(r, wk, b1, w2):
    _, b, n, width = r.shape
    return pl.pallas_call(
        _compress_prompt_body, grid=(2, b),
        in_specs=[pl.BlockSpec((1, 1, n, width), lambda k, bi: (k, bi, 0, 0)),
                  pl.BlockSpec((1, width, 2 * LANES), lambda k, bi: (k, 0, 0)),
                  pl.BlockSpec((1, 1, LANES), lambda k, bi: (k, 0, 0)),
                  pl.BlockSpec((1, LANES, LANES), lambda k, bi: (k, 0, 0))],
        out_specs=pl.BlockSpec((1, 1, NSA_KV, n, NSA_DH), lambda k, bi: (k, bi, 0, 0, 0)),
        out_shape=jax.ShapeDtypeStruct((2, b, NSA_KV, n, NSA_DH), BF16),
        compiler_params=_params(2), name="compress_prompt")(r, wk, b1, w2)


def _topk_rows(w, k, on_pick):
    n = w.shape[0]
    jidx = lax.broadcasted_iota(I32, w.shape, 0)
    for r in range(k):
        m = jnp.max(w, axis=0, keepdims=True)
        first = jnp.min(jnp.where(w == m, jidx, n), axis=0, keepdims=True)
        pick = jidx == first
        on_pick(r, first, pick)
        w = jnp.where(pick, -jnp.inf, w)


def _nsa_pair_body(q_ref, g_ref, kc_ref, vc_ref, kt_ref, v_ref, mt_ref, es_ref, o_ref):
    ns = mt_ref.shape[0]
    rows = NSA_REP * Q_BLOCK
    blk = pl.program_id(1)
    s0 = blk * Q_BLOCK
    tq = s0 + lax.broadcasted_iota(I32, (Q_BLOCK, 1), 0)
    tcol = jnp.concatenate([tq] * NSA_REP, axis=0)
    ncp = kc_ref.shape[-2]
    cmask = lax.broadcasted_iota(I32, (1, ncp), 1) * CMP_STRIDE + (CMP_BLOCK - 1) <= tcol
    jidx = lax.broadcasted_iota(I32, (ns, Q_BLOCK), 0)
    cur = lax.shift_right_logical(s0 + lax.broadcasted_iota(I32, (ns, Q_BLOCK), 1), 6)
    valid = jidx <= cur
    forced = valid & ((jidx == 0) | (jidx > cur - N_LOCAL))
    tiles_per = KV_TILE // LANES

    span = WINDOW + Q_BLOCK
    w0 = jnp.maximum(blk - WINDOW // Q_BLOCK, 0)
    start = pl.multiple_of(w0 * Q_BLOCK, Q_BLOCK)
    wpos = start + lax.broadcasted_iota(I32, (1, span), 1)
    wmask = (wpos <= tcol) & (wpos > tcol - WINDOW)
    vw_both = v_ref[0, 1, pl.ds(start, span), :]

    def vals_aug(g, v_both):
        own = (lax.broadcasted_iota(I32, v_both.shape, 1) >= NSA_DH) == (g == 1)
        return jnp.where(own, v_both, jnp.ones((), BF16))

    def normalised(g, acc):
        if g == 1:
            acc = pltpu.roll(acc, NSA_DH, 1)
        return acc[:, :NSA_DH] * (1.0 / jnp.maximum(acc[:, NSA_DH:NSA_DH + 1], 1e-30))

    q_aug, o_c, o_w = [], [], []
    for g in range(NSA_KV):
        q = q_ref[0, g * NSA_REP:(g + 1) * NSA_REP].reshape(rows, NSA_DH)
        probs = _masked_probs(_dot_nt(q, kc_ref[0, 0, g]), cmask)
        o_c.append(_dot(probs.astype(BF16), vc_ref[0, 0, g]))
        psum = probs[0:Q_BLOCK]
        for r in range(1, NSA_REP):
            psum = psum + probs[r * Q_BLOCK:(r + 1) * Q_BLOCK]
        hi, lo = _split_hi_lo(psum)
        imp = _dot_nt(mt_ref[...], hi) + _dot_nt(mt_ref[...], lo)
        picked = [jnp.zeros((ns, Q_BLOCK), F32)]

        def on_pick(r, first, pick, picked=picked):
            picked[0] = jnp.where(pick, 1.0, picked[0])

        _topk_rows(jnp.where(forced, jnp.inf, jnp.where(valid, imp, -jnp.inf)), min(SEL_TOPK, ns), on_pick)
        bias_t = jnp.where(valid, jnp.where(picked[0] > 0.0, 0.0, MASKED), MASKED).T.astype(BF16)
        q_aug.append(jnp.concatenate([jnp.concatenate([bias_t] * NSA_REP, axis=0), q], axis=1))
        kw = jnp.concatenate([kt_ref[0, NSA_KV + g, w0 + j] for j in range(span // LANES)], axis=1)
        o_w.append(normalised(g, _dot(_masked_exp(_dot(q, kw), wmask).astype(BF16), vals_aug(g, vw_both))))

    def update(carry, qa, k_aug, v, mask):
        m, acc = carry
        sc = _dot(qa, k_aug)
        if mask is not None:
            sc = jnp.where(mask, sc, MASKED)
        m_new = jnp.maximum(m, jnp.max(sc, axis=-1, keepdims=True))
        p = jnp.exp(sc - m_new)
        return m_new, jnp.exp(m - m_new) * acc + _dot(p.astype(BF16), v)

    def step(kt, carry, mask=None):
        onehot = es_ref[kt]
        v_both = v_ref[0, 0, pl.ds(pl.multiple_of(kt * KV_TILE, KV_TILE), KV_TILE), :]
        out = []
        for g in range(NSA_KV):
            kt4 = jnp.concatenate([kt_ref[0, g, kt * tiles_per + j] for j in range(tiles_per)], axis=1)
            out.append(update(carry[g], q_aug[g], jnp.concatenate([onehot, kt4], axis=0), vals_aug(g, v_both), mask))
        return tuple(out)

    one = (jnp.full((rows, 1), -jnp.inf, F32), jnp.zeros((rows, LANES), F32))
    last = (s0 + Q_BLOCK - 1) // KV_TILE
    carry = lax.fori_loop(0, last // 4, lambda p, c: step(4 * p + 3, step(4 * p + 2, step(4 * p + 1, step(4 * p, c)))),
                          (one,) * NSA_KV)
    carry = lax.fori_loop((last // 4) * 4, last, step, carry)
    causal = last * KV_TILE + lax.broadcasted_iota(I32, (1, KV_TILE), 1) <= tcol
    carry = step(last, carry, causal)

    outs = []
    for g in range(NSA_KV):
        gates = g_ref[0, g]
        o_s = normalised(g, carry[g][1])
        for r in range(NSA_REP):
            sl = slice(r * Q_BLOCK, (r + 1) * Q_BLOCK)
            outs.append(gates[:, r:r + 1] * o_c[g][sl] + gates[:, 4 + r:5 + r] * o_s[sl] + gates[:, 8 + r:9 + r] * o_w[g][sl])
    o_ref[0] = jnp.concatenate(outs, axis=1).astype(BF16)


def _nsa_pair(qh, gates, kcv, kt, v, mt, es):
    b, _, t, _ = qh.shape
    ncp = kcv.shape[-2]
    return pl.pallas_call(
        _nsa_pair_body, grid=(b, t // Q_BLOCK),
        in_specs=[pl.BlockSpec((1, NSA_HEADS, Q_BLOCK, NSA_DH), lambda bi, i: (bi, 0, i, 0)),
                  pl.BlockSpec((1, NSA_KV, Q_BLOCK, LANES), lambda bi, i: (bi, 0, i, 0)),
                  pl.BlockSpec((1, 1, NSA_KV, ncp, NSA_DH), lambda bi, i: (0, bi, 0, 0, 0)),
                  pl.BlockSpec((1, 1, NSA_KV, ncp, NSA_DH), lambda bi, i: (1, bi, 0, 0, 0)),
                  pl.BlockSpec((1, 2 * NSA_KV, t // LANES, NSA_DH, LANES), lambda bi, i: (bi, 0, 0, 0, 0)),
                  pl.BlockSpec((1, 2, t, NSA_KVW), lambda bi, i: (bi, 0, 0, 0)),
                  _full(mt.shape), _full(es.shape)],
        out_specs=pl.BlockSpec((1, Q_BLOCK, NSA_Q), lambda bi, i: (bi, i, 0)),
        out_shape=jax.ShapeDtypeStruct((b, t, NSA_Q), BF16),
        compiler_params=_params(2), name="nsa_prompt")(qh, gates, kcv, kcv, kt, v, mt, es)


def _sample_cmp_body(pt_ref, *refs, n_chunks):
    pc = PAGES_PER_STEP
    pages = refs[:pc]
    q_ref, wk_ref, b1_ref, w2_ref, m_ref, perm_ref, oc_ref, imp_ref, rk_ref, ab_ref = refs[pc:]
    c = pl.program_id(1)
    per_page = 128 // CMP_STRIDE

    def by_offset(k, kind):
        x = jnp.concatenate([pages[k + d][0, kind * LANES:(kind + 1) * LANES, :].astype(BF16) for d in range(2)], axis=1)
        return _dot(x, perm_ref[...]).astype(BF16).T

    for kind in range(2):
        for k in range(0, pc, 2):
            pair = by_offset(k, kind)
            for j in range(CMP_STRIDE):
                rk_ref[kind, k * per_page:(k + 2) * per_page, j * LANES:(j + 1) * LANES] = (
                    pair[2 * j * per_page:2 * (j + 1) * per_page])
    row0 = pl.multiple_of(c * (pc * per_page), pc * per_page)
    for kind in range(2):
        ab_ref[kind, pl.ds(row0, pc * per_page), :] = _dot(rk_ref[kind], wk_ref[kind])

    @pl.when(c == n_chunks - 1)
    def _():
        kc = _compress_finish(ab_ref[0], b1_ref[0], w2_ref[0]).astype(BF16)
        vc = _compress_finish(ab_ref[1], b1_ref[1], w2_ref[1]).astype(BF16)
        ncp = kc.shape[0]
        row = lax.broadcasted_iota(I32, (NSA_HEADS, LANES), 0)
        lane = lax.broadcasted_iota(I32, (NSA_HEADS, LANES), 1)
        qbd = jnp.where((lane >= NSA_DH) == (row >= NSA_REP), q_ref[0], jnp.zeros((), BF16))
        s = _dot_nt(qbd, kc)
        n = lax.broadcasted_iota(I32, (1, ncp), 1)
        probs = _masked_probs(s, n * CMP_STRIDE + (CMP_BLOCK - 1) <= ncp * CMP_STRIDE)
        oc_ref[0] = _dot(probs.astype(BF16), vc)
        hi, lo = _split_hi_lo(probs)
        imp8 = _dot(hi, m_ref[...]) + _dot(lo, m_ref[...])
        r8 = lax.broadcasted_iota(I32, imp8.shape, 0)
        imp_ref[0, 0:1, :] = jnp.sum(jnp.where(r8 < NSA_REP, imp8, 0.0), axis=0, keepdims=True)
        imp_ref[0, 1:2, :] = jnp.sum(jnp.where(r8 >= NSA_REP, imp8, 0.0), axis=0, keepdims=True)


def _sample_cmp(pt_flat, cache_t, qrep, wk, b1, w2, m, n_pages):
    nb = qrep.shape[0]
    pc = PAGES_PER_STEP
    n_chunks = n_pages // pc
    ncp = n_pages * 128 // CMP_STRIDE
    ns = m.shape[1]

    def page(k):
        return pl.BlockSpec((1, 2 * LANES, 128), lambda b, c, pt: (pt[b * n_pages + c * pc + k], 0, 0))

    grid_spec = pltpu.PrefetchScalarGridSpec(
        num_scalar_prefetch=1, grid=(nb, n_chunks),
        in_specs=[page(k) for k in range(pc)] + [
            pl.BlockSpec((1, NSA_HEADS, LANES), lambda b, c, pt: (b, 0, 0)),
            pl.BlockSpec(wk.shape, lambda b, c, pt: (0, 0, 0)), pl.BlockSpec(b1.shape, lambda b, c, pt: (0, 0, 0)),
            pl.BlockSpec(w2.shape, lambda b, c, pt: (0, 0, 0)), pl.BlockSpec(m.shape, lambda b, c, pt: (0, 0)),
            pl.BlockSpec((256, 256), lambda b, c, pt: (0, 0))],
        out_specs=[pl.BlockSpec((1, NSA_HEADS, LANES), lambda b, c, pt: (b, 0, 0)),
                   pl.BlockSpec((1, NSA_KV, ns), lambda b, c, pt: (b, 0, 0))],
        scratch_shapes=[pltpu.VMEM((2, pc * 8, CMP_STRIDE * LANES), BF16), pltpu.VMEM((2, ncp, 2 * LANES), F32)])
    pos = np.arange(256)
    perm = np.zeros((256, 256), np.float32)
    perm[pos, (pos % CMP_STRIDE) * 16 + (pos // 128) * 8 + (pos % 128) // CMP_STRIDE] = 1.0
    return pl.pallas_call(
        functools.partial(_sample_cmp_body, n_chunks=n_chunks), grid_spec=grid_spec,
        out_shape=[jax.ShapeDtypeStruct((nb, NSA_HEADS, LANES), F32), jax.ShapeDtypeStruct((nb, NSA_KV, ns), F32)],
        compiler_params=_params(2), name="sample_compress")(
            pt_flat, *([cache_t] * pc), qrep, wk, b1, w2, m, jnp.asarray(perm, BF16))


def _sample_topk_body(imp_ref, idx_ref, *, k):
    ns = imp_ref.shape[1]
    idx_ref[...] = jnp.zeros(idx_ref.shape, I32)
    for g in range(NSA_KV):
        jidx = lax.broadcasted_iota(I32, imp_ref.shape[1:], 0)
        w = jnp.where((jidx == 0) | (jidx > ns - N_LOCAL), jnp.inf, imp_ref[g])

        def on_pick(r, first, pick, g=g):
            idx_ref[g, r:r + 1, :] = first

        _topk_rows(w, k, on_pick)


def _sample_topk(imp_t, k):
    _, ns, nb = imp_t.shape
    return pl.pallas_call(
        functools.partial(_sample_topk_body, k=k), grid=(1,),
        in_specs=[_full(imp_t.shape)], out_specs=_full((NSA_KV, SEL_TOPK, nb)),
        out_shape=jax.ShapeDtypeStruct((NSA_KV, SEL_TOPK, nb), I32),
        compiler_params=_params(1), name="sample_topk")(imp_t)


def _attend_with_self(qg, keys_t, vals_t, k_self, v_self, mask):
    s = _dot(qg, keys_t)
    if mask is not None:
        s = jnp.where(mask, s, -jnp.inf)
    s_self = jnp.sum(qg.astype(F32) * k_self.astype(BF16).astype(F32), axis=-1, keepdims=True)
    m = jnp.maximum(jnp.max(s, axis=-1, keepdims=True), s_self)
    p = jnp.exp(s - m)
    p_self = jnp.exp(s_self - m)
    den = jnp.sum(p, axis=-1, keepdims=True) + p_self
    num = _dot_nt(p.astype(BF16), vals_t) + p_self.astype(BF16).astype(F32) * v_self.astype(BF16).astype(F32)
    return num / den


def _sample_attn_body(idx_ref, pt_ref, *refs, n_sel):
    n_blk = NSA_KV * n_sel
    blocks = refs[:n_blk]
    (win_ref, q_ref, kvn_ref, gate_ref, oc_ref, mq_ref, mem_ref,
     ob_ref, om_ref, wn_ref, ks_ref, vs_ref) = refs[n_blk:]
    b = pl.program_id(0)
    row = lax.broadcasted_iota(I32, (NSA_HEADS, NSA_DH), 0)
    q8 = q_ref[0][:, :NSA_DH]
    qg = [jnp.where((row >= NSA_REP) == (g == 1), q8, jnp.zeros((), BF16)) for g in range(NSA_KV)]
    kvn = kvn_ref[0]

    def new_row(kind, g):
        lo = kind * NSA_KVW + g * NSA_DH
        return kvn[:, lo:lo + NSA_DH]

    lane = lax.broadcasted_iota(I32, (1, n_sel * LANES), 1)
    upper = (lane & SEL_BLOCK) != 0
    slot = lax.shift_right_logical(lane, 7)
    o_sel = []
    for g in range(NSA_KV):
        want_upper = jnp.zeros((1, n_sel * LANES), I32)
        for k in range(n_sel):
            blk = blocks[g * n_sel + k]
            ks_ref[g, :, k * LANES:(k + 1) * LANES] = blk[0, g * NSA_DH:(g + 1) * NSA_DH, :].astype(BF16)
            vs_ref[g, :, k * LANES:(k + 1) * LANES] = blk[0, NSA_KVW + g * NSA_DH:NSA_KVW + (g + 1) * NSA_DH, :].astype(BF16)
            j = idx_ref[b * (NSA_KV * SEL_TOPK) + g * SEL_TOPK + k]
            want_upper = jnp.where(slot == k, j & 1, want_upper)
        o_sel.append(_attend_with_self(qg[g], ks_ref[g], vs_ref[g], new_row(2, g), new_row(3, g),
                                       upper == (want_upper != 0)))
    o_s = jnp.where(row < NSA_REP, o_sel[0], o_sel[1])

    win = win_ref[0]
    wbuf = win.shape[1]
    wcol = lax.broadcasted_iota(I32, (1, wbuf), 1)
    o_win = []
    for g in range(NSA_KV):
        kt = win[g * NSA_DH:(g + 1) * NSA_DH].astype(BF16)
        vt = win[NSA_KVW + g * NSA_DH:NSA_KVW + (g + 1) * NSA_DH].astype(BF16)
        o_win.append(_attend_with_self(qg[g], kt, vt, new_row(4, g), new_row(5, g), wcol >= 1))
    o_w = jnp.where(row < NSA_REP, o_win[0], o_win[1])
    rows_w = 2 * NSA_KVW
    eye = lax.broadcasted_iota(I32, (rows_w, rows_w), 0) == lax.broadcasted_iota(I32, (rows_w, rows_w), 1)
    new_col = jnp.sum(jnp.where(eye, kvn[:, 4 * NSA_KVW:], 0.0), axis=-1, keepdims=True)
    wn_ref[0] = jnp.where(lax.broadcasted_iota(I32, win.shape, 1) == wbuf - 1, new_col, pltpu.roll(win, wbuf - 1, 1))

    gate = gate_ref[0]
    grow = lax.broadcasted_iota(I32, (NSA_HEADS, LANES), 0)
    glane = lax.broadcasted_iota(I32, (NSA_HEADS, LANES), 1)
    oc = jnp.where((glane >= NSA_DH) == (grow >= NSA_REP), oc_ref[0], 0.0)
    o = jnp.zeros((NSA_HEADS, NSA_DH), F32)
    for j, branch in enumerate((oc[:, :NSA_DH] + oc[:, NSA_DH:], o_s, o_w)):
        gj = jnp.sum(jnp.where(glane == j * NSA_HEADS + grow, gate, 0.0), axis=-1, keepdims=True)
        o = o + gj * branch
    ob_ref[0] = o

    mq = mq_ref[0]
    per_tok = 2 * MEM_HEADS
    for h in range(MEM_HEADS):
        qh = jnp.broadcast_to(mq[:, h * MEM_DH:(h + 1) * MEM_DH], (NSA_HEADS, MEM_DH)).astype(BF16)
        km = mem_ref[0, pl.ds(h, MEM_TOKENS, stride=per_tok), :].astype(BF16)
        vm = mem_ref[0, pl.ds(MEM_HEADS + h, MEM_TOKENS, stride=per_tok), :].astype(BF16)
        p = _softmax(_dot_nt(qh, km) * (MEM_DH ** -0.5))
        om_ref[0, :, h * MEM_DH:(h + 1) * MEM_DH] = _dot(p.astype(BF16), vm)[0:1]


def _sample_attn(idx_flat, pt_flat, cache_t, win_t, qrep, kvn, gate, oc, mq, memkv, n_pages, n_sel):
    nb, rows_w, wbuf = win_t.shape

    def sel_block(g, k):
        def index(b, idx, pt):
            j = idx[b * (NSA_KV * SEL_TOPK) + g * SEL_TOPK + k]
            return (pt[b * n_pages + lax.shift_right_logical(j, 1)], 1, 0)
        return pl.BlockSpec((1, 2 * NSA_KVW, 128), index)

    def per_seq(shape):
        return pl.BlockSpec((1,) + shape, lambda b, idx, pt: (b,) + (0,) * len(shape))

    grid_spec = pltpu.PrefetchScalarGridSpec(
        num_scalar_prefetch=2, grid=(nb,),
        in_specs=[sel_block(g, k) for g in range(NSA_KV) for k in range(n_sel)] + [
            per_seq((rows_w, wbuf)), per_seq((NSA_HEADS, LANES)), per_seq((1, 768)), per_seq((1, LANES)),
            per_seq((NSA_HEADS, LANES)), per_seq((1, MEM_Q)), per_seq(memkv.shape[1:])],
        out_specs=[per_seq((NSA_HEADS, NSA_DH)), per_seq((1, MEM_Q)), per_seq((rows_w, wbuf))],
        scratch_shapes=[pltpu.VMEM((NSA_KV, NSA_DH, n_sel * LANES), BF16)] * 2)
    return pl.pallas_call(
        functools.partial(_sample_attn_body, n_sel=n_sel), grid_spec=grid_spec,
        out_shape=[jax.ShapeDtypeStruct((nb, NSA_HEADS, NSA_DH), F32), jax.ShapeDtypeStruct((nb, 1, MEM_Q), F32),
                   jax.ShapeDtypeStruct((nb, rows_w, wbuf), F32)],
        compiler_params=_params(1), name="sample_attn")(
            idx_flat, pt_flat, *([cache_t] * (NSA_KV * n_sel)), win_t, qrep, kvn, gate, oc, mq, memkv)


def _merge_body(x_ref, oa_ref, ob_ref, om_ref, g1_ref, wmg_ref, wb_ref, wo_ref, g2_ref, rwh_ref, rwl_ref, rb_ref,
                h1_ref, hn_ref, gt_ref):
    x = x_ref[...]
    ab = _rms(x, g1_ref[...]).astype(BF16)
    s = None
    for k, o_ref in enumerate((oa_ref, ob_ref, om_ref)):
        gk = jax.nn.sigmoid(_dot(ab, wmg_ref[:, k * D_MODEL:(k + 1) * D_MODEL]))
        term = gk * _dot(o_ref[...], wb_ref[k])
        s = term if s is None else s + term
    h1 = x + _dot(s.astype(BF16), wo_ref[...])
    h1_ref[...] = h1
    hn = _rms(h1, g2_ref[...])
    hi, lo = _split_hi_lo(hn)
    hn_ref[...] = hi

    logit = _dot_nt(rwh_ref[...], hi) + _dot_nt(rwh_ref[...], lo) + _dot_nt(rwl_ref[...], hi) + rb_ref[...]
    picked = [jnp.zeros(logit.shape, F32)]

    def on_pick(r, first, pick):
        picked[0] = jnp.where(pick, 1.0, picked[0])

    _topk_rows(logit, TOP_K, on_pick)
    top = jnp.max(logit, axis=0, keepdims=True)
    ex = jnp.where(picked[0] > 0.0, jnp.exp(logit - top), 0.0)
    gt_ref[...] = ex / jnp.sum(ex, axis=0, keepdims=True)


def _merge(x, oa, ob, om, g1, wmg, wb, wo, g2, rwh, rwl, rb, tm):
    n = x.shape[0]

    def row(width):
        return pl.BlockSpec((tm, width), lambda i: (i, 0))

    return pl.pallas_call(
        _merge_body, grid=(n // tm,),
        in_specs=[row(D_MODEL), row(512), row(512), row(512), _full((1, D_MODEL)), _const(wmg.shape), _const(wb.shape),
                  _const(wo.shape), _full((1, D_MODEL)), _full(rwh.shape), _full(rwl.shape), _full(rb.shape)],
        out_specs=[row(D_MODEL), row(D_MODEL), pl.BlockSpec((N_EXPERTS, tm), lambda i: (0, i))],
        out_shape=[jax.ShapeDtypeStruct((n, D_MODEL), F32), jax.ShapeDtypeStruct((n, D_MODEL), BF16),
                   jax.ShapeDtypeStruct((N_EXPERTS, n), F32)],
        compiler_params=_params(1), name="merge")(x, oa, ob, om, g1, wmg, wb, wo, g2, rwh, rwl, rb)


def _moe_body(cnt_ref, hn_ref, h1_ref, gt_ref, u_ref, wgu_ref, bgu_ref, wd_ref, bd_ref, fg_ref, y_ref,
              rank_ref, *, sub, cap):
    i, e = pl.program_id(0), pl.program_id(1)
    n_sub = hn_ref.shape[0] // sub

    @pl.when(e == 0)
    def _():
        for s in range(n_sub):
            cols = slice(s * sub, (s + 1) * sub)
            sel = jnp.where(gt_ref[:, cols] > 0.0, 1.0, 0.0).astype(BF16)
            rank_ref[:, cols] = _dot(sel, u_ref[...])
        y_ref[...] = h1_ref[...]

    slot = lax.broadcasted_iota(I32, (cap, sub), 0).astype(F32)
    spans = [slice(s * sub, (s + 1) * sub) for s in range(n_sub)]
    g_rows = [gt_ref[pl.ds(e, 1), sp] for sp in spans]
    r_rows = [rank_ref[pl.ds(e, 1), sp] for sp in spans]

    def chunk(ch, carry):
        want = slot + (ch * cap).astype(F32)
        hits = [(g_rows[s] > 0.0) & (r_rows[s] == want) for s in range(n_sub)]
        onehots = [jnp.where(h, 1.0, 0.0).astype(BF16) for h in hits]
        xe = jnp.concatenate([_dot(onehots[s], hn_ref[spans[s], :]).astype(BF16) for s in range(n_sub)], axis=0)
        gu = _dot(xe, wgu_ref[0]) + bgu_ref[0]
        gate = jnp.minimum(gu[:, :D_EXPERT], SWIGLU_LIMIT)
        up = jnp.clip(gu[:, D_EXPERT:], -SWIGLU_LIMIT, SWIGLU_LIMIT)
        act = (up + 1.0) * gate * jax.nn.sigmoid(SWIGLU_ALPHA * gate)
        ye = _dot(act.astype(BF16), wd_ref[0]) + bd_ref[0]
        for s in range(n_sub):
            w = jnp.sum(jnp.where(hits[s], g_rows[s], 0.0), axis=-1, keepdims=True)
            y_ref[spans[s], :] += _dot_tn(onehots[s], (ye[s * cap:(s + 1) * cap] * w).astype(BF16))
        return carry

    most = cnt_ref[i * n_sub * N_EXPERTS + e]
    for s in range(1, n_sub):
        most = jnp.maximum(most, cnt_ref[(i * n_sub + s) * N_EXPERTS + e])
    lax.fori_loop(0, (most + cap - 1) // cap, chunk, 0)

    @pl.when(e == N_EXPERTS - 1)
    def _():
        y_ref[...] = _rms(y_ref[...], fg_ref[...])


def _moe(counts, hn, h1, gate_t, upper, wgu, bgu, wd, bd, fg, tile, sub, cap):
    n = hn.shape[0]
    once = pl.Buffered(1)
    grid_spec = pltpu.PrefetchScalarGridSpec(
        num_scalar_prefetch=1, grid=(n // tile, N_EXPERTS),
        in_specs=[pl.BlockSpec((tile, D_MODEL), lambda i, e, c: (i, 0)),
                  pl.BlockSpec((tile, D_MODEL), lambda i, e, c: (i, 0), pipeline_mode=once),
                  pl.BlockSpec((N_EXPERTS, tile), lambda i, e, c: (0, i)),
                  pl.BlockSpec((sub, sub), lambda i, e, c: (0, 0), pipeline_mode=once),
                  pl.BlockSpec((1, D_MODEL, 2 * D_EXPERT), lambda i, e, c: (e, 0, 0)),
                  pl.BlockSpec((1, 1, 2 * D_EXPERT), lambda i, e, c: (e, 0, 0)),
                  pl.BlockSpec((1, D_EXPERT, D_MODEL), lambda i, e, c: (e, 0, 0)),
                  pl.BlockSpec((1, 1, D_MODEL), lambda i, e, c: (e, 0, 0)),
                  pl.BlockSpec((1, D_MODEL), lambda i, e, c: (0, 0))],
        out_specs=pl.BlockSpec((tile, D_MODEL), lambda i, e, c: (i, 0)),
        scratch_shapes=[pltpu.VMEM((N_EXPERTS, tile), F32)])
    return pl.pallas_call(
        functools.partial(_moe_body, sub=sub, cap=cap), grid_spec=grid_spec,
        out_shape=jax.ShapeDtypeStruct((n, D_MODEL), F32),
        compiler_params=_params(2), name="moe")(counts, hn, h1, gate_t, upper, wgu, bgu, wd, bd, fg)


def _gate_columns(order):
    idx = np.zeros((LANES,), np.int32)
    ok = np.zeros((LANES,), bool)
    for c, (h, j) in enumerate(order):
        idx[c], ok[c] = C_NG + h * 3 + j, True
    return idx, ok


def _prep_in_weights(w_in):
    def sect(lo, hi):
        return w_in[:, lo:hi]

    def gates(order):
        idx, ok = _gate_columns(order)
        return jnp.where(jnp.asarray(ok)[None, :], w_in[:, idx], 0.0)

    w_q = sect(C_Q, C_KV) * (NSA_DH ** -0.5)
    g_prompt = [gates([(NSA_REP * g + r, j) for j in range(3) for r in range(NSA_REP)]) for g in range(NSA_KV)]
    w_prompt = jnp.concatenate([sect(C_U, C_Q), w_q, sect(C_KV, C_NG)] + g_prompt + [sect(C_MQ, C_MG)], axis=1)
    q_rep = jnp.concatenate([w_q.reshape(D_MODEL, NSA_HEADS, 1, NSA_DH)] * 2, axis=2).reshape(D_MODEL, 2 * NSA_Q)
    g_sample = gates([(h, j) for j in range(3) for h in range(NSA_HEADS)])
    w_sample = jnp.concatenate([sect(C_U, C_Q), q_rep, sect(C_KV, C_NG), g_sample, sect(C_MQ, C_MG)], axis=1)
    w_kv_t = w_in.T[C_KV:C_NG]
    return w_prompt.astype(BF16), w_sample.astype(BF16), sect(C_MG, w_in.shape[1]).astype(BF16), w_kv_t.astype(BF16)


def _prep_compress(w1, b1, w2):
    eye = jnp.eye(NSA_KV, dtype=F32)
    w = w1.reshape(2, 2, CMP_STRIDE, NSA_DH, NSA_DH)
    wk = jnp.einsum('khjde,xg->kjxdhge', w, eye).reshape(2, CMP_STRIDE * LANES, 2 * LANES)
    b = jnp.concatenate([b1, b1], axis=-1).reshape(2, 1, LANES)
    w2d = jnp.einsum('kde,xg->kxdge', w2, eye).reshape(2, LANES, LANES)
    return wk.astype(BF16), b, w2d.astype(BF16)


def _cmp_to_sel_t(ncp, ns):
    cs = np.arange(ncp) * CMP_STRIDE
    ss = np.arange(ns) * SEL_BLOCK
    m = (cs[None, :] <= ss[:, None] + SEL_BLOCK - 1) & (cs[None, :] + CMP_BLOCK - 1 >= ss[:, None])
    m[:, ncp - 1] = False
    return m.astype(np.float32)


def _block_onehot(t):
    e = (np.arange(t // SEL_BLOCK)[:, None] == (np.arange(t) // SEL_BLOCK)[None, :]).astype(np.float32)
    return e.reshape(t // SEL_BLOCK, t // KV_TILE, KV_TILE).transpose(1, 0, 2)


def _moe_group(h1_and_gates, weights, tile, sub, cap):
    h1, hn, gate_t = h1_and_gates
    n = h1.shape[0]
    counts = jnp.sum((gate_t > 0.0).reshape(N_EXPERTS, n // sub, sub), axis=-1, dtype=I32).T.reshape(-1)
    upper = jnp.asarray(np.triu(np.ones((sub, sub), np.float32), 1), BF16)
    return _moe(counts, hn, h1, gate_t, upper, *weights, tile, sub, cap)


def kernel(x_prompt, x_sample, mem_prompt, cache_nsa_kv, cache_nsa_win, cache_mem_kv, page_table,
           norm1_g, w_in, sgu_ln_g, sgu_ln_b, sgu_w, sgu_b, cmp_w1, cmp_b1, cmp_w2, w_mem_kv,
           w_branch, w_out, norm2_g, router_w, router_b, exp_w_gate_up, exp_b_gate_up,
           exp_w_down, exp_b_down, final_norm_g):
    assert norm1_g.shape[0] == 1, "one layer"
    bp, t, _ = x_prompt.shape
    nb = x_sample.shape[0]
    n_pool, page = cache_nsa_kv.shape[1:3]
    n_pages = page_table.shape[1]
    past = n_pages * page
    wbuf = cache_nsa_win.shape[2]
    assert page == 128 and wbuf == WINDOW and n_pages % PAGES_PER_STEP == 0 and t % KV_TILE == 0 and t >= WINDOW + Q_BLOCK

    g1, g2, fg = norm1_g[0][None], norm2_g[0][None], final_norm_g[None]
    lng, lnb = sgu_ln_g[0][None], sgu_ln_b[0][None]
    w_prompt, w_sample, w_mg, w_kv_t = _prep_in_weights(w_in[0])
    wk, b1c, w2c = _prep_compress(cmp_w1[0], cmp_b1[0], cmp_w2[0])
    wb, wo = w_branch[0].astype(BF16), w_out[0].astype(BF16)
    rwh, rwl = _split_hi_lo(router_w[0].T)
    rb = router_b[0][:, None]
    moe_w = (exp_w_gate_up[0].astype(BF16), exp_b_gate_up[0][:, None, :], exp_w_down[0].astype(BF16),
             exp_b_down[0][:, None, :], fg)
    merge_w = (g1, w_mg, wb, wo, g2, rwh, rwl, rb)

    mkv, mkv_bf = _mem_kv(mem_prompt.reshape(bp * MEM_TOKENS, D_MODEL), w_mem_kv[0].astype(BF16))
    oa, qh, kv4t, kvwt, kt, v, kcv_rows, gates, om = _inproj_prompt(
        x_prompt, g1, w_prompt, w_kv_t, lng, lnb, sgu_w[0], sgu_b[0].T, mkv_bf.reshape(bp, MEM_TOKENS, 2 * MEM_Q))
    ncp, ns = t // CMP_STRIDE, t // SEL_BLOCK
    r = jnp.stack([kcv_rows[:, :, k * LANES:(k + 1) * LANES].reshape(bp, ncp, CMP_STRIDE * LANES) for k in range(2)])
    kcv = _compress_prompt(r, wk, b1c, w2c)
    mt = jnp.asarray(_cmp_to_sel_t(ncp, ns), BF16)
    ob = _nsa_pair(qh, gates, kcv, kt, v, mt, jnp.asarray(_block_onehot(t), BF16))
    n = bp * t
    x2 = x_prompt.reshape(n, D_MODEL)
    y_prompt = _moe_group(_merge(x2, oa.reshape(n, 512), ob.reshape(n, 512), om.reshape(n, 512), *merge_w, tm=min(MERGE_ROWS, n)),
                          moe_w, min(MOE_BLOCK, n), min(MOE_TILE, n), MOE_CAP)

    xs = x_sample.reshape(nb, D_MODEL)
    wsc = jnp.repeat(sgu_w[0][:, 0, 0], SGU_GROUP_DIM)[None]
    bsc = jnp.repeat(sgu_b[0][:, 0], SGU_GROUP_DIM)[None]
    oa_s, z2_s, q_s, kv_s, gate_s, mq_s = _inproj_sample(xs, g1, w_sample, lng, lnb, wsc, bsc)
    qrep = q_s.reshape(nb, NSA_HEADS, LANES)
    pt_flat = page_table.reshape(-1)
    ncp_s, ns_s = past // CMP_STRIDE, past // SEL_BLOCK
    m_s = jnp.asarray(_cmp_to_sel_t(ncp_s, ns_s).T, BF16)
    cache_t = jnp.transpose(cache_nsa_kv[0], (0, 2, 3, 4, 1)).reshape(n_pool, 4 * NSA_KVW, page)
    win_t = jnp.transpose(cache_nsa_win[0], (0, 2, 3, 4, 1)).reshape(nb, 2 * NSA_KVW, wbuf)
    oc_s, imp_s = _sample_cmp(pt_flat, cache_t, qrep, wk, b1c, w2c, m_s, n_pages)
    n_sel = min(SEL_TOPK, ns_s + 1) - 1
    idx = _sample_topk(imp_s.transpose(1, 2, 0), n_sel)
    idx_flat = idx.transpose(2, 0, 1).reshape(-1)
    ob_s, om_s, win_new = _sample_attn(
        idx_flat, pt_flat, cache_t, win_t, qrep, kv_s[:, None, :], gate_s[:, None, :], oc_s, mq_s[:, None, :],
        cache_mem_kv[0].reshape(nb, MEM_TOKENS * 2 * MEM_HEADS, MEM_DH), n_pages, n_sel)
    y_sample = _moe_group(_merge(xs, oa_s, ob_s.reshape(nb, NSA_Q).astype(BF16), om_s.reshape(nb, MEM_Q).astype(BF16),
                                 *merge_w, tm=nb), moe_w, nb, nb, 32)

    def positions_last(x_t, n_kinds):
        lead, _, npos = x_t.shape
        return jnp.transpose(x_t.reshape(lead, n_kinds, NSA_KV, NSA_DH, npos), (0, 4, 1, 2, 3))[None]

    return (y_prompt.reshape(bp, t, D_MODEL), y_sample.reshape(nb, 1, D_MODEL),
            positions_last(kv4t, 4), positions_last(kvwt[:, :, t - WINDOW:], 2),
            mkv.reshape(1, bp, MEM_TOKENS, 2, MEM_HEADS, MEM_DH), kv_s[:, :512].reshape(1, nb, 1, 4, NSA_KV, NSA_DH),
            positions_last(win_new, 2), z2_s.reshape(1, nb, 1, SGU_DIM))
```

```python
import functools

import numpy as np
import jax
import jax.numpy as jnp
from jax import lax
from jax.experimental import pallas as pl
from jax.experimental.pallas import tpu as pltpu

F32, BF16, I32 = jnp.float32, jnp.bfloat16, jnp.int32

D_MODEL = 1024
SGU_GROUPS, SGU_GROUP_DIM, SGU_DIM, CHUNK = 4, 128, 512, 128
NSA_HEADS, NSA_KV, NSA_REP, NSA_DH = 8, 2, 4, 64
CMP_BLOCK, CMP_STRIDE, SEL_BLOCK, SEL_TOPK, N_LOCAL, WINDOW, Q_BLOCK = 32, 16, 64, 16, 2, 512, 128
MEM_TOKENS, MEM_HEADS, MEM_DH = 256, 4, 128
N_EXPERTS, TOP_K, D_EXPERT = 32, 4, 1024
SWIGLU_LIMIT, SWIGLU_ALPHA, EPS = 7.0, 1.702, 1e-5
NSA_Q, NSA_KVW, MEM_Q = NSA_HEADS * NSA_DH, NSA_KV * NSA_DH, MEM_HEADS * MEM_DH
C_U, C_V, C_Q, C_KV, C_NG, C_MQ, C_MG = 0, 512, 1024, 1536, 2304, 2328, 2840

LANES = 128
VMEM_LIMIT = 56 * 1024 * 1024
MASKED = -1e9
KV_TILE = 1024
PAGES_PER_STEP = 64
MERGE_ROWS = 1024
MOE_TILE, MOE_CAP = 1024, 160
MOE_BLOCK = 2048


def _dot(a, b):
    return jnp.dot(a, b, preferred_element_type=F32)


def _dot_nt(a, b):
    return lax.dot_general(a, b, (((1,), (1,)), ((), ())), preferred_element_type=F32)


def _dot_tn(a, b):
    return lax.dot_general(a, b, (((0,), (0,)), ((), ())), preferred_element_type=F32)


def _rms(x, g):
    return x * lax.rsqrt(jnp.mean(x * x, axis=-1, keepdims=True) + EPS) * g


def _layernorm(x, g, b):
    mu = jnp.mean(x, axis=-1, keepdims=True)
    xc = x - mu
    return xc * lax.rsqrt(jnp.mean(xc * xc, axis=-1, keepdims=True) + EPS) * g + b


def _masked_exp(s, mask):
    s = jnp.where(mask, s, -jnp.inf)
    m = jnp.max(s, axis=-1, keepdims=True)
    m = jnp.where(jnp.isfinite(m), m, 0.0)
    return jnp.exp(s - m)


def _masked_probs(s, mask):
    p = _masked_exp(s, mask)
    return p * (1.0 / jnp.maximum(jnp.sum(p, axis=-1, keepdims=True), 1e-30))


def _softmax(s):
    p = jnp.exp(s - jnp.max(s, axis=-1, keepdims=True))
    return p / jnp.sum(p, axis=-1, keepdims=True)


def _split_hi_lo(x):
    hi = x.astype(BF16)
    return hi, (x - hi.astype(F32)).astype(BF16)


def _params(n_axes):
    return pltpu.CompilerParams(dimension_semantics=("arbitrary",) * n_axes, vmem_limit_bytes=VMEM_LIMIT)


def _full(shape):
    return pl.BlockSpec(shape, lambda *_: (0,) * len(shape))


def _const(shape):
    return pl.BlockSpec(shape, lambda *_: (0,) * len(shape), pipeline_mode=pl.Buffered(1))


def _mkv_body(x_ref, w_ref, o_ref, ob_ref):
    r = _dot(x_ref[...].astype(BF16), w_ref[...])
    o_ref[...] = r
    ob_ref[...] = r.astype(BF16)


def _mem_kv(mem2d, w_bf):
    rows, tm = mem2d.shape[0], MEM_TOKENS
    return pl.pallas_call(
        _mkv_body, grid=(rows // tm,),
        in_specs=[pl.BlockSpec((tm, D_MODEL), lambda i: (i, 0)), _full((D_MODEL, 2 * MEM_Q))],
        out_specs=[pl.BlockSpec((tm, 2 * MEM_Q), lambda i: (i, 0))] * 2,
        out_shape=[jax.ShapeDtypeStruct((rows, 2 * MEM_Q), F32), jax.ShapeDtypeStruct((rows, 2 * MEM_Q), BF16)],
        compiler_params=_params(1), name="mem_kv")(mem2d, w_bf)


def _inproj_prompt_body(x_ref, g_ref, w_ref, wkvt_ref, lng_ref, lnb_ref, ws_ref, bst_ref, mkv_ref,
                        oa_ref, qh_ref, kv4t_ref, kvwt_ref, kt_ref, v_ref, kcv_ref, gate_ref, om_ref, *, tm):
    ab = _rms(x_ref[0], g_ref[...]).astype(BF16)

    def proj(lo, hi):
        return _dot(ab, w_ref[:, lo:hi])

    z1 = jax.nn.gelu(proj(0, 512))
    z2 = _layernorm(jax.nn.gelu(proj(512, 1024)), lng_ref[...], lnb_ref[...])
    tril = lax.broadcasted_iota(I32, (CHUNK, CHUNK), 0) >= lax.broadcasted_iota(I32, (CHUNK, CHUNK), 1)
    for g in range(SGU_GROUPS):
        cols = slice(g * SGU_GROUP_DIM, (g + 1) * SGU_GROUP_DIM)
        w = jnp.where(tril, ws_ref[g], 0.0).astype(BF16)
        for c in range(tm // CHUNK):
            rows = slice(c * CHUNK, (c + 1) * CHUNK)
            mixed = _dot(w, z2[rows, cols].astype(BF16)) + bst_ref[:, g:g + 1]
            oa_ref[0, rows, cols] = (z1[rows, cols] * mixed).astype(BF16)

    q = proj(1024, 1536)
    for h in range(NSA_HEADS):
        qh_ref[0, h] = q[:, h * NSA_DH:(h + 1) * NSA_DH].astype(BF16)

    kvt = _dot_nt(wkvt_ref[...], ab)
    kv4t_ref[0] = kvt[:512]
    kvwt_ref[0] = kvt[512:]
    for c, r0 in enumerate((256, 320, 512, 576)):
        for u in range(tm // LANES):
            kt_ref[0, c, u] = kvt[r0:r0 + NSA_DH, u * LANES:(u + 1) * LANES].astype(BF16)
    kcv_ref[0] = proj(1536, 1792).astype(BF16)
    v_ref[0, 0] = proj(1920, 2048).astype(BF16)
    v_ref[0, 1] = proj(2176, 2304).astype(BF16)

    gate_ref[0, 0] = jax.nn.sigmoid(proj(2304, 2432))
    gate_ref[0, 1] = jax.nn.sigmoid(proj(2432, 2560))

    mq = proj(2560, 3072).astype(BF16)
    for h in range(MEM_HEADS):
        cols = slice(h * MEM_DH, (h + 1) * MEM_DH)
        p = _softmax(_dot_nt(mq[:, cols], mkv_ref[0, :, cols]) * (MEM_DH ** -0.5))
        om_ref[0, :, cols] = _dot(p.astype(BF16), mkv_ref[0, :, MEM_Q + h * MEM_DH:MEM_Q + (h + 1) * MEM_DH]).astype(BF16)


def _inproj_prompt(x, g1, w, wkvt, lng, lnb, ws, bst, mkv_bf):
    b, t, _ = x.shape
    tm = min(512, t)

    def row(width):
        return pl.BlockSpec((1, tm, width), lambda bi, i: (bi, i, 0))

    def col(height):
        return pl.BlockSpec((1, height, tm), lambda bi, i: (bi, 0, i))

    def heads(n):
        return pl.BlockSpec((1, n, tm, NSA_DH), lambda bi, i: (bi, 0, i, 0))

    return pl.pallas_call(
        functools.partial(_inproj_prompt_body, tm=tm), grid=(b, t // tm),
        in_specs=[row(D_MODEL), _full((1, D_MODEL)), _full(w.shape), _full(wkvt.shape),
                  _full((1, SGU_DIM)), _full((1, SGU_DIM)),
                  _full((SGU_GROUPS, CHUNK, CHUNK)), _full((CHUNK, SGU_GROUPS)),
                  pl.BlockSpec((1, MEM_TOKENS, 2 * MEM_Q), lambda bi, i: (bi, 0, 0))],
        out_specs=[row(SGU_DIM), heads(NSA_HEADS), col(512), col(256),
                   pl.BlockSpec((1, 4, tm // LANES, NSA_DH, LANES), lambda bi, i: (bi, 0, i, 0, 0)),
                   pl.BlockSpec((1, 2, tm, NSA_KVW), lambda bi, i: (bi, 0, i, 0)), row(256),
                   pl.BlockSpec((1, 2, tm, LANES), lambda bi, i: (bi, 0, i, 0)), row(MEM_Q)],
        out_shape=[jax.ShapeDtypeStruct((b, t, SGU_DIM), BF16), jax.ShapeDtypeStruct((b, NSA_HEADS, t, NSA_DH), BF16),
                   jax.ShapeDtypeStruct((b, 512, t), F32), jax.ShapeDtypeStruct((b, 256, t), F32),
                   jax.ShapeDtypeStruct((b, 4, t // LANES, NSA_DH, LANES), BF16),
                   jax.ShapeDtypeStruct((b, 2, t, NSA_KVW), BF16), jax.ShapeDtypeStruct((b, t, 256), BF16),
                   jax.ShapeDtypeStruct((b, 2, t, LANES), F32), jax.ShapeDtypeStruct((b, t, MEM_Q), BF16)],
        compiler_params=_params(2), name="inproj_prompt")(x, g1, w, wkvt, lng, lnb, ws, bst, mkv_bf)


def _inproj_sample_body(x_ref, g_ref, w_ref, lng_ref, lnb_ref, wsc_ref, bsc_ref,
                        oa_ref, z2_ref, q_ref, kv_ref, gate_ref, mq_ref):
    ab = _rms(x_ref[...], g_ref[...]).astype(BF16)

    def proj(lo, hi):
        return _dot(ab, w_ref[:, lo:hi])

    z1 = jax.nn.gelu(proj(0, 512))
    z2 = _layernorm(jax.nn.gelu(proj(512, 1024)), lng_ref[...], lnb_ref[...])
    z2_ref[...] = z2
    oa_ref[...] = (z1 * (wsc_ref[...] * z2 + bsc_ref[...])).astype(BF16)
    q_ref[...] = proj(1024, 2048).astype(BF16)
    kv_ref[...] = proj(2048, 2816)
    gate_ref[...] = jax.nn.sigmoid(proj(2816, 2944))
    mq_ref[...] = proj(2944, 3456)


def _inproj_sample(x, g1, w, lng, lnb, wsc, bsc):
    n = x.shape[0]
    shapes = [((n, SGU_DIM), BF16), ((n, SGU_DIM), F32), ((n, 2 * NSA_Q), BF16), ((n, 768), F32),
              ((n, LANES), F32), ((n, MEM_Q), F32)]
    return pl.pallas_call(
        _inproj_sample_body, grid=(1,),
        in_specs=[_full(x.shape), _full((1, D_MODEL)), _full(w.shape), _full((1, SGU_DIM)), _full((1, SGU_DIM)),
                  _full((1, SGU_DIM)), _full((1, SGU_DIM))],
        out_specs=[_full(s) for s, _ in shapes],
        out_shape=[jax.ShapeDtypeStruct(s, d) for s, d in shapes],
        compiler_params=_params(1), name="inproj_sample")(x, g1, w, lng, lnb, wsc, bsc)


def _compress_finish(ab, b1, w2):
    n = ab.shape[0]
    c = ab[:, :LANES] + pltpu.roll(ab[:, LANES:], n - 1, 0) + b1
    return _dot(jax.nn.gelu(c).astype(BF16), w2)


def _compress_prompt_body(r_ref, wk_ref, b1_ref, w2_ref, o_ref):
    out = _compress_finish(_dot(r_ref[0, 0], wk_ref[0]), b1_ref[0], w2_ref[0])
    for g in range(NSA_KV):
        o_ref[0, 0, g] = out[:, g * NSA_DH:(g + 1) * NSA_DH].astype(BF16)


def _compress_prompt(r, wk, b1, w2):
    _, b, n, width = r.shape
    return pl.pallas_call(
        _compress_prompt_body, grid=(2, b),
        in_specs=[pl.BlockSpec((1, 1, n, width), lambda k, bi: (k, bi, 0, 0)),
                  pl.BlockSpec((1, width, 2 * LANES), lambda k, bi: (k, 0, 0)),
                  pl.BlockSpec((1, 1, LANES), lambda k, bi: (k, 0, 0)),
                  pl.BlockSpec((1, LANES, LANES), lambda k, bi: (k, 0, 0))],
        out_specs=pl.BlockSpec((1, 1, NSA_KV, n, NSA_DH), lambda k, bi: (k, bi, 0, 0, 0)),
        out_shape=jax.ShapeDtypeStruct((2, b, NSA_KV, n, NSA_DH), BF16),
        compiler_params=_params(2), name="compress_prompt")(r, wk, b1, w2)


def _topk_rows(w, k, on_pick):
    n = w.shape[0]
    jidx = lax.broadcasted_iota(I32, w.shape, 0)
    for r in range(k):
        m = jnp.max(w, axis=0, keepdims=True)
        first = jnp.min(jnp.where(w == m, jidx, n), axis=0, keepdims=True)
        pick = jidx == first
        on_pick(r, first, pick)
        w = jnp.where(pick, -jnp.inf, w)


def _nsa_pair_body(q_ref, g_ref, kc_ref, vc_ref, kt_ref, v_ref, mt_ref, es_ref, o_ref):
    ns = mt_ref.shape[0]
    rows = NSA_REP * Q_BLOCK
    blk = pl.program_id(1)
    s0 = blk * Q_BLOCK
    tq = s0 + lax.broadcasted_iota(I32, (Q_BLOCK, 1), 0)
    tcol = jnp.concatenate([tq] * NSA_REP, axis=0)
    ncp = kc_ref.shape[-2]
    cmask = lax.broadcasted_iota(I32, (1, ncp), 1) * CMP_STRIDE + (CMP_BLOCK - 1) <= tcol
    jidx = lax.broadcasted_iota(I32, (ns, Q_BLOCK), 0)
    cur = lax.shift_right_logical(s0 + lax.broadcasted_iota(I32, (ns, Q_BLOCK), 1), 6)
    valid = jidx <= cur
    forced = valid & ((jidx == 0) | (jidx > cur - N_LOCAL))
    tiles_per = KV_TILE // LANES

    span = WINDOW + Q_BLOCK
    w0 = jnp.maximum(blk - WINDOW // Q_BLOCK, 0)
    start = pl.multiple_of(w0 * Q_BLOCK, Q_BLOCK)
    wpos = start + lax.broadcasted_iota(I32, (1, span), 1)
    wmask = (wpos <= tcol) & (wpos > tcol - WINDOW)
    vw_both = v_ref[0, 1, pl.ds(start, span), :]

    def vals_aug(g, v_both):
        own = (lax.broadcasted_iota(I32, v_both.shape, 1) >= NSA_DH) == (g == 1)
        return jnp.where(own, v_both, jnp.ones((), BF16))

    def normalised(g, acc):
        if g == 1:
            acc = pltpu.roll(acc, NSA_DH, 1)
        return acc[:, :NSA_DH] * (1.0 / jnp.maximum(acc[:, NSA_DH:NSA_DH + 1], 1e-30))

    q_aug, o_c, o_w = [], [], []
    for g in range(NSA_KV):
        q = q_ref[0, g * NSA_REP:(g + 1) * NSA_REP].reshape(rows, NSA_DH)
        probs = _masked_probs(_dot_nt(q, kc_ref[0, 0, g]), cmask)
        o_c.append(_dot(probs.astype(BF16), vc_ref[0, 0, g]))
        psum = probs[0:Q_BLOCK]
        for r in range(1, NSA_REP):
            psum = psum + probs[r * Q_BLOCK:(r + 1) * Q_BLOCK]
        hi, lo = _split_hi_lo(psum)
        imp = _dot_nt(mt_ref[...], hi) + _dot_nt(mt_ref[...], lo)
        picked = [jnp.zeros((ns, Q_BLOCK), F32)]

        def on_pick(r, first, pick, picked=picked):
            picked[0] = jnp.where(pick, 1.0, picked[0])

        _topk_rows(jnp.where(forced, jnp.inf, jnp.where(valid, imp, -jnp.inf)), min(SEL_TOPK, ns), on_pick)
        bias_t = jnp.where(valid, jnp.where(picked[0] > 0.0, 0.0, MASKED), MASKED).T.astype(BF16)
        q_aug.append(jnp.concatenate([jnp.concatenate([bias_t] * NSA_REP, axis=0), q], axis=1))
        kw = jnp.concatenate([kt_ref[0, NSA_KV + g, w0 + j] for j in range(span // LANES)], axis=1)
        o_w.append(normalised(g, _dot(_masked_exp(_dot(q, kw), wmask).astype(BF16), vals_aug(g, vw_both))))

    def update(carry, qa, k_aug, v, mask):
        m, acc = carry
        sc = _dot(qa, k_aug)
        if mask is not None:
            sc = jnp.where(mask, sc, MASKED)
        m_new = jnp.maximum(m, jnp.max(sc, axis=-1, keepdims=True))
        p = jnp.exp(sc - m_new)
        return m_new, jnp.exp(m - m_new) * acc + _dot(p.astype(BF16), v)

    def step(kt, carry, mask=None):
        onehot = es_ref[kt]
        v_both = v_ref[0, 0, pl.ds(pl.multiple_of(kt * KV_TILE, KV_TILE), KV_TILE), :]
        out = []
        for g in range(NSA_KV):
            kt4 = jnp.concatenate([kt_ref[0, g, kt * tiles_per + j] for j in range(tiles_per)], axis=1)
            out.append(update(carry[g], q_aug[g], jnp.concatenate([onehot, kt4], axis=0), vals_aug(g, v_both), mask))
        return tuple(out)

    one = (jnp.full((rows, 1), -jnp.inf, F32), jnp.zeros((rows, LANES), F32))
    last = (s0 + Q_BLOCK - 1) // KV_TILE
    carry = lax.fori_loop(0, last // 4, lambda p, c: step(4 * p + 3, step(4 * p + 2, step(4 * p + 1, step(4 * p, c)))),
                          (one,) * NSA_KV)
    carry = lax.fori_loop((last // 4) * 4, last, step, carry)
    causal = last * KV_TILE + lax.broadcasted_iota(I32, (1, KV_TILE), 1) <= tcol
    carry = step(last, carry, causal)

    outs = []
    for g in range(NSA_KV):
        gates = g_ref[0, g]
        o_s = normalised(g, carry[g][1])
        for r in range(NSA_REP):
            sl = slice(r * Q_BLOCK, (r + 1) * Q_BLOCK)
            outs.append(gates[:, r:r + 1] * o_c[g][sl] + gates[:, 4 + r:5 + r] * o_s[sl] + gates[:, 8 + r:9 + r] * o_w[g][sl])
    o_ref[0] = jnp.concatenate(outs, axis=1).astype(BF16)


def _nsa_pair(qh, gates, kcv, kt, v, mt, es):
    b, _, t, _ = qh.shape
    ncp = kcv.shape[-2]
    return pl.pallas_call(
        _nsa_pair_body, grid=(b, t // Q_BLOCK),
        in_specs=[pl.BlockSpec((1, NSA_HEADS, Q_BLOCK, NSA_DH), lambda bi, i: (bi, 0, i, 0)),
                  pl.BlockSpec((1, NSA_KV, Q_BLOCK, LANES), lambda bi, i: (bi, 0, i, 0)),
                  pl.BlockSpec((1, 1, NSA_KV, ncp, NSA_DH), lambda bi, i: (0, bi, 0, 0, 0)),
                  pl.BlockSpec((1, 1, NSA_KV, ncp, NSA_DH), lambda bi, i: (1, bi, 0, 0, 0)),
                  pl.BlockSpec((1, 2 * NSA_KV, t // LANES, NSA_DH, LANES), lambda bi, i: (bi, 0, 0, 0, 0)),
                  pl.BlockSpec((1, 2, t, NSA_KVW), lambda bi, i: (bi, 0, 0, 0)),
                  _full(mt.shape), _full(es.shape)],
        out_specs=pl.BlockSpec((1, Q_BLOCK, NSA_Q), lambda bi, i: (bi, i, 0)),
        out_shape=jax.ShapeDtypeStruct((b, t, NSA_Q), BF16),
        compiler_params=_params(2), name="nsa_prompt")(qh, gates, kcv, kcv, kt, v, mt, es)


def _sample_cmp_body(pt_ref, *refs, n_chunks):
    pc = PAGES_PER_STEP
    pages = refs[:pc]
    q_ref, wk_ref, b1_ref, w2_ref, m_ref, perm_ref, oc_ref, imp_ref, rk_ref, ab_ref = refs[pc:]
    c = pl.program_id(1)
    per_page = 128 // CMP_STRIDE

    def by_offset(k, kind):
        x = jnp.concatenate([pages[k + d][0, kind * LANES:(kind + 1) * LANES, :].astype(BF16) for d in range(2)], axis=1)
        return _dot(x, perm_ref[...]).astype(BF16).T

    for kind in range(2):
        for k in range(0, pc, 2):
            pair = by_offset(k, kind)
            for j in range(CMP_STRIDE):
                rk_ref[kind, k * per_page:(k + 2) * per_page, j * LANES:(j + 1) * LANES] = (
                    pair[2 * j * per_page:2 * (j + 1) * per_page])
    row0 = pl.multiple_of(c * (pc * per_page), pc * per_page)
    for kind in range(2):
        ab_ref[kind, pl.ds(row0, pc * per_page), :] = _dot(rk_ref[kind], wk_ref[kind])

    @pl.when(c == n_chunks - 1)
    def _():
        kc = _compress_finish(ab_ref[0], b1_ref[0], w2_ref[0]).astype(BF16)
        vc = _compress_finish(ab_ref[1], b1_ref[1], w2_ref[1]).astype(BF16)
        ncp = kc.shape[0]
        row = lax.broadcasted_iota(I32, (NSA_HEADS, LANES), 0)
        lane = lax.broadcasted_iota(I32, (NSA_HEADS, LANES), 1)
        qbd = jnp.where((lane >= NSA_DH) == (row >= NSA_REP), q_ref[0], jnp.zeros((), BF16))
        s = _dot_nt(qbd, kc)
        n = lax.broadcasted_iota(I32, (1, ncp), 1)
        probs = _masked_probs(s, n * CMP_STRIDE + (CMP_BLOCK - 1) <= ncp * CMP_STRIDE)
        oc_ref[0] = _dot(probs.astype(BF16), vc)
        hi, lo = _split_hi_lo(probs)
        imp8 = _dot(hi, m_ref[...]) + _dot(lo, m_ref[...])
        r8 = lax.broadcasted_iota(I32, imp8.shape, 0)
        imp_ref[0, 0:1, :] = jnp.sum(jnp.where(r8 < NSA_REP, imp8, 0.0), axis=0, keepdims=True)
        imp_ref[0, 1:2, :] = jnp.sum(jnp.where(r8 >= NSA_REP, imp8, 0.0), axis=0, keepdims=True)


def _sample_cmp(pt_flat, cache_t, qrep, wk, b1, w2, m, n_pages):
    nb = qrep.shape[0]
    pc = PAGES_PER_STEP
    n_chunks = n_pages // pc
    ncp = n_pages * 128 // CMP_STRIDE
    ns = m.shape[1]

    def page(k):
        return pl.BlockSpec((1, 2 * LANES, 128), lambda b, c, pt: (pt[b * n_pages + c * pc + k], 0, 0))

    grid_spec = pltpu.PrefetchScalarGridSpec(
        num_scalar_prefetch=1, grid=(nb, n_chunks),
        in_specs=[page(k) for k in range(pc)] + [
            pl.BlockSpec((1, NSA_HEADS, LANES), lambda b, c, pt: (b, 0, 0)),
            pl.BlockSpec(wk.shape, lambda b, c, pt: (0, 0, 0)), pl.BlockSpec(b1.shape, lambda b, c, pt: (0, 0, 0)),
            pl.BlockSpec(w2.shape, lambda b, c, pt: (0, 0, 0)), pl.BlockSpec(m.shape, lambda b, c, pt: (0, 0)),
            pl.BlockSpec((256, 256), lambda b, c, pt: (0, 0))],
        out_specs=[pl.BlockSpec((1, NSA_HEADS, LANES), lambda b, c, pt: (b, 0, 0)),
                   pl.BlockSpec((1, NSA_KV, ns), lambda b, c, pt: (b, 0, 0))],
        scratch_shapes=[pltpu.VMEM((2, pc * 8, CMP_STRIDE * LANES), BF16), pltpu.VMEM((2, ncp, 2 * LANES), F32)])
    pos = np.arange(256)
    perm = np.zeros((256, 256), np.float32)
    perm[pos, (pos % CMP_STRIDE) * 16 + (pos // 128) * 8 + (pos % 128) // CMP_STRIDE] = 1.0
    return pl.pallas_call(
        functools.partial(_sample_cmp_body, n_chunks=n_chunks), grid_spec=grid_spec,
        out_shape=[jax.ShapeDtypeStruct((nb, NSA_HEADS, LANES), F32), jax.ShapeDtypeStruct((nb, NSA_KV, ns), F32)],
        compiler_params=_params(2), name="sample_compress")(
            pt_flat, *([cache_t] * pc), qrep, wk, b1, w2, m, jnp.asarray(perm, BF16))


def _sample_topk_body(imp_ref, idx_ref, *, k):
    ns = imp_ref.shape[1]
    idx_ref[...] = jnp.zeros(idx_ref.shape, I32)
    for g in range(NSA_KV):
        jidx = lax.broadcasted_iota(I32, imp_ref.shape[1:], 0)
        w = jnp.where((jidx == 0) | (jidx > ns - N_LOCAL), jnp.inf, imp_ref[g])

        def on_pick(r, first, pick, g=g):
            idx_ref[g, r:r + 1, :] = first

        _topk_rows(w, k, on_pick)


def _sample_topk(imp_t, k):
    _, ns, nb = imp_t.shape
    return pl.pallas_call(
        functools.partial(_sample_topk_body, k=k), grid=(1,),
        in_specs=[_full(imp_t.shape)], out_specs=_full((NSA_KV, SEL_TOPK, nb)),
        out_shape=jax.ShapeDtypeStruct((NSA_KV, SEL_TOPK, nb), I32),
        compiler_params=_params(1), name="sample_topk")(imp_t)


def _attend_with_self(qg, keys_t, vals_t, k_self, v_self, mask):
    s = _dot(qg, keys_t)
    if mask is not None:
        s = jnp.where(mask, s, -jnp.inf)
    s_self = jnp.sum(qg.astype(F32) * k_self.astype(BF16).astype(F32), axis=-1, keepdims=True)
    m = jnp.maximum(jnp.max(s, axis=-1, keepdims=True), s_self)
    p = jnp.exp(s - m)
    p_self = jnp.exp(s_self - m)
    den = jnp.sum(p, axis=-1, keepdims=True) + p_self
    num = _dot_nt(p.astype(BF16), vals_t) + p_self.astype(BF16).astype(F32) * v_self.astype(BF16).astype(F32)
    return num / den


def _sample_attn_body(idx_ref, pt_ref, *refs, n_sel):
    n_blk = NSA_KV * n_sel
    blocks = refs[:n_blk]
    (win_ref, q_ref, kvn_ref, gate_ref, oc_ref, mq_ref, mem_ref,
     ob_ref, om_ref, wn_ref, ks_ref, vs_ref) = refs[n_blk:]
    b = pl.program_id(0)
    row = lax.broadcasted_iota(I32, (NSA_HEADS, NSA_DH), 0)
    q8 = q_ref[0][:, :NSA_DH]
    qg = [jnp.where((row >= NSA_REP) == (g == 1), q8, jnp.zeros((), BF16)) for g in range(NSA_KV)]
    kvn = kvn_ref[0]

    def new_row(kind, g):
        lo = kind * NSA_KVW + g * NSA_DH
        return kvn[:, lo:lo + NSA_DH]

    lane = lax.broadcasted_iota(I32, (1, n_sel * LANES), 1)
    upper = (lane & SEL_BLOCK) != 0
    slot = lax.shift_right_logical(lane, 7)
    o_sel = []
    for g in range(NSA_KV):
        want_upper = jnp.zeros((1, n_sel * LANES), I32)
        for k in range(n_sel):
            blk = blocks[g * n_sel + k]
            ks_ref[g, :, k * LANES:(k + 1) * LANES] = blk[0, g * NSA_DH:(g + 1) * NSA_DH, :].astype(BF16)
            vs_ref[g, :, k * LANES:(k + 1) * LANES] = blk[0, NSA_KVW + g * NSA_DH:NSA_KVW + (g + 1) * NSA_DH, :].astype(BF16)
            j = idx_ref[b * (NSA_KV * SEL_TOPK) + g * SEL_TOPK + k]
            want_upper = jnp.where(slot == k, j & 1, want_upper)
        o_sel.append(_attend_with_self(qg[g], ks_ref[g], vs_ref[g], new_row(2, g), new_row(3, g),
                                       upper == (want_upper != 0)))
    o_s = jnp.where(row < NSA_REP, o_sel[0], o_sel[1])

    win = win_ref[0]
    wbuf = win.shape[1]
    wcol = lax.broadcasted_iota(I32, (1, wbuf), 1)
    o_win = []
    for g in range(NSA_KV):
        kt = win[g * NSA_DH:(g + 1) * NSA_DH].astype(BF16)
        vt = win[NSA_KVW + g * NSA_DH:NSA_KVW + (g + 1) * NSA_DH].astype(BF16)
        o_win.append(_attend_with_self(qg[g], kt, vt, new_row(4, g), new_row(5, g), wcol >= 1))
    o_w = jnp.where(row < NSA_REP, o_win[0], o_win[1])
    rows_w = 2 * NSA_KVW
    eye = lax.broadcasted_iota(I32, (rows_w, rows_w), 0) == lax.broadcasted_iota(I32, (rows_w, rows_w), 1)
    new_col = jnp.sum(jnp.where(eye, kvn[:, 4 * NSA_KVW:], 0.0), axis=-1, keepdims=True)
    wn_ref[0] = jnp.where(lax.broadcasted_iota(I32, win.shape, 1) == wbuf - 1, new_col, pltpu.roll(win, wbuf - 1, 1))

    gate = gate_ref[0]
    grow = lax.broadcasted_iota(I32, (NSA_HEADS, LANES), 0)
    glane = lax.broadcasted_iota(I32, (NSA_HEADS, LANES), 1)
    oc = jnp.where((glane >= NSA_DH) == (grow >= NSA_REP), oc_ref[0], 0.0)
    o = jnp.zeros((NSA_HEADS, NSA_DH), F32)
    for j, branch in enumerate((oc[:, :NSA_DH] + oc[:, NSA_DH:], o_s, o_w)):
        gj = jnp.sum(jnp.where(glane == j * NSA_HEADS + grow, gate, 0.0), axis=-1, keepdims=True)
        o = o + gj * branch
    ob_ref[0] = o

    mq = mq_ref[0]
    per_tok = 2 * MEM_HEADS
    for h in range(MEM_HEADS):
        qh = jnp.broadcast_to(mq[:, h * MEM_DH:(h + 1) * MEM_DH], (NSA_HEADS, MEM_DH)).astype(BF16)
        km = mem_ref[0, pl.ds(h, MEM_TOKENS, stride=per_tok), :].astype(BF16)
        vm = mem_ref[0, pl.ds(MEM_HEADS + h, MEM_TOKENS, stride=per_tok), :].astype(BF16)
        p = _softmax(_dot_nt(qh, km) * (MEM_DH ** -0.5))
        om_ref[0, :, h * MEM_DH:(h + 1) * MEM_DH] = _dot(p.astype(BF16), vm)[0:1]


def _sample_attn(idx_flat, pt_flat, cache_t, win_t, qrep, kvn, gate, oc, mq, memkv, n_pages, n_sel):
    nb, rows_w, wbuf = win_t.shape

    def sel_block(g, k):
        def index(b, idx, pt):
            j = idx[b * (NSA_KV * SEL_TOPK) + g * SEL_TOPK + k]
            return (pt[b * n_pages + lax.shift_right_logical(j, 1)], 1, 0)
        return pl.BlockSpec((1, 2 * NSA_KVW, 128), index)

    def per_seq(shape):
        return pl.BlockSpec((1,) + shape, lambda b, idx, pt: (b,) + (0,) * len(shape))

    grid_spec = pltpu.PrefetchScalarGridSpec(
        num_scalar_prefetch=2, grid=(nb,),
        in_specs=[sel_block(g, k) for g in range(NSA_KV) for k in range(n_sel)] + [
            per_seq((rows_w, wbuf)), per_seq((NSA_HEADS, LANES)), per_seq((1, 768)), per_seq((1, LANES)),
            per_seq((NSA_HEADS, LANES)), per_seq((1, MEM_Q)), per_seq(memkv.shape[1:])],
        out_specs=[per_seq((NSA_HEADS, NSA_DH)), per_seq((1, MEM_Q)), per_seq((rows_w, wbuf))],
        scratch_shapes=[pltpu.VMEM((NSA_KV, NSA_DH, n_sel * LANES), BF16)] * 2)
    return pl.pallas_call(
        functools.partial(_sample_attn_body, n_sel=n_sel), grid_spec=grid_spec,
        out_shape=[jax.ShapeDtypeStruct((nb, NSA_HEADS, NSA_DH), F32), jax.ShapeDtypeStruct((nb, 1, MEM_Q), F32),
                   jax.ShapeDtypeStruct((nb, rows_w, wbuf), F32)],
        compiler_params=_params(1), name="sample_attn")(
            idx_flat, pt_flat, *([cache_t] * (NSA_KV * n_sel)), win_t, qrep, kvn, gate, oc, mq, memkv)


def _merge_body(x_ref, oa_ref, ob_ref, om_ref, g1_ref, wmg_ref, wb_ref, wo_ref, g2_ref, rwh_ref, rwl_ref, rb_ref,
                h1_ref, hn_ref, gt_ref):
    x = x_ref[...]
    ab = _rms(x, g1_ref[...]).astype(BF16)
    s = None
    for k, o_ref in enumerate((oa_ref, ob_ref, om_ref)):
        gk = jax.nn.sigmoid(_dot(ab, wmg_ref[:, k * D_MODEL:(k + 1) * D_MODEL]))
        term = gk * _dot(o_ref[...], wb_ref[k])
        s = term if s is None else s + term
    h1 = x + _dot(s.astype(BF16), wo_ref[...])
    h1_ref[...] = h1
    hn = _rms(h1, g2_ref[...])
    hi, lo = _split_hi_lo(hn)
    hn_ref[...] = hi

    logit = _dot_nt(rwh_ref[...], hi) + _dot_nt(rwh_ref[...], lo) + _dot_nt(rwl_ref[...], hi) + rb_ref[...]
    picked = [jnp.zeros(logit.shape, F32)]

    def on_pick(r, first, pick):
        picked[0] = jnp.where(pick, 1.0, picked[0])

    _topk_rows(logit, TOP_K, on_pick)
    top = jnp.max(logit, axis=0, keepdims=True)
    ex = jnp.where(picked[0] > 0.0, jnp.exp(logit - top), 0.0)
    gt_ref[...] = ex / jnp.sum(ex, axis=0, keepdims=True)


def _merge(x, oa, ob, om, g1, wmg, wb, wo, g2, rwh, rwl, rb, tm):
    n = x.shape[0]

    def row(width):
        return pl.BlockSpec((tm, width), lambda i: (i, 0))

    return pl.pallas_call(
        _merge_body, grid=(n // tm,),
        in_specs=[row(D_MODEL), row(512), row(512), row(512), _full((1, D_MODEL)), _const(wmg.shape), _const(wb.shape),
                  _const(wo.shape), _full((1, D_MODEL)), _full(rwh.shape), _full(rwl.shape), _full(rb.shape)],
        out_specs=[row(D_MODEL), row(D_MODEL), pl.BlockSpec((N_EXPERTS, tm), lambda i: (0, i))],
        out_shape=[jax.ShapeDtypeStruct((n, D_MODEL), F32), jax.ShapeDtypeStruct((n, D_MODEL), BF16),
                   jax.ShapeDtypeStruct((N_EXPERTS, n), F32)],
        compiler_params=_params(1), name="merge")(x, oa, ob, om, g1, wmg, wb, wo, g2, rwh, rwl, rb)


def _moe_body(cnt_ref, hn_ref, h1_ref, gt_ref, u_ref, wgu_ref, bgu_ref, wd_ref, bd_ref, fg_ref, y_ref,
              rank_ref, *, sub, cap):
    i, e = pl.program_id(0), pl.program_id(1)
    n_sub = hn_ref.shape[0] // sub

    @pl.when(e == 0)
    def _():
        for s in range(n_sub):
            cols = slice(s * sub, (s + 1) * sub)
            sel = jnp.where(gt_ref[:, cols] > 0.0, 1.0, 0.0).astype(BF16)
            rank_ref[:, cols] = _dot(sel, u_ref[...])
        y_ref[...] = h1_ref[...]

    slot = lax.broadcasted_iota(I32, (cap, sub), 0).astype(F32)
    spans = [slice(s * sub, (s + 1) * sub) for s in range(n_sub)]
    g_rows = [gt_ref[pl.ds(e, 1), sp] for sp in spans]
    r_rows = [rank_ref[pl.ds(e, 1), sp] for sp in spans]

    def chunk(ch, carry):
        want = slot + (ch * cap).astype(F32)
        hits = [(g_rows[s] > 0.0) & (r_rows[s] == want) for s in range(n_sub)]
        onehots = [jnp.where(h, 1.0, 0.0).astype(BF16) for h in hits]
        xe = jnp.concatenate([_dot(onehots[s], hn_ref[spans[s], :]).astype(BF16) for s in range(n_sub)], axis=0)
        gu = _dot(xe, wgu_ref[0]) + bgu_ref[0]
        gate = jnp.minimum(gu[:, :D_EXPERT], SWIGLU_LIMIT)
        up = jnp.clip(gu[:, D_EXPERT:], -SWIGLU_LIMIT, SWIGLU_LIMIT)
        act = (up + 1.0) * gate * jax.nn.sigmoid(SWIGLU_ALPHA * gate)
        ye = _dot(act.astype(BF16), wd_ref[0]) + bd_ref[0]
        for s in range(n_sub):
            w = jnp.sum(jnp.where(hits[s], g_rows[s], 0.0), axis=-1, keepdims=True)
            y_ref[spans[s], :] += _dot_tn(onehots[s], (ye[s * cap:(s + 1) * cap] * w).astype(BF16))
        return carry

    most = cnt_ref[i * n_sub * N_EXPERTS + e]
    for s in range(1, n_sub):
        most = jnp.maximum(most, cnt_ref[(i * n_sub + s) * N_EXPERTS + e])
    lax.fori_loop(0, (most + cap - 1) // cap, chunk, 0)

    @pl.when(e == N_EXPERTS - 1)
    def _():
        y_ref[...] = _rms(y_ref[...], fg_ref[...])


def _moe(counts, hn, h1, gate_t, upper, wgu, bgu, wd, bd, fg, tile, sub, cap):
    n = hn.shape[0]
    once = pl.Buffered(1)
    grid_spec = pltpu.PrefetchScalarGridSpec(
        num_scalar_prefetch=1, grid=(n // tile, N_EXPERTS),
        in_specs=[pl.BlockSpec((tile, D_MODEL), lambda i, e, c: (i, 0)),
                  pl.BlockSpec((tile, D_MODEL), lambda i, e, c: (i, 0), pipeline_mode=once),
                  pl.BlockSpec((N_EXPERTS, tile), lambda i, e, c: (0, i)),
                  pl.BlockSpec((sub, sub), lambda i, e, c: (0, 0), pipeline_mode=once),
                  pl.BlockSpec((1, D_MODEL, 2 * D_EXPERT), lambda i, e, c: (e, 0, 0)),
                  pl.BlockSpec((1, 1, 2 * D_EXPERT), lambda i, e, c: (e, 0, 0)),
                  pl.BlockSpec((1, D_EXPERT, D_MODEL), lambda i, e, c: (e, 0, 0)),
                  pl.BlockSpec((1, 1, D_MODEL), lambda i, e, c: (e, 0, 0)),
                  pl.BlockSpec((1, D_MODEL), lambda i, e, c: (0, 0))],
        out_specs=pl.BlockSpec((tile, D_MODEL), lambda i, e, c: (i, 0)),
        scratch_shapes=[pltpu.VMEM((N_EXPERTS, tile), F32)])
    return pl.pallas_call(
        functools.partial(_moe_body, sub=sub, cap=cap), grid_spec=grid_spec,
        out_shape=jax.ShapeDtypeStruct((n, D_MODEL), F32),
        compiler_params=_params(2), name="moe")(counts, hn, h1, gate_t, upper, wgu, bgu, wd, bd, fg)


def _gate_columns(order):
    idx = np.zeros((LANES,), np.int32)
    ok = np.zeros((LANES,), bool)
    for c, (h, j) in enumerate(order):
        idx[c], ok[c] = C_NG + h * 3 + j, True
    return idx, ok


def _prep_in_weights(w_in):
    def sect(lo, hi):
        return w_in[:, lo:hi]

    def gates(order):
        idx, ok = _gate_columns(order)
        return jnp.where(jnp.asarray(ok)[None, :], w_in[:, idx], 0.0)

    w_q = sect(C_Q, C_KV) * (NSA_DH ** -0.5)
    g_prompt = [gates([(NSA_REP * g + r, j) for j in range(3) for r in range(NSA_REP)]) for g in range(NSA_KV)]
    w_prompt = jnp.concatenate([sect(C_U, C_Q), w_q, sect(C_KV, C_NG)] + g_prompt + [sect(C_MQ, C_MG)], axis=1)
    q_rep = jnp.concatenate([w_q.reshape(D_MODEL, NSA_HEADS, 1, NSA_DH)] * 2, axis=2).reshape(D_MODEL, 2 * NSA_Q)
    g_sample = gates([(h, j) for j in range(3) for h in range(NSA_HEADS)])
    w_sample = jnp.concatenate([sect(C_U, C_Q), q_rep, sect(C_KV, C_NG), g_sample, sect(C_MQ, C_MG)], axis=1)
    w_kv_t = w_in.T[C_KV:C_NG]
    return w_prompt.astype(BF16), w_sample.astype(BF16), sect(C_MG, w_in.shape[1]).astype(BF16), w_kv_t.astype(BF16)


def _prep_compress(w1, b1, w2):
    eye = jnp.eye(NSA_KV, dtype=F32)
    w = w1.reshape(2, 2, CMP_STRIDE, NSA_DH, NSA_DH)
    wk = jnp.einsum('khjde,xg->kjxdhge', w, eye).reshape(2, CMP_STRIDE * LANES, 2 * LANES)
    b = jnp.concatenate([b1, b1], axis=-1).reshape(2, 1, LANES)
    w2d = jnp.einsum('kde,xg->kxdge', w2, eye).reshape(2, LANES, LANES)
    return wk.astype(BF16), b, w2d.astype(BF16)


def _cmp_to_sel_t(ncp, ns):
    cs = np.arange(ncp) * CMP_STRIDE
    ss = np.arange(ns) * SEL_BLOCK
    m = (cs[None, :] <= ss[:, None] + SEL_BLOCK - 1) & (cs[None, :] + CMP_BLOCK - 1 >= ss[:, None])
    m[:, ncp - 1] = False
    return m.astype(np.float32)


def _block_onehot(t):
    e = (np.arange(t // SEL_BLOCK)[:, None] == (np.arange(t) // SEL_BLOCK)[None, :]).astype(np.float32)
    return e.reshape(t // SEL_BLOCK, t // KV_TILE, KV_TILE).transpose(1, 0, 2)


def _moe_group(h1_and_gates, weights, tile, sub, cap):
    h1, hn, gate_t = h1_and_gates
    n = h1.shape[0]
    counts = jnp.sum((gate_t > 0.0).reshape(N_EXPERTS, n // sub, sub), axis=-1, dtype=I32).T.reshape(-1)
    upper = jnp.asarray(np.triu(np.ones((sub, sub), np.float32), 1), BF16)
    return _moe(counts, hn, h1, gate_t, upper, *weights, tile, sub, cap)


def kernel(x_prompt, x_sample, mem_prompt, cache_nsa_kv, cache_nsa_win, cache_mem_kv, page_table,
           norm1_g, w_in, sgu_ln_g, sgu_ln_b, sgu_w, sgu_b, cmp_w1, cmp_b1, cmp_w2, w_mem_kv,
           w_branch, w_out, norm2_g, router_w, router_b, exp_w_gate_up, exp_b_gate_up,
           exp_w_down, exp_b_down, final_norm_g):
    assert norm1_g.shape[0] == 1, "one layer"
    bp, t, _ = x_prompt.shape
    nb = x_sample.shape[0]
    n_pool, page = cache_nsa_kv.shape[1:3]
    n_pages = page_table.shape[1]
    past = n_pages * page
    wbuf = cache_nsa_win.shape[2]
    assert page == 128 and wbuf == WINDOW and n_pages % PAGES_PER_STEP == 0 and t % KV_TILE == 0 and t >= WINDOW + Q_BLOCK

    g1, g2, fg = norm1_g[0][None], norm2_g[0][None], final_norm_g[None]
    lng, lnb = sgu_ln_g[0][None], sgu_ln_b[0][None]
    w_prompt, w_sample, w_mg, w_kv_t = _prep_in_weights(w_in[0])
    wk, b1c, w2c = _prep_compress(cmp_w1[0], cmp_b1[0], cmp_w2[0])
    wb, wo = w_branch[0].astype(BF16), w_out[0].astype(BF16)
    rwh, rwl = _split_hi_lo(router_w[0].T)
    rb = router_b[0][:, None]
    moe_w = (exp_w_gate_up[0].astype(BF16), exp_b_gate_up[0][:, None, :], exp_w_down[0].astype(BF16),
             exp_b_down[0][:, None, :], fg)
    merge_w = (g1, w_mg, wb, wo, g2, rwh, rwl, rb)

    mkv, mkv_bf = _mem_kv(mem_prompt.reshape(bp * MEM_TOKENS, D_MODEL), w_mem_kv[0].astype(BF16))
    oa, qh, kv4t, kvwt, kt, v, kcv_rows, gates, om = _inproj_prompt(
        x_prompt, g1, w_prompt, w_kv_t, lng, lnb, sgu_w[0], sgu_b[0].T, mkv_bf.reshape(bp, MEM_TOKENS, 2 * MEM_Q))
    ncp, ns = t // CMP_STRIDE, t // SEL_BLOCK
    r = jnp.stack([kcv_rows[:, :, k * LANES:(k + 1) * LANES].reshape(bp, ncp, CMP_STRIDE * LANES) for k in range(2)])
    kcv = _compress_prompt(r, wk, b1c, w2c)
    mt = jnp.asarray(_cmp_to_sel_t(ncp, ns), BF16)
    ob = _nsa_pair(qh, gates, kcv, kt, v, mt, jnp.asarray(_block_onehot(t), BF16))
    n = bp * t
    x2 = x_prompt.reshape(n, D_MODEL)
    y_prompt = _moe_group(_merge(x2, oa.reshape(n, 512), ob.reshape(n, 512), om.reshape(n, 512), *merge_w, tm=min(MERGE_ROWS, n)),
                          moe_w, min(MOE_BLOCK, n), min(MOE_TILE, n), MOE_CAP)

    xs = x_sample.reshape(nb, D_MODEL)
    wsc = jnp.repeat(sgu_w[0][:, 0, 0], SGU_GROUP_DIM)[None]
    bsc = jnp.repeat(sgu_b[0][:, 0], SGU_GROUP_DIM)[None]
    oa_s, z2_s, q_s, kv_s, gate_s, mq_s = _inproj_sample(xs, g1, w_sample, lng, lnb, wsc, bsc)
    qrep = q_s.reshape(nb, NSA_HEADS, LANES)
    pt_flat = page_table.reshape(-1)
    ncp_s, ns_s = past // CMP_STRIDE, past // SEL_BLOCK
    m_s = jnp.asarray(_cmp_to_sel_t(ncp_s, ns_s).T, BF16)
    cache_t = jnp.transpose(cache_nsa_kv[0], (0, 2, 3, 4, 1)).reshape(n_pool, 4 * NSA_KVW, page)
    win_t = jnp.transpose(cache_nsa_win[0], (0, 2, 3, 4, 1)).reshape(nb, 2 * NSA_KVW, wbuf)
    oc_s, imp_s = _sample_cmp(pt_flat, cache_t, qrep, wk, b1c, w2c, m_s, n_pages)
    n_sel = min(SEL_TOPK, ns_s + 1) - 1
    idx = _sample_topk(imp_s.transpose(1, 2, 0), n_sel)
    idx_flat = idx.transpose(2, 0, 1).reshape(-1)
    ob_s, om_s, win_new = _sample_attn(
        idx_flat, pt_flat, cache_t, win_t, qrep, kv_s[:, None, :], gate_s[:, None, :], oc_s, mq_s[:, None, :],
        cache_mem_kv[0].reshape(nb, MEM_TOKENS * 2 * MEM_HEADS, MEM_DH), n_pages, n_sel)
    y_sample = _moe_group(_merge(xs, oa_s, ob_s.reshape(nb, NSA_Q).astype(BF16), om_s.reshape(nb, MEM_Q).astype(BF16),
                                 *merge_w, tm=nb), moe_w, nb, nb, 32)

    def positions_last(x_t, n_kinds):
        lead, _, npos = x_t.shape
        return jnp.transpose(x_t.reshape(lead, n_kinds, NSA_KV, NSA_DH, npos), (0, 4, 1, 2, 3))[None]

    return (y_prompt.reshape(bp, t, D_MODEL), y_sample.reshape(nb, 1, D_MODEL),
            positions_last(kv4t, 4), positions_last(kvwt[:, :, t - WINDOW:], 2),
            mkv.reshape(1, bp, MEM_TOKENS, 2, MEM_HEADS, MEM_DH), kv_s[:, :512].reshape(1, nb, 1, 4, NSA_KV, NSA_DH),
            positions_last(win_new, 2), z2_s.reshape(1, nb, 1, SGU_DIM))
```
